```python
import math
import jax
import jax.numpy as jnp
from jax import lax
import numpy as np

D_MODEL = 2048
BATCH = 4
SEQ = 2048
DEPTH = 2
DEC_BATCH = 8
DEC_SEQ = 4
PAST_LEN = 16384
PAGE_SIZE = 128

N_META = 16
EPS = 1e-6
ROPE_THETA = 10000.0
QBLK = 128
HEAD_UNIT = 256
H_A = D_MODEL // HEAD_UNIT
DK_A = 64
DV_A = 2 * DK_A
W_A = H_A * DV_A
C_B = D_MODEL - W_A
CONV_W = 31
H_C = D_MODEL // HEAD_UNIT
D_C = 128
W_C = H_C * D_C
H_D = D_MODEL // HEAD_UNIT
DK_D = 128
DV_D = (D_MODEL - W_C) // H_D
W_D = H_D * DV_D
HGRN_CHUNK = 64
D_FF = ((8 * D_MODEL // 3 + 255) // 256) * 256
N_EXPERTS = 8
TOP_K = 2
D_FF_E = 7 * D_MODEL // 2
MOE_BLOCK = 128
DEC_MOE_BLOCK = 8
POOL_NUM = 5
POOL_DEN = 4

kernel_name = 'hybrid_diffattn_conformer_stickbreak_hgrn2_step'


def rmsnorm(x, g):
    xf = x.astype(jnp.float32)
    y = xf * lax.rsqrt(jnp.mean(xf * xf, axis=-1, keepdims=True) + EPS)
    return (y * g.astype(jnp.float32)).astype(x.dtype)


def layernorm(x, g, b):
    xf = x.astype(jnp.float32)
    xc = xf - jnp.mean(xf, axis=-1, keepdims=True)
    y = xc * lax.rsqrt(jnp.mean(xc * xc, axis=-1, keepdims=True) + EPS)
    return (y * g.astype(jnp.float32) + b.astype(jnp.float32)).astype(x.dtype)


def rope(x, pos):
    d = x.shape[-1]
    half = d // 2
    inv = ROPE_THETA ** (-jnp.arange(half, dtype=jnp.float32) * 2.0 / d)
    ang = pos.astype(jnp.float32)[:, None] * inv[None, :]
    cos = jnp.cos(ang)[:, None, None, :]
    sin = jnp.sin(ang)[:, None, None, :]
    xf = x.astype(jnp.float32)
    x1, x2 = xf[..., :half], xf[..., half:]
    return jnp.concatenate([x1 * cos - x2 * sin, x1 * sin + x2 * cos], axis=-1).astype(x.dtype)


def lambda_init(layer):
    return 0.8 - 0.6 * math.exp(-0.3 * layer)


def diff_attend(q, q_pos, k, v, k_pos, lam):
    s = jnp.einsum('bqhcd,bkhcd->bhcqk', q, k, preferred_element_type=jnp.float32) * (DK_A ** -0.5)
    s = jnp.where(k_pos[None, :] <= q_pos[:, None], s, -jnp.inf)
    p = jax.nn.softmax(s, axis=-1)
    w = p[:, :, 0] - lam * p[:, :, 1]
    return jnp.einsum('bhqk,bkhe->bqhe', w.astype(v.dtype), v)


def sb_attend(q, q_pos, k, v, k_pos):
    z = jnp.einsum('bqhd,bkhd->bhqk', q, k, preferred_element_type=jnp.float32) * (D_C ** -0.5)
    valid = k_pos[None, :] < q_pos[:, None]
    log_1m = jnp.where(valid, jax.nn.log_sigmoid(-z), 0.0)
    between = lax.cumsum(log_1m, axis=3, reverse=True) - log_1m
    a = jnp.where(valid, jnp.exp(jax.nn.log_sigmoid(z) + between), 0.0)
    return jnp.einsum('bhqk,bkhd->bqhd', a.astype(v.dtype), v)


def sweep_queries(attend, q, q_pos):
    b = q.shape[0]
    lead = attend(q[:, :N_META], q_pos[:N_META])
    rest = q[:, N_META:]
    n = rest.shape[1] // QBLK
    qb = jnp.moveaxis(rest.reshape((b, n, QBLK) + rest.shape[2:]), 1, 0)
    pb = q_pos[N_META:].reshape(n, QBLK)
    ob = lax.map(lambda qp: attend(qp[0], qp[1]), (qb, pb))
    ob = jnp.moveaxis(ob, 0, 1)
    ob = ob.reshape((b, n * QBLK) + ob.shape[3:])
    return jnp.concatenate([lead, ob], axis=1)


def hgrn_chunk(s, q, k, logf, v):
    c = q.shape[1]
    cb = jnp.cumsum(logf, axis=1)
    o_inter = jnp.einsum('bthd,bhde->bthe', q * jnp.exp(cb), s)
    causal = jnp.tril(jnp.ones((c, c), dtype=bool))
    rel = cb[:, :, None] - cb[:, None, :]
    decay = jnp.exp(jnp.where(causal[None, :, :, None, None], rel, -jnp.inf))
    a = jnp.einsum('bthd,btshd,bshd->bhts', q, decay, k)
    o_intra = jnp.einsum('bhts,bshe->bthe', a, v)
    last = cb[:, -1]
    s_new = jnp.exp(last)[..., None] * s + jnp.einsum('bshd,bshe->bhde', k * jnp.exp(last[:, None] - cb), v)
    return o_inter + o_intra, s_new


def hgrn_prompt(s0, q, k, logf, v):
    o_lead, s = hgrn_chunk(s0, q[:, :N_META], k[:, :N_META], logf[:, :N_META], v[:, :N_META])
    b = q.shape[0]
    n = (q.shape[1] - N_META) // HGRN_CHUNK

    def blocks(t):
        r = t[:, N_META:]
        return jnp.moveaxis(r.reshape((b, n, HGRN_CHUNK) + r.shape[2:]), 1, 0)

    def step(state, xs):
        o, state = hgrn_chunk(state, *xs)
        return state, o

    s, ob = lax.scan(step, s, (blocks(q), blocks(k), blocks(logf), blocks(v)))
    ob = jnp.moveaxis(ob, 0, 1)
    ob = ob.reshape((b, n * HGRN_CHUNK) + ob.shape[3:])
    return jnp.concatenate([o_lead, ob], axis=1), s


def causal_dwconv(u, prefix, w, bias):
    xp = jnp.concatenate([prefix.astype(u.dtype), u], axis=1)
    y = lax.conv_general_dilated(xp, w[:, None, :].astype(u.dtype), window_strides=(1,), padding='VALID',
                                 dimension_numbers=('NWC', 'WIO', 'NWC'), feature_group_count=u.shape[-1])
    return y + bias.astype(u.dtype), xp[:, -(CONV_W - 1):]


def even_mixer(a, pos, k_past, v_past, conv_prefix, p, layer, prompt):
    bsz, length, _ = a.shape
    proj = a @ p['w_in0']
    q, k, v, gl = jnp.split(proj, [W_A, 2 * W_A, 3 * W_A], axis=-1)
    q = rope(q.reshape(bsz, length, H_A, 2, DK_A), pos)
    k = rope(k.reshape(bsz, length, H_A, 2, DK_A), pos)
    v = v.reshape(bsz, length, H_A, DV_A)
    k_all = jnp.concatenate([k_past.astype(k.dtype), k], axis=1)
    v_all = jnp.concatenate([v_past.astype(v.dtype), v], axis=1)
    k_pos = jnp.arange(k_all.shape[1])
    f32 = jnp.float32
    lam = (jnp.exp(jnp.sum(p['lambda_q1'].astype(f32) * p['lambda_k1'].astype(f32)))
           - jnp.exp(jnp.sum(p['lambda_q2'].astype(f32) * p['lambda_k2'].astype(f32))) + lambda_init(layer))
    attend = lambda qq, pp: diff_attend(qq, pp, k_all, v_all, k_pos, lam)
    o = sweep_queries(attend, q, pos) if prompt else attend(q, pos)
    o = rmsnorm(o, p['subln_g0']) * (1.0 - lambda_init(layer))
    u = gl[..., :C_B] * jax.nn.sigmoid(gl[..., C_B:])
    c, new_prefix = causal_dwconv(u, conv_prefix, p['conv_w0'], p['conv_b0'])
    c = jax.nn.silu(layernorm(c, p['conv_ln_g0'], p['conv_ln_b0']))
    out = jnp.concatenate([o.reshape(bsz, length, W_A), c], axis=-1) @ p['w_out0']
    return out, [k.reshape(bsz, length, H_A, 2 * DK_A), v, new_prefix]


def odd_mixer(a, pos, k_past, v_past, s_past, p, layer, prompt):
    bsz, length, _ = a.shape
    proj = a @ p['w_in1']
    sizes = [W_C, W_C, W_C, H_D * DK_D, H_D * DK_D, W_D, W_D]
    splits = [int(s) for s in np.cumsum(sizes)[:-1]]
    q, k, v, hq, hf, hi, hg = jnp.split(proj, splits, axis=-1)
    q = q.reshape(bsz, length, H_C, D_C)
    k = k.reshape(bsz, length, H_C, D_C)
    v = v.reshape(bsz, length, H_C, D_C)
    k_all = jnp.concatenate([k_past.astype(k.dtype), k], axis=1)
    v_all = jnp.concatenate([v_past.astype(v.dtype), v], axis=1)
    k_pos = jnp.arange(k_all.shape[1])
    attend = lambda qq, pp: sb_attend(qq, pp, k_all, v_all, k_pos)
    sb = sweep_queries(attend, q, pos) if prompt else attend(q, pos)
    f32 = jnp.float32
    sm = jax.nn.softmax(p['hgrn_lb_logits'].astype(f32), axis=0)
    lb = (jnp.cumsum(sm, axis=0)[layer] - sm[0]).reshape(H_D, DK_D)
    f = lb + (1.0 - lb) * jax.nn.sigmoid(hf.reshape(bsz, length, H_D, DK_D).astype(f32))
    logf = jnp.log(f)
    kk = 1.0 - f
    qq = hq.reshape(bsz, length, H_D, DK_D).astype(f32)
    vv = hi.reshape(bsz, length, H_D, DV_D).astype(f32)
    s0 = s_past.astype(f32)
    if prompt:
        o, s_new = hgrn_prompt(s0, qq, kk, logf, vv)
    else:
        o, s_new = hgrn_chunk(s0, qq, kk, logf, vv)
    o = rmsnorm(o.astype(a.dtype), p['hgrn_norm_g1']) * jax.nn.silu(hg.reshape(bsz, length, H_D, DV_D))
    out = jnp.concatenate([sb.reshape(bsz, length, W_C), o.reshape(bsz, length, W_D)], axis=-1) @ p['w_out1']
    return out, [k, v, s_new.astype(a.dtype)]


def swiglu(x, wg, wu, wd):
    return (jax.nn.silu(x @ wg) * (x @ wu)) @ wd


def moe(x, router, wg, wu, wd, blk):
    bsz, length, d = x.shape
    t = x.reshape(-1, d)
    n_tok = t.shape[0]
    logits = (t @ router).astype(jnp.float32)
    top_v, top_i = lax.top_k(logits, TOP_K)
    gates = jax.nn.softmax(top_v, axis=-1)
    eid = top_i.reshape(-1)
    tok = jnp.repeat(jnp.arange(n_tok, dtype=jnp.int32), TOP_K)
    gate = gates.reshape(-1)
    order = jnp.argsort(eid)
    se = eid[order]
    counts = jnp.bincount(eid, length=N_EXPERTS)
    starts = jnp.cumsum(counts) - counts
    padded = (counts + blk - 1) // blk * blk
    pends = jnp.cumsum(padded)
    pstarts = pends - padded
    dest = pstarts[se] + (jnp.arange(n_tok * TOP_K) - starts[se])
    n_blk = (n_tok * TOP_K + N_EXPERTS * (blk - 1) + blk - 1) // blk
    cap = n_blk * blk
    slot_tok = jnp.zeros((cap,), jnp.int32).at[dest].set(tok[order])
    slot_gate = jnp.zeros((cap,), jnp.float32).at[dest].set(gate[order])
    blk_exp = jnp.minimum(jnp.sum(pends[None, :] <= (jnp.arange(n_blk) * blk)[:, None], axis=1), N_EXPERTS - 1)

    def run(args):
        e, idx = args
        xb = t[idx]
        return (jax.nn.silu(xb @ wg[e]) * (xb @ wu[e])) @ wd[e]

    yb = lax.map(run, (blk_exp, slot_tok.reshape(n_blk, blk))).reshape(cap, d)
    y = jnp.zeros_like(t).at[slot_tok].add(yb * slot_gate[:, None].astype(t.dtype))
    return y.reshape(bsz, length, d)


def run_trunk(x, pos, k0_past, v0_past, conv_prefix, k1_past, v1_past, s_past, p, prompt):
    h = x
    new = []
    for l in range(DEPTH):
        a = rmsnorm(h, p['norm_mix'][l])
        if l % 2 == 0:
            mix, st = even_mixer(a, pos, k0_past, v0_past, conv_prefix, p, l, prompt)
        else:
            mix, st = odd_mixer(a, pos, k1_past, v1_past, s_past, p, l, prompt)
        new.extend(st)
        h = h + mix
        a = rmsnorm(h, p['norm_ffn'][l])
        if l % 2 == 0:
            h = h + swiglu(a, p['ffn_gate0'], p['ffn_up0'], p['ffn_down0'])
        else:
            h = h + moe(a, p['router1'], p['moe_gate1'], p['moe_up1'], p['moe_down1'],
                        MOE_BLOCK if prompt else DEC_MOE_BLOCK)
    return rmsnorm(h, p['norm_final']), new


def setup_inputs(seed: int = 0) -> dict:
    key = jax.random.key(seed)
    ks = iter(jax.random.split(key, 48))

    def nrm(shape, scale=1.0):
        return jax.random.normal(next(ks), shape, jnp.float32) * scale

    n_pages = PAST_LEN // PAGE_SIZE
    n_phys = DEC_BATCH * n_pages * POOL_NUM // POOL_DEN
    page_table = jax.random.permutation(next(ks), n_phys)[:DEC_BATCH * n_pages].reshape(DEC_BATCH, n_pages).astype(jnp.int32)
    p0_width = 3 * W_A + 2 * C_B
    p1_width = 3 * W_C + 2 * H_D * DK_D + 2 * W_D
    return {
        'x_prompt': nrm((BATCH, SEQ, D_MODEL)),
        'x_sample': nrm((DEC_BATCH, DEC_SEQ, D_MODEL)),
        'cache_k0': nrm((n_phys, PAGE_SIZE, H_A, 2 * DK_A)),
        'cache_v0': nrm((n_phys, PAGE_SIZE, H_A, DV_A)),
        'state_conv0': nrm((DEC_BATCH, CONV_W - 1, C_B)),
        'cache_k1': nrm((n_phys, PAGE_SIZE, H_C, D_C)),
        'cache_v1': nrm((n_phys, PAGE_SIZE, H_C, D_C)),
        'state_hgrn1': nrm((DEC_BATCH, H_D, DK_D, DV_D), 0.5),
        'page_table': page_table,
        'meta_tokens': nrm((N_META, D_MODEL)),
        'norm_mix': 1.0 + nrm((DEPTH, D_MODEL), 0.02),
        'norm_ffn': 1.0 + nrm((DEPTH, D_MODEL), 0.02),
        'norm_final': 1.0 + nrm((D_MODEL,), 0.02),
        'w_in0': nrm((D_MODEL, p0_width), D_MODEL ** -0.5),
        'lambda_q1': nrm((DK_A,), 0.1),
        'lambda_k1': nrm((DK_A,), 0.1),
        'lambda_q2': nrm((DK_A,), 0.1),
        'lambda_k2': nrm((DK_A,), 0.1),
        'subln_g0': 1.0 + nrm((DV_A,), 0.02),
        'conv_w0': nrm((CONV_W, C_B), CONV_W ** -0.5),
        'conv_b0': nrm((C_B,), 0.02),
        'conv_ln_g0': 1.0 + nrm((C_B,), 0.02),
        'conv_ln_b0': nrm((C_B,), 0.02),
        'w_out0': nrm((W_A + C_B, D_MODEL), (W_A + C_B) ** -0.5),
        'ffn_gate0': nrm((D_MODEL, D_FF), D_MODEL ** -0.5),
        'ffn_up0': nrm((D_MODEL, D_FF), D_MODEL ** -0.5),
        'ffn_down0': nrm((D_FF, D_MODEL), D_FF ** -0.5),
        'w_in1': nrm((D_MODEL, p1_width), D_MODEL ** -0.5),
        'hgrn_lb_logits': nrm((DEPTH, H_D * DK_D), 0.1),
        'hgrn_norm_g1': 1.0 + nrm((DV_D,), 0.02),
        'w_out1': nrm((W_C + W_D, D_MODEL), (W_C + W_D) ** -0.5),
        'router1': nrm((D_MODEL, N_EXPERTS), D_MODEL ** -0.5),
        'moe_gate1': nrm((N_EXPERTS, D_MODEL, D_FF_E), D_MODEL ** -0.5),
        'moe_up1': nrm((N_EXPERTS, D_MODEL, D_FF_E), D_MODEL ** -0.5),
        'moe_down1': nrm((N_EXPERTS, D_FF_E, D_MODEL), D_FF_E ** -0.5),
    }


def reference(x_prompt, x_sample, cache_k0, cache_v0, state_conv0, cache_k1, cache_v1, state_hgrn1,
              page_table, meta_tokens, norm_mix, norm_ffn, norm_final, w_in0, lambda_q1, lambda_k1,
              lambda_q2, lambda_k2, subln_g0, conv_w0, conv_b0, conv_ln_g0, conv_ln_b0, w_out0,
              ffn_gate0, ffn_up0, ffn_down0, w_in1, hgrn_lb_logits, hgrn_norm_g1, w_out1, router1,
              moe_gate1, moe_up1, moe_down1):
    p = {
        'norm_mix': norm_mix, 'norm_ffn': norm_ffn, 'norm_final': norm_final,
        'w_in0': w_in0, 'lambda_q1': lambda_q1, 'lambda_k1': lambda_k1,
        'lambda_q2': lambda_q2, 'lambda_k2': lambda_k2, 'subln_g0': subln_g0,
        'conv_w0': conv_w0, 'conv_b0': conv_b0, 'conv_ln_g0': conv_ln_g0, 'conv_ln_b0': conv_ln_b0,
        'w_out0': w_out0, 'ffn_gate0': ffn_gate0, 'ffn_up0': ffn_up0, 'ffn_down0': ffn_down0,
        'w_in1': w_in1, 'hgrn_lb_logits': hgrn_lb_logits, 'hgrn_norm_g1': hgrn_norm_g1,
        'w_out1': w_out1, 'router1': router1, 'moe_gate1': moe_gate1, 'moe_up1': moe_up1,
        'moe_down1': moe_down1,
    }
    dt = x_prompt.dtype
    b = x_prompt.shape[0]
    meta = jnp.broadcast_to(meta_tokens.astype(dt)[None], (b, N_META, D_MODEL))
    xp = jnp.concatenate([meta, x_prompt], axis=1)
    pos_p = jnp.arange(xp.shape[1])
    yp, new_p = run_trunk(xp, pos_p,
                          jnp.zeros((b, 0, H_A, 2, DK_A), dt), jnp.zeros((b, 0, H_A, DV_A), dt),
                          jnp.zeros((b, CONV_W - 1, C_B), dt),
                          jnp.zeros((b, 0, H_C, D_C), dt), jnp.zeros((b, 0, H_C, D_C), dt),
                          jnp.zeros((b, H_D, DK_D, DV_D), jnp.float32), p, True)
    y_prompt = yp[:, N_META:]
    k0_p, v0_p, conv_p, k1_p, v1_p, hgrn_p = new_p
    db = x_sample.shape[0]
    k0_past = cache_k0[page_table].reshape(db, -1, H_A, 2, DK_A)
    v0_past = cache_v0[page_table].reshape(db, -1, H_A, DV_A)
    k1_past = cache_k1[page_table].reshape(db, -1, H_C, D_C)
    v1_past = cache_v1[page_table].reshape(db, -1, H_C, D_C)
    pos_s = k0_past.shape[1] + jnp.arange(x_sample.shape[1])
    y_sample, new_s = run_trunk(x_sample, pos_s, k0_past, v0_past, state_conv0, k1_past, v1_past,
                                state_hgrn1, p, False)
    k0_s, v0_s, conv_s, k1_s, v1_s, hgrn_s = new_s
    return (y_prompt, y_sample, k0_p, v0_p, conv_p, k1_p, v1_p, hgrn_p,
            k0_s, v0_s, conv_s, k1_s, v1_s, hgrn_s)
```

```python
import functools
import math

import jax
import jax.numpy as jnp
from jax import lax
from jax.experimental import pallas as pl
from jax.experimental.pallas import tpu as pltpu

F32 = jnp.float32
BF16 = jnp.bfloat16

N_META = 16
EPS = 1e-6
ROPE_THETA = 10000.0
N_HEADS = 8
HEAD_DIM = 128
W_MIX = N_HEADS * HEAD_DIM
DK_A = 64
CONV_W = 31
N_EXPERTS = 8
TOP_K = 2
HALO = 32

LANES = 128
SUBLANES = 8
TAIL_ROWS = 128
SAMPLE_ROWS = 8
V7X_VMEM_BYTES = 64 * 1024 * 1024
VMEM_LIMIT_BYTES = 56 * 1024 * 1024
PAGES_PER_STEP = 8


def _lambda_init(layer):
    return 0.8 - 0.6 * math.exp(-0.3 * layer)


def _cparams(semantics, vmem=VMEM_LIMIT_BYTES):
    return pltpu.CompilerParams(dimension_semantics=semantics, vmem_limit_bytes=vmem)


def _pick_tile(n, cap, mult):
    best = None
    for t in range(mult, min(n, cap) + 1, mult):
        if n % t == 0:
            best = t
    assert best is not None, (n, cap, mult)
    return best


def _dot(a, b):
    return jnp.dot(a, b, preferred_element_type=F32)


def _dot_nt(a, b):
    return lax.dot_general(a, b, (((1,), (1,)), ((), ())), preferred_element_type=F32)


def _sigmoid(x):
    return 1.0 / (1.0 + jnp.exp(-x))


def _log_sigmoid(x):
    return jnp.minimum(x, 0.0) - jnp.log(1.0 + jnp.exp(-jnp.abs(x)))


def _rms(x, g):
    return x * lax.rsqrt(jnp.mean(x * x, axis=-1, keepdims=True) + EPS) * g


def _split2(x):
    hi = x.astype(BF16)
    lo = (x - hi.astype(F32)).astype(BF16)
    return hi, lo


def _dot3(a, b):
    return _dot(a[0], b[0]) + _dot(a[1], b[0]) + _dot(a[0], b[1])


def _dot3_nt(a, b):
    return _dot_nt(a[0], b[0]) + _dot_nt(a[1], b[0]) + _dot_nt(a[0], b[1])


def _lhs_store(a_scr, a, precise):
    if precise:
        hi, lo = _split2(a)
        a_scr[0] = hi
        a_scr[1] = lo
    else:
        a_scr[0] = a.astype(BF16)


def _lhs_load(a_scr, precise):
    return (a_scr[0], a_scr[1]) if precise else a_scr[0]


def _mm(a, w, precise):
    if not precise:
        return _dot(a, w.astype(BF16))
    a_hi, a_lo = a
    w_hi, w_lo = _split2(w)
    return _dot(a_hi, w_hi) + _dot(a_lo, w_hi) + _dot(a_hi, w_lo)


def _proj_kernel(x_ref, g_ref, w_ref, *rest, precise):
    o_ref, a_scr = rest[-2:]

    @pl.when(pl.program_id(1) == 0)
    def _():
        _lhs_store(a_scr, _rms(x_ref[...], g_ref[...]), precise)

    o_ref[...] = _mm(_lhs_load(a_scr, precise), w_ref[...], precise)


def _outproj_kernel(x_ref, w_ref, r_ref, *rest, precise):
    o_ref, a_scr = rest[-2:]

    @pl.when(pl.program_id(1) == 0)
    def _():
        _lhs_store(a_scr, x_ref[...], precise)

    o_ref[...] = r_ref[...] + _mm(_lhs_load(a_scr, precise), w_ref[...], precise)


def _ffn_kernel(h_ref, g_ref, wg_ref, wu_ref, wd_ref, *rest, precise):
    o_ref, a_scr, acc_scr = rest[-3:]
    f = pl.program_id(1)

    @pl.when(f == 0)
    def _():
        _lhs_store(a_scr, _rms(h_ref[...], g_ref[...]), precise)
        acc_scr[...] = jnp.zeros_like(acc_scr)

    a = _lhs_load(a_scr, precise)
    gate = _mm(a, wg_ref[...], precise)
    up = _mm(a, wu_ref[...], precise)
    t = (gate * _sigmoid(gate)) * up
    acc_scr[...] += _mm(_split2(t) if precise else t.astype(BF16), wd_ref[...], precise)

    @pl.when(f == pl.num_programs(1) - 1)
    def _():
        o_ref[...] = h_ref[...] + acc_scr[...]


def _dense_call(kernel_fn, name, x, consts, weights, res, out_width, out_tile, tm, n_steps, prev, acc):
    r, d = x.shape
    precise = prev is not None
    if precise:
        tm, row0, n_row_tiles = TAIL_ROWS, (r - TAIL_ROWS) // TAIL_ROWS, 1
    else:
        row0, n_row_tiles = 0, r // tm
    if out_tile == out_width:
        out_map = lambda i, j: (row0 + i, 0)
    else:
        out_map = lambda i, j: (row0 + i, j)
    in_specs = [pl.BlockSpec((tm, d), lambda i, j: (row0 + i, 0))]
    args = [x]
    for arr in consts:
        in_specs.append(pl.BlockSpec(arr.shape, lambda i, j: (0, 0)))
        args.append(arr)
    for arr, shape, imap in weights:
        in_specs.append(pl.BlockSpec(shape, lambda i, j, imap=imap: imap(j)))
        args.append(arr)
    if res is not None:
        in_specs.append(pl.BlockSpec((tm, out_tile), out_map))
        args.append(res)
    aliases = {}
    if precise:
        in_specs.append(pl.BlockSpec(memory_space=pl.ANY))
        args.append(prev)
        aliases = {len(args) - 1: 0}
    return pl.pallas_call(
        functools.partial(kernel_fn, precise=precise),
        out_shape=jax.ShapeDtypeStruct((r, out_width), F32),
        grid=(n_row_tiles, n_steps),
        in_specs=in_specs,
        out_specs=pl.BlockSpec((tm, out_tile), out_map),
        scratch_shapes=[pltpu.VMEM((2 if precise else 1, tm, d), BF16)] + (
            [pltpu.VMEM((tm, d), F32)] if acc else []),
        input_output_aliases=aliases,
        compiler_params=_cparams(("parallel", "arbitrary")),
        name=name + ("_tail" if precise else ""),
    )(*args)


def _norm_matmul(x, g, w, tm, tn):
    d, n = w.shape
    run = lambda prev: _dense_call(_proj_kernel, "norm_matmul", x, [g.reshape(1, d)],
                                   [(w, (d, tn), lambda j: (0, j))], None, n, tn, tm, n // tn, prev, False)
    return run(run(None))


def _matmul_residual(x, w, res, tm, tn):
    d, n = w.shape
    run = lambda prev: _dense_call(_outproj_kernel, "matmul_residual", x, [],
                                   [(w, (d, tn), lambda j: (0, j))], res, n, tn, tm, n // tn, prev, False)
    return run(run(None))


def _ffn(h, g, wg, wu, wd, tm, tf):
    d, ff = wg.shape
    weights = [(wg, (d, tf), lambda j: (0, j)), (wu, (d, tf), lambda j: (0, j)), (wd, (tf, d), lambda j: (j, 0))]
    run = lambda prev: _dense_call(_ffn_kernel, "ffn_swiglu", h, [g.reshape(1, d)], weights, None,
                                   d, d, tm, ff // tf, prev, True)
    return run(run(None))


def _rope_tables(pos):
    half = DK_A // 2
    inv = ROPE_THETA ** (-jnp.arange(half, dtype=F32) * 2.0 / DK_A)
    ang = pos.astype(F32)[:, None] * inv[None, :]
    cos = jnp.cos(ang)
    sin = jnp.sin(ang)
    cos_t = jnp.concatenate([cos, cos, cos, cos], axis=1)
    sin_t = jnp.concatenate([-sin, sin, -sin, sin], axis=1)
    return cos_t, sin_t


def _rope(x, cos_t, sin_t):
    half = DK_A // 2
    lane = lax.broadcasted_iota(jnp.int32, x.shape, 1)
    first = (lane % DK_A) < half
    partner = jnp.where(first, pltpu.roll(x, LANES - half, axis=1), pltpu.roll(x, half, axis=1))
    return x * cos_t + partner * sin_t


def _kv_layout_kernel(*refs, rope, s):
    if rope:
        (k_ref, kt_ref, v_ref, vt_ref, cos_ref, sin_ref, cost_ref, sint_ref, kp_ref, vp_ref, ktr_ref) = refs
        k = _rope(k_ref[...], cos_ref[...], sin_ref[...])
        kt = _rope(kt_ref[...], cost_ref[...], sint_ref[...])
        ktr_ref[...] = kt
    else:
        (k_ref, kt_ref, v_ref, vt_ref, kp_ref, vp_ref) = refs
        k = k_ref[...]
        kt = kt_ref[...]
    kp_ref[0, 0:N_META, :] = kt[0:N_META]
    kp_ref[0, N_META:N_META + s, :] = k
    vp_ref[0, 0:N_META, :] = vt_ref[0:N_META, :]
    vp_ref[0, N_META:N_META + s, :] = v_ref[...]


def _kv_layout(proj, b, s, tables):
    tb = (b * s) // TAIL_ROWS
    kcol, vcol = W_MIX // HEAD_DIM, 2 * W_MIX // HEAD_DIM
    rope = tables is not None
    in_specs = [
        pl.BlockSpec((s, HEAD_DIM), lambda h, bb: (bb, kcol + h)),
        pl.BlockSpec((TAIL_ROWS, HEAD_DIM), lambda h, bb: (tb, kcol + h)),
        pl.BlockSpec((s, HEAD_DIM), lambda h, bb: (bb, vcol + h)),
        pl.BlockSpec((TAIL_ROWS, HEAD_DIM), lambda h, bb: (tb, vcol + h)),
    ]
    args = [proj, proj, proj, proj]
    out_shape = [jax.ShapeDtypeStruct((b, N_META + s, W_MIX), F32)] * 2
    out_specs = [pl.BlockSpec((1, N_META + s, HEAD_DIM), lambda h, bb: (bb, 0, h))] * 2
    if rope:
        cos_t, sin_t = tables
        in_specs += [
            pl.BlockSpec((s, LANES), lambda h, bb: (0, 0)),
            pl.BlockSpec((s, LANES), lambda h, bb: (0, 0)),
            pl.BlockSpec((TAIL_ROWS, LANES), lambda h, bb: (s // TAIL_ROWS, 0)),
            pl.BlockSpec((TAIL_ROWS, LANES), lambda h, bb: (s // TAIL_ROWS, 0)),
        ]
        args += [cos_t, sin_t, cos_t, sin_t]
        out_shape = out_shape + [jax.ShapeDtypeStruct((TAIL_ROWS, W_MIX), F32)]
        out_specs = out_specs + [pl.BlockSpec((TAIL_ROWS, HEAD_DIM), lambda h, bb: (0, h))]
    return pl.pallas_call(
        functools.partial(_kv_layout_kernel, rope=rope, s=s),
        out_shape=out_shape,
        grid=(N_HEADS, b),
        in_specs=in_specs,
        out_specs=out_specs,
        compiler_params=_cparams(("arbitrary", "arbitrary")),
        name="kv_layout_rope" if rope else "kv_layout",
    )(*args)


def _diff_lambda(lamp_ref, layer):
    lp = lamp_ref[...]
    d1 = jnp.sum(lp[0:1] * lp[1:2], axis=-1, keepdims=True)
    d2 = jnp.sum(lp[2:3] * lp[3:4], axis=-1, keepdims=True)
    return jnp.exp(d1) - jnp.exp(d2) + _lambda_init(layer)


def _diff_queries_f32(q):
    lane = lax.broadcasted_iota(jnp.int32, q.shape, 1)
    qs = q * (DK_A ** -0.5)
    q1 = jnp.where(lane < DK_A, qs, 0.0)
    q2 = jnp.where(lane >= DK_A, qs, 0.0)
    return jnp.concatenate([q1, q2], axis=0)


def _diff_queries(q):
    return _diff_queries_f32(q).astype(BF16)


def _diff_update(carry, qs, kc, vc, mask2):
    m, l, acc = carry
    sc = _dot_nt(qs, kc.astype(BF16))
    if mask2 is not None:
        sc = jnp.where(mask2, sc, -jnp.inf)
    m_new = jnp.maximum(m, jnp.max(sc, axis=-1, keepdims=True))
    alpha = jnp.exp(m - m_new)
    p = jnp.exp(sc - m_new)
    l = alpha * l + jnp.sum(p, axis=-1, keepdims=True)
    acc = alpha * acc + _dot(p.astype(BF16), vc.astype(BF16))
    return m_new, l, acc


def _diff_finish(carry, n, lam, g, layer):
    _, l, acc = carry
    o = acc[:n] / l[:n] - lam * (acc[n:] / l[n:])
    return _rms(o, g) * (1.0 - _lambda_init(layer))


def _upper_ones(n):
    j = lax.broadcasted_iota(jnp.int32, (n, n), 0)
    s = lax.broadcasted_iota(jnp.int32, (n, n), 1)
    return jnp.where(j > s, 1.0, 0.0).astype(BF16)


def _later_ones(n):
    s = lax.broadcasted_iota(jnp.int32, (n, n), 0)
    j = lax.broadcasted_iota(jnp.int32, (n, n), 1)
    return jnp.where(j > s, 1.0, 0.0).astype(BF16)


def _sb_update(carry, qb, kc, vc, mask, u):
    c, acc = carry
    z = _dot_nt(qb, kc.astype(BF16)) * (HEAD_DIM ** -0.5)
    ls = _log_sigmoid(z)
    l1m = ls - z
    if mask is not None:
        l1m = jnp.where(mask, l1m, 0.0)
    hi, lo = _split2(l1m)
    between = _dot(hi, u) + _dot(lo, u)
    a = jnp.exp(ls + between + c)
    if mask is not None:
        a = jnp.where(mask, a, 0.0)
    acc = acc + _dot(a.astype(BF16), vc.astype(BF16))
    c = c + jnp.sum(l1m, axis=-1, keepdims=True)
    return c, acc


def _attn_prompt_kernel(*refs, mode, tq, layer):
    if mode == "diff":
        q_ref, k_ref, v_ref, cos_ref, sin_ref, lamp_ref, g_ref, o_ref = refs
    else:
        q_ref, k_ref, v_ref, o_ref = refs
    i = pl.program_id(2)
    rows = lax.broadcasted_iota(jnp.int32, (tq, tq), 0)
    cols = lax.broadcasted_iota(jnp.int32, (tq, tq), 1)

    def chunk(j):
        start = pl.multiple_of(N_META + j * tq, SUBLANES)
        return k_ref[0, pl.ds(start, tq), :], v_ref[0, pl.ds(start, tq), :]

    k_meta = k_ref[0, 0:N_META, :]
    v_meta = v_ref[0, 0:N_META, :]

    if mode == "diff":
        qs = _diff_queries(_rope(q_ref[...], cos_ref[...], sin_ref[...]))
        carry = (jnp.full((2 * tq, 1), -jnp.inf, F32), jnp.zeros((2 * tq, 1), F32),
                 jnp.zeros((2 * tq, HEAD_DIM), F32))
        carry = _diff_update(carry, qs, k_meta, v_meta, None)

        def body(j, c):
            kc, vc = chunk(j)
            return _diff_update(c, qs, kc, vc, None)

        carry = lax.fori_loop(0, i, body, carry)
        kc, vc = chunk(i)
        causal = cols <= rows
        carry = _diff_update(carry, qs, kc, vc, jnp.concatenate([causal, causal], axis=0))
        o_ref[...] = _diff_finish(carry, tq, _diff_lambda(lamp_ref, layer), g_ref[...], layer)
    else:
        qb = q_ref[...].astype(BF16)
        u = _upper_ones(tq)
        carry = (jnp.zeros((tq, 1), F32), jnp.zeros((tq, HEAD_DIM), F32))
        kc, vc = chunk(i)
        carry = _sb_update(carry, qb, kc, vc, cols < rows, u)

        def body(it, c):
            kc, vc = chunk(i - 1 - it)
            return _sb_update(c, qb, kc, vc, None, u)

        carry = lax.fori_loop(0, i, body, carry)
        carry = _sb_update(carry, qb, k_meta, v_meta, None, _upper_ones(N_META))
        o_ref[...] = carry[1]


def _attn_prompt(proj, kp, vp, mode, layer, b, s, tq, extra):
    r = proj.shape[0]
    nq = s // tq
    in_specs = [
        pl.BlockSpec((tq, HEAD_DIM), lambda bb, h, i: (bb * nq + i, h)),
        pl.BlockSpec((1, N_META + s, HEAD_DIM), lambda bb, h, i: (bb, 0, h)),
        pl.BlockSpec((1, N_META + s, HEAD_DIM), lambda bb, h, i: (bb, 0, h)),
    ]
    args = [proj, kp, vp]
    if mode == "diff":
        cos_t, sin_t, lamp, g = extra
        in_specs += [
            pl.BlockSpec((tq, LANES), lambda bb, h, i: (i, 0)),
            pl.BlockSpec((tq, LANES), lambda bb, h, i: (i, 0)),
            pl.BlockSpec(lamp.shape, lambda bb, h, i: (0, 0)),
            pl.BlockSpec((1, HEAD_DIM), lambda bb, h, i: (0, 0)),
        ]
        args += [cos_t, sin_t, lamp, g]
    return pl.pallas_call(
        functools.partial(_attn_prompt_kernel, mode=mode, tq=tq, layer=layer),
        out_shape=jax.ShapeDtypeStruct((r, 2 * W_MIX), F32),
        grid=(b, N_HEADS, nq),
        in_specs=in_specs,
        out_specs=pl.BlockSpec((tq, HEAD_DIM), lambda bb, h, i: (bb * nq + i, h)),
        compiler_params=_cparams(("parallel", "parallel", "arbitrary")),
        name="attn_prompt_" + mode,
    )(*args)


def _column_rows(width):
    c = lax.broadcasted_iota(jnp.int32, (1, LANES), 1)
    return c % width


def _lane_to_rows(x):
    return jnp.transpose(jnp.broadcast_to(x, (LANES, LANES)))[:, 0:1]


def _block_diag_rows(pv):
    return jnp.concatenate(
        [pv[16 * h:16 * (h + 1), HEAD_DIM * h:HEAD_DIM * (h + 1)] for h in range(N_HEADS)], axis=0)


def _attn_tail_kernel(pt_ref, *refs, mode, layer, ds, n_steps):
    del pt_ref
    pps = PAGES_PER_STEP
    if mode == "diff":
        (q_ref, k_ref, v_ref, cos_ref, sin_ref, lamp_ref, g_ref) = refs[:7]
        refs = refs[7:]
    else:
        (q_ref, k_ref, v_ref) = refs[:3]
        refs = refs[3:]
    kc_refs = refs[:pps]
    vc_refs = refs[pps:2 * pps]
    _mix_in, o_ref, qbd_scr, m_scr, l_scr, acc_scr = refs[2 * pps:]
    b = pl.program_id(0)
    st = pl.program_id(1)
    row0 = pl.multiple_of(N_META + SAMPLE_ROWS * b, SUBLANES)
    key_row = lax.broadcasted_iota(jnp.int32, (LANES, LANES), 0)

    def head(x, h):
        return x[:, HEAD_DIM * h:HEAD_DIM * (h + 1)]

    @pl.when((b == 0) & (st == 0))
    def _():
        o_ref[...] = jnp.zeros_like(o_ref)
        rows = lax.broadcasted_iota(jnp.int32, (N_META, N_META), 0)
        cols = lax.broadcasted_iota(jnp.int32, (N_META, N_META), 1)
        qm = q_ref[0:N_META, :]
        km = k_ref[0:N_META, :]
        vm = v_ref[0:N_META, :]
        for h in range(N_HEADS):
            if mode == "diff":
                qs = _diff_queries(_rope(head(qm, h), cos_ref[0:N_META, :], sin_ref[0:N_META, :]))
                causal = cols <= rows
                carry = (jnp.full((2 * N_META, 1), -jnp.inf, F32), jnp.zeros((2 * N_META, 1), F32),
                         jnp.zeros((2 * N_META, HEAD_DIM), F32))
                carry = _diff_update(carry, qs, head(km, h), head(vm, h),
                                     jnp.concatenate([causal, causal], axis=0))
                o = _diff_finish(carry, N_META, _diff_lambda(lamp_ref, layer), g_ref[...], layer)
            else:
                carry = (jnp.zeros((N_META, 1), F32), jnp.zeros((N_META, HEAD_DIM), F32))
                carry = _sb_update(carry, head(qm, h).astype(BF16), head(km, h), head(vm, h),
                                   cols < rows, _upper_ones(N_META))
                o = carry[1]
            o_ref[0:N_META, HEAD_DIM * h:HEAD_DIM * (h + 1)] = o

    def page(ref):
        return _split2(jnp.concatenate([ref[0, :, h, :] for h in range(N_HEADS)], axis=1))

    def pv_product(p, v):
        return _dot3(_split2(jnp.transpose(p)), v)

    @pl.when(st == 0)
    def _():
        q8 = q_ref[pl.ds(row0, SAMPLE_ROWS), :]
        blocks = []
        for h in range(N_HEADS):
            qh = head(q8, h)
            if mode == "diff":
                qh = _diff_queries_f32(_rope(qh, cos_ref[pl.ds(row0, SAMPLE_ROWS), :],
                                             sin_ref[pl.ds(row0, SAMPLE_ROWS), :]))
            else:
                qh = jnp.concatenate([qh, jnp.zeros_like(qh)], axis=0)
            zero = jnp.zeros_like(qh)
            blocks.append(jnp.concatenate([qh if hh == h else zero for hh in range(N_HEADS)], axis=1))
        q_hi, q_lo = _split2(jnp.concatenate(blocks, axis=0))
        qbd_scr[0] = q_hi
        qbd_scr[1] = q_lo

        pad = jnp.zeros((LANES - SAMPLE_ROWS, W_MIX), F32)
        k_new = _split2(jnp.concatenate([k_ref[pl.ds(row0, SAMPLE_ROWS), :], pad], axis=0))
        v_new = _split2(jnp.concatenate([v_ref[pl.ds(row0, SAMPLE_ROWS), :], pad], axis=0))
        sc = _dot3_nt(k_new, (q_hi, q_lo))
        qrow = _column_rows(SAMPLE_ROWS)
        if mode == "diff":
            valid = (key_row <= qrow) & (key_row < ds)
            sc = jnp.where(valid, sc, -jnp.inf)
            m = jnp.max(sc, axis=0, keepdims=True)
            p = jnp.exp(sc - m)
            m_scr[...] = m
            l_scr[...] = jnp.sum(p, axis=0, keepdims=True)
        else:
            valid = (key_row < qrow) & (key_row < ds)
            z = sc * (HEAD_DIM ** -0.5)
            ls = _log_sigmoid(z)
            l1m = jnp.where(valid, ls - z, 0.0)
            hi, lo = _split2(l1m)
            lower = _later_ones(LANES)
            between = _dot(lower, hi) + _dot(lower, lo)
            p = jnp.where(valid, jnp.exp(ls + between), 0.0)
            m_scr[...] = jnp.sum(l1m, axis=0, keepdims=True)
            l_scr[...] = jnp.zeros_like(l_scr)
        acc_scr[...] = _block_diag_rows(pv_product(p, v_new))

    qbd = (qbd_scr[0], qbd_scr[1])
    if mode == "diff":
        scs = [_dot3_nt(page(kc_refs[j]), qbd) for j in range(pps)]
        m_old = m_scr[...]
        m_new = m_old
        for sc in scs:
            m_new = jnp.maximum(m_new, jnp.max(sc, axis=0, keepdims=True))
        alpha = jnp.exp(m_old - m_new)
        l = alpha * l_scr[...]
        pv = jnp.zeros((LANES, W_MIX), F32)
        for j in range(pps):
            p = jnp.exp(scs[j] - m_new)
            l = l + jnp.sum(p, axis=0, keepdims=True)
            pv = pv + pv_product(p, page(vc_refs[j]))
        m_scr[...] = m_new
        l_scr[...] = l
        acc_scr[...] = _lane_to_rows(alpha) * acc_scr[...] + _block_diag_rows(pv)
    else:
        lower = _later_ones(LANES)
        carry = m_scr[...]
        pv = jnp.zeros((LANES, W_MIX), F32)
        for j in reversed(range(pps)):
            z = _dot3_nt(page(kc_refs[j]), qbd) * (HEAD_DIM ** -0.5)
            ls = _log_sigmoid(z)
            l1m = ls - z
            hi, lo = _split2(l1m)
            between = _dot(lower, hi) + _dot(lower, lo)
            p = jnp.exp(ls + between + carry)
            pv = pv + pv_product(p, page(vc_refs[j]))
            carry = carry + jnp.sum(l1m, axis=0, keepdims=True)
        m_scr[...] = carry
        acc_scr[...] = acc_scr[...] + _block_diag_rows(pv)

    @pl.when(st == n_steps - 1)
    def _():
        acc = acc_scr[...]
        real = lax.broadcasted_iota(jnp.int32, (SAMPLE_ROWS, HEAD_DIM), 0) < ds
        outs = []
        if mode == "diff":
            acc = acc / _lane_to_rows(l_scr[...])
            lam = _diff_lambda(lamp_ref, layer)
            for h in range(N_HEADS):
                o = acc[16 * h:16 * h + 8] - lam * acc[16 * h + 8:16 * h + 16]
                o = _rms(o, g_ref[...]) * (1.0 - _lambda_init(layer))
                outs.append(jnp.where(real, o, 0.0))
        else:
            for h in range(N_HEADS):
                outs.append(jnp.where(real, acc[16 * h:16 * h + 8], 0.0))
        o_ref[pl.ds(row0, SAMPLE_ROWS), :] = jnp.concatenate(outs, axis=1)


def _attn_tail(mix, proj, k_tail, k_tail_col, cache_k, cache_v, page_table, mode, layer, b, s, ds, extra):
    r = proj.shape[0]
    tb = (b * s) // TAIL_ROWS
    db, n_pages = page_table.shape
    pps = PAGES_PER_STEP
    assert n_pages % pps == 0
    n_steps = n_pages // pps
    page = cache_k.shape[1]
    assert page == LANES and cache_k.shape[2:] == (N_HEADS, HEAD_DIM)
    page_block = (1, page, N_HEADS, HEAD_DIM)
    vcol = 2

    in_specs = [
        pl.BlockSpec((TAIL_ROWS, W_MIX), lambda bb, st, pt: (tb, 0)),
        pl.BlockSpec((TAIL_ROWS, W_MIX), lambda bb, st, pt: k_tail_col),
        pl.BlockSpec((TAIL_ROWS, W_MIX), lambda bb, st, pt: (tb, vcol)),
    ]
    args = [proj, k_tail, proj]
    if mode == "diff":
        cos_t, sin_t, lamp, g = extra
        in_specs += [
            pl.BlockSpec((TAIL_ROWS, LANES), lambda bb, st, pt: (s // TAIL_ROWS, 0)),
            pl.BlockSpec((TAIL_ROWS, LANES), lambda bb, st, pt: (s // TAIL_ROWS, 0)),
            pl.BlockSpec(lamp.shape, lambda bb, st, pt: (0, 0)),
            pl.BlockSpec((1, HEAD_DIM), lambda bb, st, pt: (0, 0)),
        ]
        args += [cos_t, sin_t, lamp, g]

    def page_spec(j):
        if mode == "diff":
            return pl.BlockSpec(page_block, lambda bb, st, pt: (pt[bb, st * pps + j], 0, 0, 0))
        return pl.BlockSpec(page_block, lambda bb, st, pt: (pt[bb, n_pages - (st + 1) * pps + j], 0, 0, 0))

    in_specs += [page_spec(j) for j in range(pps)] + [page_spec(j) for j in range(pps)]
    args += [cache_k] * pps + [cache_v] * pps
    in_specs.append(pl.BlockSpec(memory_space=pl.ANY))
    args.append(mix)
    n_in = len(args)

    grid_spec = pltpu.PrefetchScalarGridSpec(
        num_scalar_prefetch=1,
        grid=(db, n_steps),
        in_specs=in_specs,
        out_specs=pl.BlockSpec((TAIL_ROWS, W_MIX), lambda bb, st, pt: (tb, 0)),
        scratch_shapes=[
            pltpu.VMEM((2, LANES, W_MIX), BF16),
            pltpu.VMEM((1, LANES), F32),
            pltpu.VMEM((1, LANES), F32),
            pltpu.VMEM((LANES, HEAD_DIM), F32),
        ],
    )
    return pl.pallas_call(
        functools.partial(_attn_tail_kernel, mode=mode, layer=layer, ds=ds, n_steps=n_steps),
        out_shape=jax.ShapeDtypeStruct((r, 2 * W_MIX), F32),
        grid_spec=grid_spec,
        input_output_aliases={n_in: 0},
        compiler_params=_cparams(("arbitrary", "arbitrary")),
        name="attn_tail_" + mode,
    )(page_table, *args)


def _glu(val, gate):
    return val * _sigmoid(gate)


def _dwconv_rows(buf_ref, w_ref, n_rows, row_strip):
    off = HALO - (CONV_W - 1)
    strips = []
    for r0 in range(0, n_rows, row_strip):
        cols = []
        for c0 in range(0, W_MIX, LANES):
            acc = jnp.zeros((row_strip, LANES), F32)
            for k in range(CONV_W):
                acc = acc + w_ref[k:k + 1, c0:c0 + LANES] * buf_ref[r0 + off + k:r0 + off + k + row_strip, c0:c0 + LANES]
            cols.append(acc)
        strips.append(jnp.concatenate(cols, axis=1))
    return strips[0] if len(strips) == 1 else jnp.concatenate(strips, axis=0)


def _ln_silu(y, g, b):
    yc = y - jnp.mean(y, axis=-1, keepdims=True)
    yn = yc * lax.rsqrt(jnp.mean(yc * yc, axis=-1, keepdims=True) + EPS) * g + b
    return yn * _sigmoid(yn)


def _conv_prompt_kernel(val_ref, gate_ref, tval_ref, tgate_ref, w_ref, b_ref, g_ref, beta_ref, _mix_in,
                        o_ref, cp_ref, buf_scr, *, t):
    i = pl.program_id(1)

    @pl.when(i == 0)
    def _():
        buf_scr[0:HALO - N_META, :] = jnp.zeros((HALO - N_META, W_MIX), F32)
        buf_scr[HALO - N_META:HALO, :] = _glu(tval_ref[0:N_META, :], tgate_ref[0:N_META, :])

    @pl.when(i > 0)
    def _():
        buf_scr[0:HALO, :] = buf_scr[t:t + HALO, :]

    buf_scr[HALO:HALO + t, :] = _glu(val_ref[...], gate_ref[...])
    y = _dwconv_rows(buf_scr, w_ref, t, 64) + b_ref[...]
    o_ref[...] = _ln_silu(y, g_ref[...], beta_ref[...])

    @pl.when(i == pl.num_programs(1) - 1)
    def _():
        cp_ref[0] = buf_scr[HALO + t - (CONV_W - 1):HALO + t, :]


def _conv_prompt(mix, proj, w, bias, g, beta, b, s, t):
    r = proj.shape[0]
    tb = (b * s) // TAIL_ROWS
    nb = s // t
    vcol, gcol = 3, 4
    row = lambda x: x.reshape(1, W_MIX)
    full = lambda bb, i: (0, 0)
    return pl.pallas_call(
        functools.partial(_conv_prompt_kernel, t=t),
        out_shape=[jax.ShapeDtypeStruct((r, 2 * W_MIX), F32),
                   jax.ShapeDtypeStruct((b, CONV_W - 1, W_MIX), F32)],
        grid=(b, nb),
        in_specs=[
            pl.BlockSpec((t, W_MIX), lambda bb, i: (bb * nb + i, vcol)),
            pl.BlockSpec((t, W_MIX), lambda bb, i: (bb * nb + i, gcol)),
            pl.BlockSpec((TAIL_ROWS, W_MIX), lambda bb, i: (tb, vcol)),
            pl.BlockSpec((TAIL_ROWS, W_MIX), lambda bb, i: (tb, gcol)),
            pl.BlockSpec((CONV_W, W_MIX), full),
            pl.BlockSpec((1, W_MIX), full),
            pl.BlockSpec((1, W_MIX), full),
            pl.BlockSpec((1, W_MIX), full),
            pl.BlockSpec(memory_space=pl.ANY),
        ],
        out_specs=[pl.BlockSpec((t, W_MIX), lambda bb, i: (bb * nb + i, 1)),
                   pl.BlockSpec((1, CONV_W - 1, W_MIX), lambda bb, i: (bb, 0, 0))],
        scratch_shapes=[pltpu.VMEM((HALO + t, W_MIX), F32)],
        input_output_aliases={8: 0},
        compiler_params=_cparams(("parallel", "arbitrary")),
        name="conv_prompt",
    )(proj, proj, proj, proj, w, row(bias), row(g), row(beta), mix)


def _conv_tail_kernel(tval_ref, tgate_ref, st_ref, w_ref, b_ref, g_ref, beta_ref, _mix_in,
                      o_ref, cs_ref, buf_scr, *, db, ds):
    u = _glu(tval_ref[...], tgate_ref[...])
    o_ref[...] = jnp.zeros_like(o_ref)
    keep = CONV_W - 1

    def finish(n_rows):
        y = _dwconv_rows(buf_scr, w_ref, n_rows, n_rows) + b_ref[...]
        return _ln_silu(y, g_ref[...], beta_ref[...])

    buf_scr[0:HALO, :] = jnp.zeros((HALO, W_MIX), F32)
    buf_scr[HALO:HALO + N_META, :] = u[0:N_META]
    o_ref[0:N_META, :] = finish(N_META)
    real = lax.broadcasted_iota(jnp.int32, (SAMPLE_ROWS, W_MIX), 0) < ds
    for b in range(db):
        r0 = N_META + SAMPLE_ROWS * b
        buf_scr[0:HALO, :] = st_ref[b]
        buf_scr[HALO:HALO + SAMPLE_ROWS, :] = u[r0:r0 + SAMPLE_ROWS]
        o_ref[r0:r0 + SAMPLE_ROWS, :] = jnp.where(real, finish(SAMPLE_ROWS), 0.0)
        cs_ref[b] = buf_scr[HALO + ds - keep:HALO + ds, :]


def _conv_tail(mix, proj, state, w, bias, g, beta, b, s, ds):
    r = proj.shape[0]
    tb = (b * s) // TAIL_ROWS
    db = state.shape[0]
    keep = CONV_W - 1
    vcol, gcol = 3, 4
    state_pad = jnp.pad(state, ((0, 0), (HALO - keep, 0), (0, 0)))
    row = lambda x: x.reshape(1, W_MIX)
    full2 = lambda i: (0, 0)
    full3 = lambda i: (0, 0, 0)
    return pl.pallas_call(
        functools.partial(_conv_tail_kernel, db=db, ds=ds),
        out_shape=[jax.ShapeDtypeStruct((r, 2 * W_MIX), F32),
                   jax.ShapeDtypeStruct((db, keep, W_MIX), F32)],
        grid=(1,),
        in_specs=[
            pl.BlockSpec((TAIL_ROWS, W_MIX), lambda i: (tb, vcol)),
            pl.BlockSpec((TAIL_ROWS, W_MIX), lambda i: (tb, gcol)),
            pl.BlockSpec((db, HALO, W_MIX), full3),
            pl.BlockSpec((CONV_W, W_MIX), full2),
            pl.BlockSpec((1, W_MIX), full2),
            pl.BlockSpec((1, W_MIX), full2),
            pl.BlockSpec((1, W_MIX), full2),
            pl.BlockSpec(memory_space=pl.ANY),
        ],
        out_specs=[pl.BlockSpec((TAIL_ROWS, W_MIX), lambda i: (tb, 1)),
                   pl.BlockSpec((db, keep, W_MIX), full3)],
        scratch_shapes=[pltpu.VMEM((HALO + N_META, W_MIX), F32)],
        input_output_aliases={7: 0},
        compiler_params=_cparams(("arbitrary",)),
        name="conv_tail",
    )(proj, proj, state_pad, w, row(bias), row(g), row(beta), mix)


def _group_ref(cb, c, lvl):
    grp, half = 2 << lvl, 1 << lvl
    if grp >= 2 * SUBLANES:
        x3 = cb.reshape(c // grp, grp, LANES)
        ref = x3[:, half - 1:half, :]
        return jnp.broadcast_to(ref, (c // grp, grp, LANES)).reshape(c, LANES)
    x3 = cb.reshape(c // SUBLANES, SUBLANES, LANES)
    sub = lax.broadcasted_iota(jnp.int32, x3.shape, 1)
    out = None
    for g0 in range(0, SUBLANES, grp):
        ref = jnp.broadcast_to(x3[:, g0 + half - 1:g0 + half, :], x3.shape)
        out = ref if out is None else jnp.where(sub >= g0, ref, out)
    return out.reshape(c, LANES)


def _pad_rows(a):
    c = a.shape[0]
    if c % LANES == 0:
        return a
    return jnp.concatenate([a, jnp.zeros((LANES - c % LANES, a.shape[1]), a.dtype)], axis=0)


def _hgrn_chunk(hq, hf, hi, lb, st, c, valid, precise=False):
    if precise:
        mm_nt = lambda x, y: _dot3_nt(_split2(x), _split2(y))
        mm = lambda x, y: _dot3(_split2(x), _split2(y))
    else:
        mm_nt = lambda x, y: _dot_nt(x.astype(BF16), y.astype(BF16))
        mm = lambda x, y: _dot(x.astype(BF16), y.astype(BF16))
    f = lb + (1.0 - lb) * _sigmoid(hf)
    logf = jnp.log(f)
    k = 1.0 - f
    if valid is not None:
        logf = jnp.where(valid, logf, 0.0)
        k = jnp.where(valid, k, 0.0)
    row = lax.broadcasted_iota(jnp.int32, (c, LANES), 0)
    cb = logf
    sh = 1
    while sh < c:
        cb = cb + jnp.where(row >= sh, pltpu.roll(cb, sh, axis=0), 0.0)
        sh *= 2
    rows = lax.broadcasted_iota(jnp.int32, (c, c), 0)
    cols = lax.broadcasted_iota(jnp.int32, (c, c), 1)
    a = jnp.where(rows == cols, mm_nt(hq, k), 0.0)
    lvl = 0
    while (1 << lvl) < c:
        ref = _group_ref(cb, c, lvl)
        right = ((row >> lvl) & 1) == 1
        qe = jnp.where(right, hq * jnp.exp(jnp.minimum(cb - ref, 0.0)), 0.0)
        ke = jnp.where(right, 0.0, k * jnp.exp(jnp.minimum(ref - cb, 0.0)))
        a = a + jnp.where((rows >> (lvl + 1)) == (cols >> (lvl + 1)), mm_nt(qe, ke), 0.0)
        lvl += 1
    o = mm(a, hi) + mm_nt(hq * jnp.exp(cb), st)
    last = cb[c - 1:c, :]
    st_new = st * jnp.exp(last) + mm(jnp.transpose(_pad_rows(hi)), _pad_rows(k * jnp.exp(last - cb)))
    return o, st_new


def _hgrn_lb(lbl_ref):
    lg = lbl_ref[...]
    m = jnp.max(lg, axis=0, keepdims=True)
    e = jnp.exp(lg - m)
    sm = e / jnp.sum(e, axis=0, keepdims=True)
    return (sm[0:1] + sm[1:2]) - sm[0:1]


def _hgrn_out(o, g, hg):
    return _rms(o, g) * (hg * _sigmoid(hg))


def _hgrn_prompt_kernel(hq_ref, hf_ref, hi_ref, hg_ref, lbl_ref, g_ref, st0_ref, _mix_in, o_ref, s_ref, st_scr, *, c):
    ci = pl.program_id(2)

    @pl.when(ci == 0)
    def _():
        st_scr[...] = st0_ref[0]

    o, st = _hgrn_chunk(hq_ref[...], hf_ref[...], hi_ref[...], _hgrn_lb(lbl_ref), st_scr[...], c, None)
    st_scr[...] = st
    o_ref[...] = _hgrn_out(o, g_ref[...], hg_ref[...])

    @pl.when(ci == pl.num_programs(2) - 1)
    def _():
        s_ref[0, 0] = jnp.transpose(st)


def _hgrn_cols():
    base = 3 * W_MIX // HEAD_DIM
    return base, base + N_HEADS, base + 2 * N_HEADS, base + 3 * N_HEADS


def _hgrn_prompt(mix, proj, lb_logits, g, st_meta, b, s, c):
    r = proj.shape[0]
    nc = s // c
    qc, fc, ic, gc = _hgrn_cols()
    blk = lambda col: pl.BlockSpec((c, HEAD_DIM), lambda bb, h, ci: (bb * nc + ci, col + h))
    return pl.pallas_call(
        functools.partial(_hgrn_prompt_kernel, c=c),
        out_shape=[jax.ShapeDtypeStruct((r, 2 * W_MIX), F32),
                   jax.ShapeDtypeStruct((b, N_HEADS, HEAD_DIM, HEAD_DIM), F32)],
        grid=(b, N_HEADS, nc),
        in_specs=[
            blk(qc), blk(fc), blk(ic), blk(gc),
            pl.BlockSpec((2, HEAD_DIM), lambda bb, h, ci: (0, h)),
            pl.BlockSpec((1, HEAD_DIM), lambda bb, h, ci: (0, 0)),
            pl.BlockSpec((1, HEAD_DIM, HEAD_DIM), lambda bb, h, ci: (h, 0, 0)),
            pl.BlockSpec(memory_space=pl.ANY),
        ],
        out_specs=[pl.BlockSpec((c, HEAD_DIM), lambda bb, h, ci: (bb * nc + ci, N_HEADS + h)),
                   pl.BlockSpec((1, 1, HEAD_DIM, HEAD_DIM), lambda bb, h, ci: (bb, h, 0, 0))],
        scratch_shapes=[pltpu.VMEM((HEAD_DIM, HEAD_DIM), F32)],
        input_output_aliases={7: 0},
        compiler_params=_cparams(("parallel", "parallel", "arbitrary")),
        name="hgrn_prompt",
    )(proj, proj, proj, proj, lb_logits, g.reshape(1, HEAD_DIM), st_meta, mix)


def _hgrn_tail_kernel(hq_ref, hf_ref, hi_ref, hg_ref, lbl_ref, g_ref, s0_ref, _mix_in,
                      o_ref, stm_ref, sn_ref, st_scr, *, db, ds):
    j = pl.program_id(1)
    c = 2 * SAMPLE_ROWS
    row = lax.broadcasted_iota(jnp.int32, (c, HEAD_DIM), 0)
    lb = _hgrn_lb(lbl_ref)

    @pl.when(j == 0)
    def _():
        o, st = _hgrn_chunk(hq_ref[...], hf_ref[...], hi_ref[...], lb, jnp.zeros((HEAD_DIM, HEAD_DIM), F32), c, None)
        stm_ref[0] = st
        o_ref[...] = _hgrn_out(o, g_ref[...], hg_ref[...])

    @pl.when((j > 0) & (j <= db // 2))
    def _():
        outs = []
        for half in range(2):
            valid = (row >= half * SAMPLE_ROWS) & (row < half * SAMPLE_ROWS + ds)
            st0 = jnp.transpose(s0_ref[half, 0])
            o, st = _hgrn_chunk(hq_ref[...], hf_ref[...], hi_ref[...], lb, st0, c, valid, precise=True)
            sn_ref[half, 0] = jnp.transpose(st)
            outs.append(jnp.where(valid, o, 0.0))
        o_ref[...] = _hgrn_out(outs[0] + outs[1], g_ref[...], hg_ref[...])

    @pl.when(j > db // 2)
    def _():
        o_ref[...] = jnp.zeros_like(o_ref)


def _hgrn_tail(mix, proj, lb_logits, g, state, b, s, ds):
    r = proj.shape[0]
    db = state.shape[0]
    assert db % 2 == 0
    c = 2 * SAMPLE_ROWS
    row_blk0 = (b * s) // c
    n_chunks = TAIL_ROWS // c
    qc, fc, ic, gc = _hgrn_cols()
    blk = lambda col: pl.BlockSpec((c, HEAD_DIM), lambda h, j: (row_blk0 + j, col + h))
    pair = lambda h, j: (jnp.clip(j - 1, 0, db // 2 - 1), h, 0, 0)
    return pl.pallas_call(
        functools.partial(_hgrn_tail_kernel, db=db, ds=ds),
        out_shape=[jax.ShapeDtypeStruct((r, 2 * W_MIX), F32),
                   jax.ShapeDtypeStruct((N_HEADS, HEAD_DIM, HEAD_DIM), F32),
                   jax.ShapeDtypeStruct((db, N_HEADS, HEAD_DIM, HEAD_DIM), F32)],
        grid=(N_HEADS, n_chunks),
        in_specs=[
            blk(qc), blk(fc), blk(ic), blk(gc),
            pl.BlockSpec((2, HEAD_DIM), lambda h, j: (0, h)),
            pl.BlockSpec((1, HEAD_DIM), lambda h, j: (0, 0)),
            pl.BlockSpec((2, 1, HEAD_DIM, HEAD_DIM), pair),
            pl.BlockSpec(memory_space=pl.ANY),
        ],
        out_specs=[pl.BlockSpec((c, HEAD_DIM), lambda h, j: (row_blk0 + j, N_HEADS + h)),
                   pl.BlockSpec((1, HEAD_DIM, HEAD_DIM), lambda h, j: (h, 0, 0)),
                   pl.BlockSpec((2, 1, HEAD_DIM, HEAD_DIM), pair)],
        scratch_shapes=[pltpu.VMEM((HEAD_DIM, HEAD_DIM), F32)],
        input_output_aliases={7: 0},
        compiler_params=_cparams(("arbitrary", "arbitrary")),
        name="hgrn_tail",
    )(proj, proj, proj, proj, lb_logits, g.reshape(1, HEAD_DIM), state, mix)


def _router_kernel(h_ref, g_ref, w_ref, eid_ref, gate_ref, a_ref):
    a = _rms(h_ref[...], g_ref[...])
    a_ref[...] = a.astype(BF16)
    logits = jnp.dot(a, w_ref[...], preferred_element_type=F32, precision=lax.Precision.HIGHEST)
    lane = lax.broadcasted_iota(jnp.int32, logits.shape, 1)
    logits = jnp.where(lane < N_EXPERTS, logits, -jnp.inf)
    m1 = jnp.max(logits, axis=-1, keepdims=True)
    i1 = jnp.min(jnp.where(logits == m1, lane, LANES), axis=-1, keepdims=True)
    rest = jnp.where(lane == i1, -jnp.inf, logits)
    m2 = jnp.max(rest, axis=-1, keepdims=True)
    i2 = jnp.min(jnp.where(rest == m2, lane, LANES), axis=-1, keepdims=True)
    e2 = jnp.exp(m2 - m1)
    g1 = 1.0 / (1.0 + e2)
    g2 = e2 / (1.0 + e2)
    eid_ref[...] = jnp.where(lane == 0, i1, jnp.where(lane == 1, i2, 0))
    gate_ref[...] = jnp.where(lane == 0, g1, jnp.where(lane == 1, g2, 0.0))


def _router(h, g, w_router, tm):
    r, d = h.shape
    w_pad = jnp.pad(w_router, ((0, 0), (0, LANES - N_EXPERTS)))
    eid, gate, a = pl.pallas_call(
        _router_kernel,
        out_shape=[jax.ShapeDtypeStruct((r, LANES), jnp.int32), jax.ShapeDtypeStruct((r, LANES), F32),
                   jax.ShapeDtypeStruct((r, d), BF16)],
        grid=(r // tm,),
        in_specs=[
            pl.BlockSpec((tm, d), lambda i: (i, 0)),
            pl.BlockSpec((1, d), lambda i: (0, 0)),
            pl.BlockSpec((d, LANES), lambda i: (0, 0)),
        ],
        out_specs=[pl.BlockSpec((tm, LANES), lambda i: (i, 0)), pl.BlockSpec((tm, LANES), lambda i: (i, 0)),
                   pl.BlockSpec((tm, d), lambda i: (i, 0))],
        compiler_params=_cparams(("parallel",)),
        name="moe_router",
    )(h, g.reshape(1, d), w_pad)
    return eid, gate, a


def _moe_permute_kernel(first_ref, last_ref, tok_ref, a_ref, o_ref, *, sblk):
    i = pl.program_id(0)
    k = pl.program_id(1)
    blk = first_ref[i] + k

    @pl.when(k == 0)
    def _():
        o_ref[...] = jnp.zeros_like(o_ref)

    @pl.when(blk <= last_ref[i])
    def _():
        col = lax.broadcasted_iota(jnp.int32, (o_ref.shape[0], sblk), 1)
        onehot = jnp.where(tok_ref[...] - blk * sblk == col, 1.0, 0.0).astype(BF16)
        o_ref[...] += _dot(onehot, a_ref[...]).astype(BF16)


def _moe_permute(a, slot_tok, tile_first, tile_last, tm, sblk):
    r, d = a.shape
    cap = slot_tok.shape[0]
    grid_spec = pltpu.PrefetchScalarGridSpec(
        num_scalar_prefetch=2,
        grid=(cap // tm, r // sblk),
        in_specs=[
            pl.BlockSpec((tm, 1), lambda i, k, tf, tl: (i, 0)),
            pl.BlockSpec((sblk, d), lambda i, k, tf, tl: (jnp.minimum(tf[i] + k, jnp.maximum(tl[i], 0)), 0)),
        ],
        out_specs=pl.BlockSpec((tm, d), lambda i, k, tf, tl: (i, 0)),
    )
    return pl.pallas_call(
        functools.partial(_moe_permute_kernel, sblk=sblk),
        out_shape=jax.ShapeDtypeStruct((cap, d), BF16),
        grid_spec=grid_spec,
        compiler_params=_cparams(("arbitrary", "arbitrary")),
        name="moe_permute",
    )(tile_first, tile_last, slot_tok.reshape(cap, 1), a)


def _moe_ffn_kernel(te_ref, tv_ref, tr_ref, x_ref, sg_ref, wg_ref, wu_ref, wd_ref, o_ref, acc_scr):
    i = pl.program_id(0)
    f = pl.program_id(1)
    del te_ref, tr_ref

    @pl.when(tv_ref[i] == 1)
    def _():
        @pl.when(f == 0)
        def _():
            acc_scr[...] = jnp.zeros_like(acc_scr)

        a = x_ref[...]
        gate = _dot(a, wg_ref[0].astype(BF16))
        up = _dot(a, wu_ref[0].astype(BF16))
        t = (gate * _sigmoid(gate)) * up
        acc_scr[...] += _dot(t.astype(BF16), wd_ref[0].astype(BF16))

        @pl.when(f == pl.num_programs(1) - 1)
        def _():
            o_ref[...] = acc_scr[...] * sg_ref[...]

    @pl.when((tv_ref[i] == 0) & (f == 0))
    def _():
        o_ref[...] = jnp.zeros_like(o_ref)


def _moe_ffn(x_sorted, slot_gate, wg, wu, wd, tile_expert, tile_valid, tile_row, tm, tf):
    cap, d = x_sorted.shape
    ff = wg.shape[2]
    nf = ff // tf
    n_tiles = cap // tm
    grid_spec = pltpu.PrefetchScalarGridSpec(
        num_scalar_prefetch=3,
        grid=(n_tiles, nf),
        in_specs=[
            pl.BlockSpec((tm, d), lambda i, f, te, tv, tr: (tr[i], 0)),
            pl.BlockSpec((tm, 1), lambda i, f, te, tv, tr: (tr[i], 0)),
            pl.BlockSpec((1, d, tf), lambda i, f, te, tv, tr: (te[i], 0, jnp.where(tv[i] == 1, f, nf - 1))),
            pl.BlockSpec((1, d, tf), lambda i, f, te, tv, tr: (te[i], 0, jnp.where(tv[i] == 1, f, nf - 1))),
            pl.BlockSpec((1, tf, d), lambda i, f, te, tv, tr: (te[i], jnp.where(tv[i] == 1, f, nf - 1), 0)),
        ],
        out_specs=pl.BlockSpec((tm, d), lambda i, f, te, tv, tr: (i, 0)),
        scratch_shapes=[pltpu.VMEM((tm, d), F32)],
    )
    return pl.pallas_call(
        _moe_ffn_kernel,
        out_shape=jax.ShapeDtypeStruct((cap, d), F32),
        grid_spec=grid_spec,
        compiler_params=_cparams(("arbitrary", "arbitrary")),
        name="moe_ffn",
    )(tile_expert, tile_valid, tile_row, x_sorted, slot_gate.reshape(cap, 1), wg, wu, wd)


def _moe_combine_kernel(ws_ref, h_ref, eid_ref, dest_ref, g_ref, *refs, n_main, win):
    y_refs = refs[:N_EXPERTS]
    yp_ref, yt_ref = refs[N_EXPERTS:]
    j = pl.program_id(0)
    eid = eid_ref[...]
    dest = dest_ref[...]
    col = lax.broadcasted_iota(jnp.int32, (TAIL_ROWS, win), 1)
    acc = h_ref[...]
    for e in range(N_EXPERTS):
        start = ws_ref[j * N_EXPERTS + e] * SUBLANES
        onehot = jnp.zeros((TAIL_ROWS, win), BF16)
        for k in range(TOP_K):
            rel = jnp.where(eid[:, k:k + 1] == e, dest[:, k:k + 1] - start, -1)
            onehot = onehot + jnp.where(rel == col, 1.0, 0.0).astype(BF16)
        hi, lo = _split2(y_refs[e][...])
        acc = acc + _dot(onehot, hi) + _dot(onehot, lo)
    y = _rms(acc, g_ref[...])

    @pl.when(j < n_main)
    def _():
        yp_ref[...] = y

    @pl.when(j == n_main)
    def _():
        yt_ref[...] = y


def _moe_combine(h, y_sorted, eid, dest, win_start, g, n_main_rows, win):
    r, d = h.shape
    t = TAIL_ROWS
    n_main = n_main_rows // t
    window = lambda e: pl.BlockSpec((pl.Element(win), pl.Element(d)),
                                    lambda j, ws: (ws[j * N_EXPERTS + e] * SUBLANES, 0))
    grid_spec = pltpu.PrefetchScalarGridSpec(
        num_scalar_prefetch=1,
        grid=(r // t,),
        in_specs=[
            pl.BlockSpec((t, d), lambda j, ws: (j, 0)),
            pl.BlockSpec((t, LANES), lambda j, ws: (j, 0)),
            pl.BlockSpec((t, LANES), lambda j, ws: (j, 0)),
            pl.BlockSpec((1, d), lambda j, ws: (0, 0)),
        ] + [window(e) for e in range(N_EXPERTS)],
        out_specs=[pl.BlockSpec((t, d), lambda j, ws: (jnp.minimum(j, n_main - 1), 0)),
                   pl.BlockSpec((t, d), lambda j, ws: (0, 0))],
    )
    return pl.pallas_call(
        functools.partial(_moe_combine_kernel, n_main=n_main, win=win),
        out_shape=[jax.ShapeDtypeStruct((n_main_rows, d), F32), jax.ShapeDtypeStruct((t, d), F32)],
        grid_spec=grid_spec,
        compiler_params=_cparams(("arbitrary",)),
        name="moe_combine_norm",
    )(win_start, h, eid, dest, g.reshape(1, d), *([y_sorted] * N_EXPERTS))


def _moe(h, g_ffn, w_router, wg, wu, wd, g_final, n_main_rows, tm):
    r, d = h.shape
    eid_pad, gate_pad, a = _router(h, g_ffn, w_router, _pick_tile(r, 1024, LANES))
    eid, gate = eid_pad[:, :TOP_K], gate_pad[:, :TOP_K]
    n_slots = r * TOP_K
    big = jnp.int32(2 ** 30)
    e_flat = eid.reshape(n_slots)
    onehot = (e_flat[:, None] == jnp.arange(N_EXPERTS, dtype=jnp.int32)[None, :]).astype(jnp.int32)
    rank = jnp.take_along_axis(jnp.cumsum(onehot, axis=0), e_flat[:, None], axis=1)[:, 0] - 1
    counts = jnp.sum(onehot, axis=0)
    padded = (counts + tm - 1) // tm * tm
    pends = jnp.cumsum(padded)
    pstarts = pends - padded
    dest = (pstarts[e_flat] + rank).astype(jnp.int32)
    n_tiles = (n_slots + N_EXPERTS * (tm - 1)) // tm
    cap = n_tiles * tm
    slot_tok = jnp.full((cap,), -1, jnp.int32).at[dest].set(jnp.arange(n_slots, dtype=jnp.int32) // TOP_K)
    slot_gate = jnp.zeros((cap,), F32).at[dest].set(gate.reshape(n_slots))
    n_used = pends[-1] // tm
    tile_idx = jnp.arange(n_tiles, dtype=jnp.int32)
    tile_valid = (tile_idx < n_used).astype(jnp.int32)
    tile_row = jnp.minimum(tile_idx, n_used - 1).astype(jnp.int32)
    tile_expert = jnp.minimum(
        jnp.sum((pends[None, :] <= (tile_row * tm)[:, None]).astype(jnp.int32), axis=1), N_EXPERTS - 1)
    sblk = tm
    tok_tiles = slot_tok.reshape(n_tiles, tm)
    tok_max = jnp.max(tok_tiles, axis=1)
    tok_min = jnp.min(jnp.where(tok_tiles >= 0, tok_tiles, big), axis=1)
    tile_first = jnp.where(tok_max >= 0, tok_min // sblk, 1).astype(jnp.int32)
    tile_last = jnp.where(tok_max >= 0, tok_max // sblk, 0).astype(jnp.int32)
    win = TAIL_ROWS + SUBLANES
    n_tok_tiles = r // TAIL_ROWS
    dest_e = jnp.where(onehot == 1, dest[:, None], big).reshape(n_tok_tiles, TAIL_ROWS * TOP_K, N_EXPERTS)
    first_row = jnp.min(dest_e, axis=1)
    win_start = jnp.clip(first_row // SUBLANES, 0, (cap - win) // SUBLANES).reshape(-1).astype(jnp.int32)

    x_sorted = _moe_permute(a, slot_tok, tile_first, tile_last, tm, sblk)
    y_sorted = _moe_ffn(x_sorted, slot_gate, wg, wu, wd, tile_expert.astype(jnp.int32), tile_valid,
                        tile_row, tm, _pick_tile(wg.shape[2], 256, LANES))
    dest_pad = jnp.pad(dest.reshape(r, TOP_K), ((0, 0), (0, LANES - TOP_K)))
    return _moe_combine(h, y_sorted, eid_pad, dest_pad, win_start, g_final, n_main_rows, win)


def kernel(x_prompt, x_sample, cache_k0, cache_v0, state_conv0, cache_k1, cache_v1, state_hgrn1, page_table, meta_tokens, norm_mix, norm_ffn, norm_final, w_in0, lambda_q1, lambda_k1, lambda_q2, lambda_k2, subln_g0, conv_w0, conv_b0, conv_ln_g0, conv_ln_b0, w_out0, ffn_gate0, ffn_up0, ffn_down0, w_in1, hgrn_lb_logits, hgrn_norm_g1, w_out1, router1, moe_gate1, moe_up1, moe_down1):
    b, s, d = x_prompt.shape
    db, ds, _ = x_sample.shape
    n_pages = page_table.shape[1]
    page = cache_k0.shape[1]
    assert d == 2 * W_MIX and ds <= SAMPLE_ROWS and N_META + SAMPLE_ROWS * db <= TAIL_ROWS
    assert s % 256 == 0
    bs = b * s
    r = bs + TAIL_ROWS
    tm = _pick_tile(r, 768, LANES)
    tq = 256

    xs = jnp.pad(x_sample, ((0, 0), (0, SAMPLE_ROWS - ds), (0, 0))).reshape(db * SAMPLE_ROWS, d)
    tail = jnp.concatenate(
        [meta_tokens.astype(F32), xs, jnp.zeros((TAIL_ROWS - N_META - db * SAMPLE_ROWS, d), F32)], axis=0)
    x_flat = jnp.concatenate([x_prompt.reshape(bs, d), tail], axis=0)

    pos_tail = jnp.concatenate([
        jnp.arange(N_META, dtype=jnp.int32),
        jnp.tile(n_pages * page + jnp.arange(SAMPLE_ROWS, dtype=jnp.int32), db),
        jnp.zeros((TAIL_ROWS - N_META - db * SAMPLE_ROWS,), jnp.int32)])
    tables = _rope_tables(jnp.concatenate([N_META + jnp.arange(s, dtype=jnp.int32), pos_tail]))
    lamp = jnp.stack([lambda_q1, lambda_k1, lambda_q2, lambda_k2]).astype(F32)
    g_sub = subln_g0.reshape(1, HEAD_DIM)

    def sample_rows(x):
        return x[N_META:N_META + db * SAMPLE_ROWS].reshape(db, SAMPLE_ROWS, N_HEADS, HEAD_DIM)[:, :ds]

    proj0 = _norm_matmul(x_flat, norm_mix[0], w_in0, tm, 512)
    k0p, v0p, k0_tail = _kv_layout(proj0, b, s, tables)
    extra0 = (tables[0], tables[1], lamp, g_sub)
    mix0 = _attn_prompt(proj0, k0p, v0p, "diff", 0, b, s, tq, extra0)
    mix0 = _attn_tail(mix0, proj0, k0_tail, (0, 0), cache_k0, cache_v0, page_table, "diff", 0, b, s, ds, extra0)
    mix0, conv0_p = _conv_prompt(mix0, proj0, conv_w0, conv_b0, conv_ln_g0, conv_ln_b0, b, s, 128)
    mix0, conv0_s = _conv_tail(mix0, proj0, state_conv0, conv_w0, conv_b0, conv_ln_g0, conv_ln_b0, b, s, ds)
    h = _matmul_residual(mix0, w_out0, x_flat, tm, 512)
    h = _ffn(h, norm_ffn[0], ffn_gate0, ffn_up0, ffn_down0, tm, _pick_tile(ffn_gate0.shape[1], 256, LANES))

    proj1 = _norm_matmul(h, norm_mix[1], w_in1, tm, 512)
    k1p, v1p = _kv_layout(proj1, b, s, None)
    tb = bs // TAIL_ROWS
    mix1 = _attn_prompt(proj1, k1p, v1p, "sb", 1, b, s, tq, None)
    mix1 = _attn_tail(mix1, proj1, proj1, (tb, 1), cache_k1, cache_v1, page_table, "sb", 1, b, s, ds, None)
    mix1, st_meta, hgrn_s = _hgrn_tail(mix1, proj1, hgrn_lb_logits, hgrn_norm_g1, state_hgrn1, b, s, ds)
    mix1, hgrn_p = _hgrn_prompt(mix1, proj1, hgrn_lb_logits, hgrn_norm_g1, st_meta, b, s, 256)
    h = _matmul_residual(mix1, w_out1, h, tm, 512)
    y_main, y_tail = _moe(h, norm_ffn[1], router1, moe_gate1, moe_up1, moe_down1, norm_final, bs, tm)

    y_prompt = y_main.reshape(b, s, d)
    y_sample = y_tail[N_META:N_META + db * SAMPLE_ROWS].reshape(db, SAMPLE_ROWS, d)[:, :ds]
    heads = lambda x: x.reshape(b, N_META + s, N_HEADS, HEAD_DIM)
    tail0 = proj0[bs:]
    tail1 = proj1[bs:]
    return (y_prompt, y_sample,
            heads(k0p), heads(v0p), conv0_p, heads(k1p), heads(v1p), hgrn_p,
            sample_rows(k0_tail), sample_rows(tail0[:, 2 * W_MIX:3 * W_MIX]), conv0_s,
            sample_rows(tail1[:, W_MIX:2 * W_MIX]), sample_rows(tail1[:, 2 * W_MIX:3 * W_MIX]), hgrn_s)
```

```python
import functools
import math

import jax
import jax.numpy as jnp
from jax import lax
from jax.experimental import pallas as pl
from jax.experimental.pallas import tpu as pltpu

F32 = jnp.float32
BF16 = jnp.bfloat16

N_META = 16
EPS = 1e-6
ROPE_THETA = 10000.0
N_HEADS = 8
HEAD_DIM = 128
W_MIX = N_HEADS * HEAD_DIM
DK_A = 64
CONV_W = 31
N_EXPERTS = 8
TOP_K = 2
HALO = 32

LANES = 128
SUBLANES = 8
TAIL_ROWS = 128
SAMPLE_ROWS = 8
V7X_VMEM_BYTES = 64 * 1024 * 1024
VMEM_LIMIT_BYTES = 56 * 1024 * 1024
PAGES_PER_STEP = 8


def _lambda_init(layer):
    return 0.8 - 0.6 * math.exp(-0.3 * layer)


def _cparams(semantics, vmem=VMEM_LIMIT_BYTES):
    return pltpu.CompilerParams(dimension_semantics=semantics, vmem_limit_bytes=vmem)


def _pick_tile(n, cap, mult):
    best = None
    for t in range(mult, min(n, cap) + 1, mult):
        if n % t == 0:
            best = t
    assert best is not None, (n, cap, mult)
    return best


def _dot(a, b):
    return jnp.dot(a, b, preferred_element_type=F32)


def _dot_nt(a, b):
    return lax.dot_general(a, b, (((1,), (1,)), ((), ())), preferred_element_type=F32)


def _sigmoid(x):
    return 1.0 / (1.0 + jnp.exp(-x))


def _log_sigmoid(x):
    return jnp.minimum(x, 0.0) - jnp.log(1.0 + jnp.exp(-jnp.abs(x)))


def _rms(x, g):
    return x * lax.rsqrt(jnp.mean(x * x, axis=-1, keepdims=True) + EPS) * g


def _split2(x):
    hi = x.astype(BF16)
    lo = (x - hi.astype(F32)).astype(BF16)
    return hi, lo


def _dot3(a, b):
    return _dot(a[0], b[0]) + _dot(a[1], b[0]) + _dot(a[0], b[1])


def _dot3_nt(a, b):
    return _dot_nt(a[0], b[0]) + _dot_nt(a[1], b[0]) + _dot_nt(a[0], b[1])


def _lhs_store(a_scr, a, precise):
    if precise:
        hi, lo = _split2(a)
        a_scr[0] = hi
        a_scr[1] = lo
    else:
        a_scr[0] = a.astype(BF16)


def _lhs_load(a_scr, precise):
    return (a_scr[0], a_scr[1]) if precise else a_scr[0]


def _mm(a, w, precise):
    if not precise:
        return _dot(a, w.astype(BF16))
    a_hi, a_lo = a
    w_hi, w_lo = _split2(w)
    return _dot(a_hi, w_hi) + _dot(a_lo, w_hi) + _dot(a_hi, w_lo)


def _proj_kernel(x_ref, g_ref, w_ref, *rest, precise):
    o_ref, a_scr = rest[-2:]

    @pl.when(pl.program_id(1) == 0)
    def _():
        _lhs_store(a_scr, _rms(x_ref[...], g_ref[...]), precise)

    o_ref[...] = _mm(_lhs_load(a_scr, precise), w_ref[...], precise)


def _outproj_kernel(x_ref, w_ref, r_ref, *rest, precise):
    o_ref, a_scr = rest[-2:]

    @pl.when(pl.program_id(1) == 0)
    def _():
        _lhs_store(a_scr, x_ref[...], precise)

    o_ref[...] = r_ref[...] + _mm(_lhs_load(a_scr, precise), w_ref[...], precise)


def _ffn_kernel(h_ref, g_ref, wg_ref, wu_ref, wd_ref, *rest, precise):
    o_ref, a_scr, acc_scr = rest[-3:]
    f = pl.program_id(1)

    @pl.when(f == 0)
    def _():
        _lhs_store(a_scr, _rms(h_ref[...], g_ref[...]), precise)
        acc_scr[...] = jnp.zeros_like(acc_scr)

    a = _lhs_load(a_scr, precise)
    gate = _mm(a, wg_ref[...], precise)
    up = _mm(a, wu_ref[...], precise)
    t = (gate * _sigmoid(gate)) * up
    acc_scr[...] += _mm(_split2(t) if precise else t.astype(BF16), wd_ref[...], precise)

    @pl.when(f == pl.num_programs(1) - 1)
    def _():
        o_ref[...] = h_ref[...] + acc_scr[...]


def _dense_call(kernel_fn, name, x, consts, weights, res, out_width, out_tile, tm, n_steps, prev, acc):
    r, d = x.shape
    precise = prev is not None
    if precise:
        tm, row0, n_row_tiles = TAIL_ROWS, (r - TAIL_ROWS) // TAIL_ROWS, 1
    else:
        row0, n_row_tiles = 0, r // tm
    if out_tile == out_width:
        out_map = lambda i, j: (row0 + i, 0)
    else:
        out_map = lambda i, j: (row0 + i, j)
    in_specs = [pl.BlockSpec((tm, d), lambda i, j: (row0 + i, 0))]
    args = [x]
    for arr in consts:
        in_specs.append(pl.BlockSpec(arr.shape, lambda i, j: (0, 0)))
        args.append(arr)
    for arr, shape, imap in weights:
        in_specs.append(pl.BlockSpec(shape, lambda i, j, imap=imap: imap(j)))
        args.append(arr)
    if res is not None:
        in_specs.append(pl.BlockSpec((tm, out_tile), out_map))
        args.append(res)
    aliases = {}
    if precise:
        in_specs.append(pl.BlockSpec(memory_space=pl.ANY))
        args.append(prev)
        aliases = {len(args) - 1: 0}
    return pl.pallas_call(
        functools.partial(kernel_fn, precise=precise),
        out_shape=jax.ShapeDtypeStruct((r, out_width), F32),
        grid=(n_row_tiles, n_steps),
        in_specs=in_specs,
        out_specs=pl.BlockSpec((tm, out_tile), out_map),
        scratch_shapes=[pltpu.VMEM((2 if precise else 1, tm, d), BF16)] + (
            [pltpu.VMEM((tm, d), F32)] if acc else []),
        input_output_aliases=aliases,
        compiler_params=_cparams(("parallel", "arbitrary")),
        name=name + ("_tail" if precise else ""),
    )(*args)


def _norm_matmul(x, g, w, tm, tn):
    d, n = w.shape
    run = lambda prev: _dense_call(_proj_kernel, "norm_matmul", x, [g.reshape(1, d)],
                                   [(w, (d, tn), lambda j: (0, j))], None, n, tn, tm, n // tn, prev, False)
    return run(run(None))


def _matmul_residual(x, w, res, tm, tn):
    d, n = w.shape
    run = lambda prev: _dense_call(_outproj_kernel, "matmul_residual", x, [],
                                   [(w, (d, tn), lambda j: (0, j))], res, n, tn, tm, n // tn, prev, False)
    return run(run(None))


def _ffn(h, g, wg, wu, wd, tm, tf):
    d, ff = wg.shape
    weights = [(wg, (d, tf), lambda j: (0, j)), (wu, (d, tf), lambda j: (0, j)), (wd, (tf, d), lambda j: (j, 0))]
    run = lambda prev: _dense_call(_ffn_kernel, "ffn_swiglu", h, [g.reshape(1, d)], weights, None,
                                   d, d, tm, ff // tf, prev, True)
    return run(run(None))


def _rope_tables(pos):
    half = DK_A // 2
    inv = ROPE_THETA ** (-jnp.arange(half, dtype=F32) * 2.0 / DK_A)
    ang = pos.astype(F32)[:, None] * inv[None, :]
    cos = jnp.cos(ang)
    sin = jnp.sin(ang)
    cos_t = jnp.concatenate([cos, cos, cos, cos], axis=1)
    sin_t = jnp.concatenate([-sin, sin, -sin, sin], axis=1)
    return cos_t, sin_t


def _rope(x, cos_t, sin_t):
    half = DK_A // 2
    lane = lax.broadcasted_iota(jnp.int32, x.shape, 1)
    first = (lane % DK_A) < half
    partner = jnp.where(first, pltpu.roll(x, LANES - half, axis=1), pltpu.roll(x, half, axis=1))
    return x * cos_t + partner * sin_t


def _kv_layout_kernel(*refs, rope, s):
    if rope:
        (k_ref, kt_ref, v_ref, vt_ref, cos_ref, sin_ref, cost_ref, sint_ref, kp_ref, vp_ref, ktr_ref) = refs
        k = _rope(k_ref[...], cos_ref[...], sin_ref[...])
        kt = _rope(kt_ref[...], cost_ref[...], sint_ref[...])
        ktr_ref[...] = kt
    else:
        (k_ref, kt_ref, v_ref, vt_ref, kp_ref, vp_ref) = refs
        k = k_ref[...]
        kt = kt_ref[...]
    kp_ref[0, 0:N_META, :] = kt[0:N_META]
    kp_ref[0, N_META:N_META + s, :] = k
    vp_ref[0, 0:N_META, :] = vt_ref[0:N_META, :]
    vp_ref[0, N_META:N_META + s, :] = v_ref[...]


def _kv_layout(proj, b, s, tables):
    tb = (b * s) // TAIL_ROWS
    kcol, vcol = W_MIX // HEAD_DIM, 2 * W_MIX // HEAD_DIM
    rope = tables is not None
    in_specs = [
        pl.BlockSpec((s, HEAD_DIM), lambda h, bb: (bb, kcol + h)),
        pl.BlockSpec((TAIL_ROWS, HEAD_DIM), lambda h, bb: (tb, kcol + h)),
        pl.BlockSpec((s, HEAD_DIM), lambda h, bb: (bb, vcol + h)),
        pl.BlockSpec((TAIL_ROWS, HEAD_DIM), lambda h, bb: (tb, vcol + h)),
    ]
    args = [proj, proj, proj, proj]
    out_shape = [jax.ShapeDtypeStruct((b, N_META + s, W_MIX), F32)] * 2
    out_specs = [pl.BlockSpec((1, N_META + s, HEAD_DIM), lambda h, bb: (bb, 0, h))] * 2
    if rope:
        cos_t, sin_t = tables
        in_specs += [
            pl.BlockSpec((s, LANES), lambda h, bb: (0, 0)),
            pl.BlockSpec((s, LANES), lambda h, bb: (0, 0)),
            pl.BlockSpec((TAIL_ROWS, LANES), lambda h, bb: (s // TAIL_ROWS, 0)),
            pl.BlockSpec((TAIL_ROWS, LANES), lambda h, bb: (s // TAIL_ROWS, 0)),
        ]
        args += [cos_t, sin_t, cos_t, sin_t]
        out_shape = out_shape + [jax.ShapeDtypeStruct((TAIL_ROWS, W_MIX), F32)]
        out_specs = out_specs + [pl.BlockSpec((TAIL_ROWS, HEAD_DIM), lambda h, bb: (0, h))]
    return pl.pallas_call(
        functools.partial(_kv_layout_kernel, rope=rope, s=s),
        out_shape=out_shape,
        grid=(N_HEADS, b),
        in_specs=in_specs,
        out_specs=out_specs,
        compiler_params=_cparams(("arbitrary", "arbitrary")),
        name="kv_layout_rope" if rope else "kv_layout",
    )(*args)


def _diff_lambda(lamp_ref, layer):
    lp = lamp_ref[...]
    d1 = jnp.sum(lp[0:1] * lp[1:2], axis=-1, keepdims=True)
    d2 = jnp.sum(lp[2:3] * lp[3:4], axis=-1, keepdims=True)
    return jnp.exp(d1) - jnp.exp(d2) + _lambda_init(layer)


def _diff_queries_f32(q):
    lane = lax.broadcasted_iota(jnp.int32, q.shape, 1)
    qs = q * (DK_A ** -0.5)
    q1 = jnp.where(lane < DK_A, qs, 0.0)
    q2 = jnp.where(lane >= DK_A, qs, 0.0)
    return jnp.concatenate([q1, q2], axis=0)


def _diff_queries(q):
    return _diff_queries_f32(q).astype(BF16)


def _diff_update(carry, qs, kc, vc, mask2):
    m, l, acc = carry
    sc = _dot_nt(qs, kc.astype(BF16))
    if mask2 is not None:
        sc = jnp.where(mask2, sc, -jnp.inf)
    m_new = jnp.maximum(m, jnp.max(sc, axis=-1, keepdims=True))
    alpha = jnp.exp(m - m_new)
    p = jnp.exp(sc - m_new)
    l = alpha * l + jnp.sum(p, axis=-1, keepdims=True)
    acc = alpha * acc + _dot(p.astype(BF16), vc.astype(BF16))
    return m_new, l, acc


def _diff_finish(carry, n, lam, g, layer):
    _, l, acc = carry
    o = acc[:n] / l[:n] - lam * (acc[n:] / l[n:])
    return _rms(o, g) * (1.0 - _lambda_init(layer))


def _upper_ones(n):
    j = lax.broadcasted_iota(jnp.int32, (n, n), 0)
    s = lax.broadcasted_iota(jnp.int32, (n, n), 1)
    return jnp.where(j > s, 1.0, 0.0).astype(BF16)


def _later_ones(n):
    s = lax.broadcasted_iota(jnp.int32, (n, n), 0)
    j = lax.broadcasted_iota(jnp.int32, (n, n), 1)
    return jnp.where(j > s, 1.0, 0.0).astype(BF16)


def _sb_update(carry, qb, kc, vc, mask, u):
    c, acc = carry
    z = _dot_nt(qb, kc.astype(BF16)) * (HEAD_DIM ** -0.5)
    ls = _log_sigmoid(z)
    l1m = ls - z
    if mask is not None:
        l1m = jnp.where(mask, l1m, 0.0)
    hi, lo = _split2(l1m)
    between = _dot(hi, u) + _dot(lo, u)
    a = jnp.exp(ls + between + c)
    if mask is not None:
        a = jnp.where(mask, a, 0.0)
    acc = acc + _dot(a.astype(BF16), vc.astype(BF16))
    c = c + jnp.sum(l1m, axis=-1, keepdims=True)
    return c, acc


def _attn_prompt_kernel(*refs, mode, tq, layer):
    if mode == "diff":
        q_ref, k_ref, v_ref, cos_ref, sin_ref, lamp_ref, g_ref, o_ref = refs
    else:
        q_ref, k_ref, v_ref, o_ref = refs
    i = pl.program_id(2)
    rows = lax.broadcasted_iota(jnp.int32, (tq, tq), 0)
    cols = lax.broadcasted_iota(jnp.int32, (tq, tq), 1)

    def chunk(j):
        start = pl.multiple_of(N_META + j * tq, SUBLANES)
        return k_ref[0, pl.ds(start, tq), :], v_ref[0, pl.ds(start, tq), :]

    k_meta = k_ref[0, 0:N_META, :]
    v_meta = v_ref[0, 0:N_META, :]

    if mode == "diff":
        qs = _diff_queries(_rope(q_ref[...], cos_ref[...], sin_ref[...]))
        carry = (jnp.full((2 * tq, 1), -jnp.inf, F32), jnp.zeros((2 * tq, 1), F32),
                 jnp.zeros((2 * tq, HEAD_DIM), F32))
        carry = _diff_update(carry, qs, k_meta, v_meta, None)

        def body(j, c):
            kc, vc = chunk(j)
            return _diff_update(c, qs, kc, vc, None)

        carry = lax.fori_loop(0, i, body, carry)
        kc, vc = chunk(i)
        causal = cols <= rows
        carry = _diff_update(carry, qs, kc, vc, jnp.concatenate([causal, causal], axis=0))
        o_ref[...] = _diff_finish(carry, tq, _diff_lambda(lamp_ref, layer), g_ref[...], layer)
    else:
        qb = q_ref[...].astype(BF16)
        u = _upper_ones(tq)
        carry = (jnp.zeros((tq, 1), F32), jnp.zeros((tq, HEAD_DIM), F32))
        kc, vc = chunk(i)
        carry = _sb_update(carry, qb, kc, vc, cols < rows, u)

        def body(it, c):
            kc, vc = chunk(i - 1 - it)
            return _sb_update(c, qb, kc, vc, None, u)

        carry = lax.fori_loop(0, i, body, carry)
        carry = _sb_update(carry, qb, k_meta, v_meta, None, _upper_ones(N_META))
        o_ref[...] = carry[1]


def _attn_prompt(proj, kp, vp, mode, layer, b, s, tq, extra):
    r = proj.shape[0]
    nq = s // tq
    in_specs = [
        pl.BlockSpec((tq, HEAD_DIM), lambda bb, h, i: (bb * nq + i, h)),
        pl.BlockSpec((1, N_META + s, HEAD_DIM), lambda bb, h, i: (bb, 0, h)),
        pl.BlockSpec((1, N_META + s, HEAD_DIM), lambda bb, h, i: (bb, 0, h)),
    ]
    args = [proj, kp, vp]
    if mode == "diff":
        cos_t, sin_t, lamp, g = extra
        in_specs += [
            pl.BlockSpec((tq, LANES), lambda bb, h, i: (i, 0)),
            pl.BlockSpec((tq, LANES), lambda bb, h, i: (i, 0)),
            pl.BlockSpec(lamp.shape, lambda bb, h, i: (0, 0)),
            pl.BlockSpec((1, HEAD_DIM), lambda bb, h, i: (0, 0)),
        ]
        args += [cos_t, sin_t, lamp, g]
    return pl.pallas_call(
        functools.partial(_attn_prompt_kernel, mode=mode, tq=tq, layer=layer),
        out_shape=jax.ShapeDtypeStruct((r, 2 * W_MIX), F32),
        grid=(b, N_HEADS, nq),
        in_specs=in_specs,
        out_specs=pl.BlockSpec((tq, HEAD_DIM), lambda bb, h, i: (bb * nq + i, h)),
        compiler_params=_cparams(("parallel", "parallel", "arbitrary")),
        name="attn_prompt_" + mode,
    )(*args)


def _attn_tail_kernel(pt_ref, *refs, mode, layer, ds, n_steps):
    del pt_ref
    pps = PAGES_PER_STEP
    if mode == "diff":
        (q_ref, k_ref, v_ref, cos_ref, sin_ref, lamp_ref, g_ref) = refs[:7]
        refs = refs[7:]
    else:
        (q_ref, k_ref, v_ref) = refs[:3]
        refs = refs[3:]
    kc_refs = refs[:pps]
    vc_refs = refs[pps:2 * pps]
    _mix_in, o_ref, q_scr, m_scr, l_scr, acc_scr = refs[2 * pps:]
    b = pl.program_id(0)
    st = pl.program_id(1)
    row0 = pl.multiple_of(N_META + SAMPLE_ROWS * b, SUBLANES)

    def head(x, h):
        return x[:, HEAD_DIM * h:HEAD_DIM * (h + 1)]

    @pl.when((b == 0) & (st == 0))
    def _():
        o_ref[...] = jnp.zeros_like(o_ref)
        rows = lax.broadcasted_iota(jnp.int32, (N_META, N_META), 0)
        cols = lax.broadcasted_iota(jnp.int32, (N_META, N_META), 1)
        qm = q_ref[0:N_META, :]
        km = k_ref[0:N_META, :]
        vm = v_ref[0:N_META, :]
        for h in range(N_HEADS):
            if mode == "diff":
                qs = _diff_queries(_rope(head(qm, h), cos_ref[0:N_META, :], sin_ref[0:N_META, :]))
                causal = cols <= rows
                carry = (jnp.full((2 * N_META, 1), -jnp.inf, F32), jnp.zeros((2 * N_META, 1), F32),
                         jnp.zeros((2 * N_META, HEAD_DIM), F32))
                carry = _diff_update(carry, qs, head(km, h), head(vm, h),
                                     jnp.concatenate([causal, causal], axis=0))
                o = _diff_finish(carry, N_META, _diff_lambda(lamp_ref, layer), g_ref[...], layer)
            else:
                carry = (jnp.zeros((N_META, 1), F32), jnp.zeros((N_META, HEAD_DIM), F32))
                carry = _sb_update(carry, head(qm, h).astype(BF16), head(km, h), head(vm, h),
                                   cols < rows, _upper_ones(N_META))
                o = carry[1]
            o_ref[0:N_META, HEAD_DIM * h:HEAD_DIM * (h + 1)] = o

    def scores(kflat):
        k_hi, k_lo = _split2(kflat)
        qcat = q_scr[...]
        return _dot_nt(qcat, jnp.concatenate([k_hi, k_hi], axis=1)) + _dot_nt(qcat[:, :HEAD_DIM], k_lo)

    def weighted_values(p, vflat):
        p_hi, p_lo = _split2(p)
        v_hi, v_lo = _split2(vflat)
        r = _dot(jnp.concatenate([p_hi, p_lo], axis=0), jnp.concatenate([v_hi, v_lo], axis=1))
        return r[:LANES, :HEAD_DIM] + r[:LANES, HEAD_DIM:] + r[LANES:, :HEAD_DIM]

    def cumsum_and_total(l1m):
        hi, lo = _split2(l1m)
        ones = jnp.ones((LANES, LANES), BF16)
        r = _dot(jnp.concatenate([hi, lo], axis=0), jnp.concatenate([_upper_ones(LANES), ones], axis=1))
        return r[:LANES, :LANES] + r[LANES:, :LANES], r[:LANES, LANES:] + r[LANES:, LANES:]

    def flat(ref):
        return ref[0].reshape(LANES * N_HEADS, HEAD_DIM)

    col = lax.broadcasted_iota(jnp.int32, (LANES, LANES), 0)
    lane = lax.broadcasted_iota(jnp.int32, (LANES, LANES), 1)
    page_lane = lax.broadcasted_iota(jnp.int32, (LANES, LANES * N_HEADS), 1)
    page_col = lax.broadcasted_iota(jnp.int32, (LANES, LANES * N_HEADS), 0)
    own_head = (page_lane % N_HEADS) == (page_col // 16)

    @pl.when(st == 0)
    def _():
        q8 = q_ref[pl.ds(row0, SAMPLE_ROWS), :]
        blocks = []
        for h in range(N_HEADS):
            qh = head(q8, h)
            if mode == "diff":
                qh = _diff_queries_f32(_rope(qh, cos_ref[pl.ds(row0, SAMPLE_ROWS), :],
                                             sin_ref[pl.ds(row0, SAMPLE_ROWS), :]))
            else:
                qh = jnp.concatenate([qh, jnp.zeros_like(qh)], axis=0)
            blocks.append(qh)
        q_hi, q_lo = _split2(jnp.concatenate(blocks, axis=0))
        q_scr[...] = jnp.concatenate([q_hi, q_lo], axis=1)

        k8 = k_ref[pl.ds(row0, SAMPLE_ROWS), :]
        v8 = v_ref[pl.ds(row0, SAMPLE_ROWS), :]
        pad = jnp.zeros((LANES - SAMPLE_ROWS * N_HEADS, HEAD_DIM), F32)
        k_new = jnp.concatenate([head(k8, h) for h in range(N_HEADS)] + [pad], axis=0)
        v_new = jnp.concatenate([head(v8, h) for h in range(N_HEADS)] + [pad], axis=0)
        tok = lane % SAMPLE_ROWS
        qrow = col % SAMPLE_ROWS
        ok = (lane // SAMPLE_ROWS == col // 16) & (tok < ds)
        sc = scores(k_new)
        if mode == "diff":
            sc = jnp.where(ok & (tok <= qrow), sc, -jnp.inf)
            m = jnp.max(sc, axis=1, keepdims=True)
            p = jnp.exp(sc - m)
            m_scr[...] = jnp.broadcast_to(m, (LANES, LANES))
            l_scr[...] = jnp.broadcast_to(jnp.sum(p, axis=1, keepdims=True), (LANES, LANES))
        else:
            ok = ok & (tok < qrow)
            z = sc * (HEAD_DIM ** -0.5)
            ls = _log_sigmoid(z)
            between, total = cumsum_and_total(jnp.where(ok, ls - z, 0.0))
            p = jnp.where(ok, jnp.exp(ls + between), 0.0)
            m_scr[...] = total
            l_scr[...] = jnp.zeros_like(l_scr)
        acc_scr[...] = weighted_values(p, v_new)

    if mode == "diff":
        scs = [jnp.where(own_head, scores(flat(kc_refs[j])), -jnp.inf) for j in range(pps)]
        m_old = m_scr[:, 0:1]
        m_new = m_old
        for sc in scs:
            m_new = jnp.maximum(m_new, jnp.max(sc, axis=1, keepdims=True))
        alpha = jnp.exp(m_old - m_new)
        l = alpha * l_scr[:, 0:1]
        acc = alpha * acc_scr[...]
        for j in range(pps):
            p = jnp.exp(scs[j] - m_new)
            l = l + jnp.sum(p, axis=1, keepdims=True)
            acc = acc + weighted_values(p, flat(vc_refs[j]))
        m_scr[...] = jnp.broadcast_to(m_new, (LANES, LANES))
        l_scr[...] = jnp.broadcast_to(l, (LANES, LANES))
        acc_scr[...] = acc
    else:
        running = m_scr[...]
        acc = acc_scr[...]
        n_blk = N_HEADS
        for j in reversed(range(pps)):
            z = scores(flat(kc_refs[j])) * (HEAD_DIM ** -0.5)
            ls = _log_sigmoid(z)
            l1m = jnp.where(own_head, ls - z, 0.0)
            ps = [None] * n_blk
            for i in reversed(range(n_blk)):
                blk = slice(LANES * i, LANES * (i + 1))
                between, total = cumsum_and_total(l1m[:, blk])
                ps[i] = jnp.where(own_head[:, blk], jnp.exp(ls[:, blk] + between + running), 0.0)
                running = running + total
            acc = acc + weighted_values(jnp.concatenate(ps, axis=1), flat(vc_refs[j]))
        m_scr[...] = running
        acc_scr[...] = acc

    @pl.when(st == n_steps - 1)
    def _():
        acc = acc_scr[...]
        real = lax.broadcasted_iota(jnp.int32, (SAMPLE_ROWS, HEAD_DIM), 0) < ds
        outs = []
        if mode == "diff":
            acc = acc / l_scr[:, 0:1]
            lam = _diff_lambda(lamp_ref, layer)
            for h in range(N_HEADS):
                o = acc[16 * h:16 * h + 8] - lam * acc[16 * h + 8:16 * h + 16]
                o = _rms(o, g_ref[...]) * (1.0 - _lambda_init(layer))
                outs.append(jnp.where(real, o, 0.0))
        else:
            for h in range(N_HEADS):
                outs.append(jnp.where(real, acc[16 * h:16 * h + 8], 0.0))
        o_ref[pl.ds(row0, SAMPLE_ROWS), :] = jnp.concatenate(outs, axis=1)


def _attn_tail(mix, proj, k_tail, k_tail_col, cache_k, cache_v, page_table, mode, layer, b, s, ds, extra):
    r = proj.shape[0]
    tb = (b * s) // TAIL_ROWS
    db, n_pages = page_table.shape
    pps = PAGES_PER_STEP
    assert n_pages % pps == 0
    n_steps = n_pages // pps
    page = cache_k.shape[1]
    assert page == LANES and cache_k.shape[2:] == (N_HEADS, HEAD_DIM)
    page_block = (1, page, N_HEADS, HEAD_DIM)
    vcol = 2

    in_specs = [
        pl.BlockSpec((TAIL_ROWS, W_MIX), lambda bb, st, pt: (tb, 0)),
        pl.BlockSpec((TAIL_ROWS, W_MIX), lambda bb, st, pt: k_tail_col),
        pl.BlockSpec((TAIL_ROWS, W_MIX), lambda bb, st, pt: (tb, vcol)),
    ]
    args = [proj, k_tail, proj]
    if mode == "diff":
        cos_t, sin_t, lamp, g = extra
        in_specs += [
            pl.BlockSpec((TAIL_ROWS, LANES), lambda bb, st, pt: (s // TAIL_ROWS, 0)),
            pl.BlockSpec((TAIL_ROWS, LANES), lambda bb, st, pt: (s // TAIL_ROWS, 0)),
            pl.BlockSpec(lamp.shape, lambda bb, st, pt: (0, 0)),
            pl.BlockSpec((1, HEAD_DIM), lambda bb, st, pt: (0, 0)),
        ]
        args += [cos_t, sin_t, lamp, g]

    def page_spec(j):
        if mode == "diff":
            return pl.BlockSpec(page_block, lambda bb, st, pt: (pt[bb, st * pps + j], 0, 0, 0))
        return pl.BlockSpec(page_block, lambda bb, st, pt: (pt[bb, n_pages - (st + 1) * pps + j], 0, 0, 0))

    in_specs += [page_spec(j) for j in range(pps)] + [page_spec(j) for j in range(pps)]
    args += [cache_k] * pps + [cache_v] * pps
    in_specs.append(pl.BlockSpec(memory_space=pl.ANY))
    args.append(mix)
    n_in = len(args)

    grid_spec = pltpu.PrefetchScalarGridSpec(
        num_scalar_prefetch=1,
        grid=(db, n_steps),
        in_specs=in_specs,
        out_specs=pl.BlockSpec((TAIL_ROWS, W_MIX), lambda bb, st, pt: (tb, 0)),
        scratch_shapes=[
            pltpu.VMEM((LANES, 2 * HEAD_DIM), BF16),
            pltpu.VMEM((LANES, LANES), F32),
            pltpu.VMEM((LANES, LANES), F32),
            pltpu.VMEM((LANES, HEAD_DIM), F32),
        ],
    )
    return pl.pallas_call(
        functools.partial(_attn_tail_kernel, mode=mode, layer=layer, ds=ds, n_steps=n_steps),
        out_shape=jax.ShapeDtypeStruct((r, 2 * W_MIX), F32),
        grid_spec=grid_spec,
        input_output_aliases={n_in: 0},
        compiler_params=_cparams(("arbitrary", "arbitrary")),
        name="attn_tail_" + mode,
    )(page_table, *args)


def _glu(val, gate):
    return val * _sigmoid(gate)


def _dwconv_rows(buf_ref, w_ref, n_rows, row_strip):
    off = HALO - (CONV_W - 1)
    strips = []
    for r0 in range(0, n_rows, row_strip):
        cols = []
        for c0 in range(0, W_MIX, LANES):
            acc = jnp.zeros((row_strip, LANES), F32)
            for k in range(CONV_W):
                acc = acc + w_ref[k:k + 1, c0:c0 + LANES] * buf_ref[r0 + off + k:r0 + off + k + row_strip, c0:c0 + LANES]
            cols.append(acc)
        strips.append(jnp.concatenate(cols, axis=1))
    return strips[0] if len(strips) == 1 else jnp.concatenate(strips, axis=0)


def _ln_silu(y, g, b):
    yc = y - jnp.mean(y, axis=-1, keepdims=True)
    yn = yc * lax.rsqrt(jnp.mean(yc * yc, axis=-1, keepdims=True) + EPS) * g + b
    return yn * _sigmoid(yn)


def _conv_prompt_kernel(val_ref, gate_ref, tval_ref, tgate_ref, w_ref, b_ref, g_ref, beta_ref, _mix_in,
                        o_ref, cp_ref, buf_scr, *, t):
    i = pl.program_id(1)

    @pl.when(i == 0)
    def _():
        buf_scr[0:HALO - N_META, :] = jnp.zeros((HALO - N_META, W_MIX), F32)
        buf_scr[HALO - N_META:HALO, :] = _glu(tval_ref[0:N_META, :], tgate_ref[0:N_META, :])

    @pl.when(i > 0)
    def _():
        buf_scr[0:HALO, :] = buf_scr[t:t + HALO, :]

    buf_scr[HALO:HALO + t, :] = _glu(val_ref[...], gate_ref[...])
    y = _dwconv_rows(buf_scr, w_ref, t, 64) + b_ref[...]
    o_ref[...] = _ln_silu(y, g_ref[...], beta_ref[...])

    @pl.when(i == pl.num_programs(1) - 1)
    def _():
        cp_ref[0] = buf_scr[HALO + t - (CONV_W - 1):HALO + t, :]


def _conv_prompt(mix, proj, w, bias, g, beta, b, s, t):
    r = proj.shape[0]
    tb = (b * s) // TAIL_ROWS
    nb = s // t
    vcol, gcol = 3, 4
    row = lambda x: x.reshape(1, W_MIX)
    full = lambda bb, i: (0, 0)
    return pl.pallas_call(
        functools.partial(_conv_prompt_kernel, t=t),
        out_shape=[jax.ShapeDtypeStruct((r, 2 * W_MIX), F32),
                   jax.ShapeDtypeStruct((b, CONV_W - 1, W_MIX), F32)],
        grid=(b, nb),
        in_specs=[
            pl.BlockSpec((t, W_MIX), lambda bb, i: (bb * nb + i, vcol)),
            pl.BlockSpec((t, W_MIX), lambda bb, i: (bb * nb + i, gcol)),
            pl.BlockSpec((TAIL_ROWS, W_MIX), lambda bb, i: (tb, vcol)),
            pl.BlockSpec((TAIL_ROWS, W_MIX), lambda bb, i: (tb, gcol)),
            pl.BlockSpec((CONV_W, W_MIX), full),
            pl.BlockSpec((1, W_MIX), full),
            pl.BlockSpec((1, W_MIX), full),
            pl.BlockSpec((1, W_MIX), full),
            pl.BlockSpec(memory_space=pl.ANY),
        ],
        out_specs=[pl.BlockSpec((t, W_MIX), lambda bb, i: (bb * nb + i, 1)),
                   pl.BlockSpec((1, CONV_W - 1, W_MIX), lambda bb, i: (bb, 0, 0))],
        scratch_shapes=[pltpu.VMEM((HALO + t, W_MIX), F32)],
        input_output_aliases={8: 0},
        compiler_params=_cparams(("parallel", "arbitrary")),
        name="conv_prompt",
    )(proj, proj, proj, proj, w, row(bias), row(g), row(beta), mix)


def _conv_tail_kernel(tval_ref, tgate_ref, st_ref, w_ref, b_ref, g_ref, beta_ref, _mix_in,
                      o_ref, cs_ref, buf_scr, *, db, ds):
    u = _glu(tval_ref[...], tgate_ref[...])
    o_ref[...] = jnp.zeros_like(o_ref)
    keep = CONV_W - 1

    def finish(n_rows):
        y = _dwconv_rows(buf_scr, w_ref, n_rows, n_rows) + b_ref[...]
        return _ln_silu(y, g_ref[...], beta_ref[...])

    buf_scr[0:HALO, :] = jnp.zeros((HALO, W_MIX), F32)
    buf_scr[HALO:HALO + N_META, :] = u[0:N_META]
    o_ref[0:N_META, :] = finish(N_META)
    real = lax.broadcasted_iota(jnp.int32, (SAMPLE_ROWS, W_MIX), 0) < ds
    for b in range(db):
        r0 = N_META + SAMPLE_ROWS * b
        buf_scr[0:HALO, :] = st_ref[b]
        buf_scr[HALO:HALO + SAMPLE_ROWS, :] = u[r0:r0 + SAMPLE_ROWS]
        o_ref[r0:r0 + SAMPLE_ROWS, :] = jnp.where(real, finish(SAMPLE_ROWS), 0.0)
        cs_ref[b] = buf_scr[HALO + ds - keep:HALO + ds, :]


def _conv_tail(mix, proj, state, w, bias, g, beta, b, s, ds):
    r = proj.shape[0]
    tb = (b * s) // TAIL_ROWS
    db = state.shape[0]
    keep = CONV_W - 1
    vcol, gcol = 3, 4
    state_pad = jnp.pad(state, ((0, 0), (HALO - keep, 0), (0, 0)))
    row = lambda x: x.reshape(1, W_MIX)
    full2 = lambda i: (0, 0)
    full3 = lambda i: (0, 0, 0)
    return pl.pallas_call(
        functools.partial(_conv_tail_kernel, db=db, ds=ds),
        out_shape=[jax.ShapeDtypeStruct((r, 2 * W_MIX), F32),
                   jax.ShapeDtypeStruct((db, keep, W_MIX), F32)],
        grid=(1,),
        in_specs=[
            pl.BlockSpec((TAIL_ROWS, W_MIX), lambda i: (tb, vcol)),
            pl.BlockSpec((TAIL_ROWS, W_MIX), lambda i: (tb, gcol)),
            pl.BlockSpec((db, HALO, W_MIX), full3),
            pl.BlockSpec((CONV_W, W_MIX), full2),
            pl.BlockSpec((1, W_MIX), full2),
            pl.BlockSpec((1, W_MIX), full2),
            pl.BlockSpec((1, W_MIX), full2),
            pl.BlockSpec(memory_space=pl.ANY),
        ],
        out_specs=[pl.BlockSpec((TAIL_ROWS, W_MIX), lambda i: (tb, 1)),
                   pl.BlockSpec((db, keep, W_MIX), full3)],
        scratch_shapes=[pltpu.VMEM((HALO + N_META, W_MIX), F32)],
        input_output_aliases={7: 0},
        compiler_params=_cparams(("arbitrary",)),
        name="conv_tail",
    )(proj, proj, state_pad, w, row(bias), row(g), row(beta), mix)


def _group_ref(cb, c, lvl):
    grp, half = 2 << lvl, 1 << lvl
    if grp >= 2 * SUBLANES:
        x3 = cb.reshape(c // grp, grp, LANES)
        ref = x3[:, half - 1:half, :]
        return jnp.broadcast_to(ref, (c // grp, grp, LANES)).reshape(c, LANES)
    x3 = cb.reshape(c // SUBLANES, SUBLANES, LANES)
    sub = lax.broadcasted_iota(jnp.int32, x3.shape, 1)
    out = None
    for g0 in range(0, SUBLANES, grp):
        ref = jnp.broadcast_to(x3[:, g0 + half - 1:g0 + half, :], x3.shape)
        out = ref if out is None else jnp.where(sub >= g0, ref, out)
    return out.reshape(c, LANES)


def _pad_rows(a):
    c = a.shape[0]
    if c % LANES == 0:
        return a
    return jnp.concatenate([a, jnp.zeros((LANES - c % LANES, a.shape[1]), a.dtype)], axis=0)


def _hgrn_chunk(hq, hf, hi, lb, st, c, valid, precise=False):
    if precise:
        mm_nt = lambda x, y: _dot3_nt(_split2(x), _split2(y))
        mm = lambda x, y: _dot3(_split2(x), _split2(y))
    else:
        mm_nt = lambda x, y: _dot_nt(x.astype(BF16), y.astype(BF16))
        mm = lambda x, y: _dot(x.astype(BF16), y.astype(BF16))
    f = lb + (1.0 - lb) * _sigmoid(hf)
    logf = jnp.log(f)
    k = 1.0 - f
    if valid is not None:
        logf = jnp.where(valid, logf, 0.0)
        k = jnp.where(valid, k, 0.0)
    row = lax.broadcasted_iota(jnp.int32, (c, LANES), 0)
    cb = logf
    sh = 1
    while sh < c:
        cb = cb + jnp.where(row >= sh, pltpu.roll(cb, sh, axis=0), 0.0)
        sh *= 2
    rows = lax.broadcasted_iota(jnp.int32, (c, c), 0)
    cols = lax.broadcasted_iota(jnp.int32, (c, c), 1)
    a = jnp.where(rows == cols, mm_nt(hq, k), 0.0)
    lvl = 0
    while (1 << lvl) < c:
        ref = _group_ref(cb, c, lvl)
        right = ((row >> lvl) & 1) == 1
        qe = jnp.where(right, hq * jnp.exp(jnp.minimum(cb - ref, 0.0)), 0.0)
        ke = jnp.where(right, 0.0, k * jnp.exp(jnp.minimum(ref - cb, 0.0)))
        a = a + jnp.where((rows >> (lvl + 1)) == (cols >> (lvl + 1)), mm_nt(qe, ke), 0.0)
        lvl += 1
    o = mm(a, hi) + mm_nt(hq * jnp.exp(cb), st)
    last = cb[c - 1:c, :]
    st_new = st * jnp.exp(last) + mm(jnp.transpose(_pad_rows(hi)), _pad_rows(k * jnp.exp(last - cb)))
    return o, st_new


def _hgrn_lb(lbl_ref):
    lg = lbl_ref[...]
    m = jnp.max(lg, axis=0, keepdims=True)
    e = jnp.exp(lg - m)
    sm = e / jnp.sum(e, axis=0, keepdims=True)
    return (sm[0:1] + sm[1:2]) - sm[0:1]


def _hgrn_out(o, g, hg):
    return _rms(o, g) * (hg * _sigmoid(hg))


def _hgrn_prompt_kernel(hq_ref, hf_ref, hi_ref, hg_ref, lbl_ref, g_ref, st0_ref, _mix_in, o_ref, s_ref, st_scr, *, c):
    ci = pl.program_id(2)

    @pl.when(ci == 0)
    def _():
        st_scr[...] = st0_ref[0]

    o, st = _hgrn_chunk(hq_ref[...], hf_ref[...], hi_ref[...], _hgrn_lb(lbl_ref), st_scr[...], c, None)
    st_scr[...] = st
    o_ref[...] = _hgrn_out(o, g_ref[...], hg_ref[...])

    @pl.when(ci == pl.num_programs(2) - 1)
    def _():
        s_ref[0, 0] = jnp.transpose(st)


def _hgrn_cols():
    base = 3 * W_MIX // HEAD_DIM
    return base, base + N_HEADS, base + 2 * N_HEADS, base + 3 * N_HEADS


def _hgrn_prompt(mix, proj, lb_logits, g, st_meta, b, s, c):
    r = proj.shape[0]
    nc = s // c
    qc, fc, ic, gc = _hgrn_cols()
    blk = lambda col: pl.BlockSpec((c, HEAD_DIM), lambda bb, h, ci: (bb * nc + ci, col + h))
    return pl.pallas_call(
        functools.partial(_hgrn_prompt_kernel, c=c),
        out_shape=[jax.ShapeDtypeStruct((r, 2 * W_MIX), F32),
                   jax.ShapeDtypeStruct((b, N_HEADS, HEAD_DIM, HEAD_DIM), F32)],
        grid=(b, N_HEADS, nc),
        in_specs=[
            blk(qc), blk(fc), blk(ic), blk(gc),
            pl.BlockSpec((2, HEAD_DIM), lambda bb, h, ci: (0, h)),
            pl.BlockSpec((1, HEAD_DIM), lambda bb, h, ci: (0, 0)),
            pl.BlockSpec((1, HEAD_DIM, HEAD_DIM), lambda bb, h, ci: (h, 0, 0)),
            pl.BlockSpec(memory_space=pl.ANY),
        ],
        out_specs=[pl.BlockSpec((c, HEAD_DIM), lambda bb, h, ci: (bb * nc + ci, N_HEADS + h)),
                   pl.BlockSpec((1, 1, HEAD_DIM, HEAD_DIM), lambda bb, h, ci: (bb, h, 0, 0))],
        scratch_shapes=[pltpu.VMEM((HEAD_DIM, HEAD_DIM), F32)],
        input_output_aliases={7: 0},
        compiler_params=_cparams(("parallel", "parallel", "arbitrary")),
        name="hgrn_prompt",
    )(proj, proj, proj, proj, lb_logits, g.reshape(1, HEAD_DIM), st_meta, mix)


def _hgrn_tail_kernel(hq_ref, hf_ref, hi_ref, hg_ref, lbl_ref, g_ref, s0_ref, _mix_in,
                      o_ref, stm_ref, sn_ref, st_scr, *, db, ds):
    j = pl.program_id(1)
    c = 2 * SAMPLE_ROWS
    row = lax.broadcasted_iota(jnp.int32, (c, HEAD_DIM), 0)
    lb = _hgrn_lb(lbl_ref)

    @pl.when(j == 0)
    def _():
        o, st = _hgrn_chunk(hq_ref[...], hf_ref[...], hi_ref[...], lb, jnp.zeros((HEAD_DIM, HEAD_DIM), F32), c, None)
        stm_ref[0] = st
        o_ref[...] = _hgrn_out(o, g_ref[...], hg_ref[...])

    @pl.when((j > 0) & (j <= db // 2))
    def _():
        outs = []
        for half in range(2):
            valid = (row >= half * SAMPLE_ROWS) & (row < half * SAMPLE_ROWS + ds)
            st0 = jnp.transpose(s0_ref[half, 0])
            o, st = _hgrn_chunk(hq_ref[...], hf_ref[...], hi_ref[...], lb, st0, c, valid, precise=True)
            sn_ref[half, 0] = jnp.transpose(st)
            outs.append(jnp.where(valid, o, 0.0))
        o_ref[...] = _hgrn_out(outs[0] + outs[1], g_ref[...], hg_ref[...])

    @pl.when(j > db // 2)
    def _():
        o_ref[...] = jnp.zeros_like(o_ref)


def _hgrn_tail(mix, proj, lb_logits, g, state, b, s, ds):
    r = proj.shape[0]
    db = state.shape[0]
    assert db % 2 == 0
    c = 2 * SAMPLE_ROWS
    row_blk0 = (b * s) // c
    n_chunks = TAIL_ROWS // c
    qc, fc, ic, gc = _hgrn_cols()
    blk = lambda col: pl.BlockSpec((c, HEAD_DIM), lambda h, j: (row_blk0 + j, col + h))
    pair = lambda h, j: (jnp.clip(j - 1, 0, db // 2 - 1), h, 0, 0)
    return pl.pallas_call(
        functools.partial(_hgrn_tail_kernel, db=db, ds=ds),
        out_shape=[jax.ShapeDtypeStruct((r, 2 * W_MIX), F32),
                   jax.ShapeDtypeStruct((N_HEADS, HEAD_DIM, HEAD_DIM), F32),
                   jax.ShapeDtypeStruct((db, N_HEADS, HEAD_DIM, HEAD_DIM), F32)],
        grid=(N_HEADS, n_chunks),
        in_specs=[
            blk(qc), blk(fc), blk(ic), blk(gc),
            pl.BlockSpec((2, HEAD_DIM), lambda h, j: (0, h)),
            pl.BlockSpec((1, HEAD_DIM), lambda h, j: (0, 0)),
            pl.BlockSpec((2, 1, HEAD_DIM, HEAD_DIM), pair),
            pl.BlockSpec(memory_space=pl.ANY),
        ],
        out_specs=[pl.BlockSpec((c, HEAD_DIM), lambda h, j: (row_blk0 + j, N_HEADS + h)),
                   pl.BlockSpec((1, HEAD_DIM, HEAD_DIM), lambda h, j: (h, 0, 0)),
                   pl.BlockSpec((2, 1, HEAD_DIM, HEAD_DIM), pair)],
        scratch_shapes=[pltpu.VMEM((HEAD_DIM, HEAD_DIM), F32)],
        input_output_aliases={7: 0},
        compiler_params=_cparams(("arbitrary", "arbitrary")),
        name="hgrn_tail",
    )(proj, proj, proj, proj, lb_logits, g.reshape(1, HEAD_DIM), state, mix)


def _router_kernel(h_ref, g_ref, w_ref, eid_ref, gate_ref, a_ref):
    a = _rms(h_ref[...], g_ref[...])
    a_ref[...] = a.astype(BF16)
    logits = jnp.dot(a, w_ref[...], preferred_element_type=F32, precision=lax.Precision.HIGHEST)
    lane = lax.broadcasted_iota(jnp.int32, logits.shape, 1)
    logits = jnp.where(lane < N_EXPERTS, logits, -jnp.inf)
    m1 = jnp.max(logits, axis=-1, keepdims=True)
    i1 = jnp.min(jnp.where(logits == m1, lane, LANES), axis=-1, keepdims=True)
    rest = jnp.where(lane == i1, -jnp.inf, logits)
    m2 = jnp.max(rest, axis=-1, keepdims=True)
    i2 = jnp.min(jnp.where(rest == m2, lane, LANES), axis=-1, keepdims=True)
    e2 = jnp.exp(m2 - m1)
    g1 = 1.0 / (1.0 + e2)
    g2 = e2 / (1.0 + e2)
    eid_ref[...] = jnp.where(lane == 0, i1, jnp.where(lane == 1, i2, 0))
    gate_ref[...] = jnp.where(lane == 0, g1, jnp.where(lane == 1, g2, 0.0))


def _router(h, g, w_router, tm):
    r, d = h.shape
    w_pad = jnp.pad(w_router, ((0, 0), (0, LANES - N_EXPERTS)))
    eid, gate, a = pl.pallas_call(
        _router_kernel,
        out_shape=[jax.ShapeDtypeStruct((r, LANES), jnp.int32), jax.ShapeDtypeStruct((r, LANES), F32),
                   jax.ShapeDtypeStruct((r, d), BF16)],
        grid=(r // tm,),
        in_specs=[
            pl.BlockSpec((tm, d), lambda i: (i, 0)),
            pl.BlockSpec((1, d), lambda i: (0, 0)),
            pl.BlockSpec((d, LANES), lambda i: (0, 0)),
        ],
        out_specs=[pl.BlockSpec((tm, LANES), lambda i: (i, 0)), pl.BlockSpec((tm, LANES), lambda i: (i, 0)),
                   pl.BlockSpec((tm, d), lambda i: (i, 0))],
        compiler_params=_cparams(("parallel",)),
        name="moe_router",
    )(h, g.reshape(1, d), w_pad)
    return eid, gate, a


def _moe_permute_kernel(first_ref, last_ref, tok_ref, a_ref, o_ref, *, sblk):
    i = pl.program_id(0)
    k = pl.program_id(1)
    blk = first_ref[i] + k

    @pl.when(k == 0)
    def _():
        o_ref[...] = jnp.zeros_like(o_ref)

    @pl.when(blk <= last_ref[i])
    def _():
        col = lax.broadcasted_iota(jnp.int32, (o_ref.shape[0], sblk), 1)
        onehot = jnp.where(tok_ref[...] - blk * sblk == col, 1.0, 0.0).astype(BF16)
        o_ref[...] += _dot(onehot, a_ref[...]).astype(BF16)


def _moe_permute(a, slot_tok, tile_first, tile_last, tm, sblk):
    r, d = a.shape
    cap = slot_tok.shape[0]
    grid_spec = pltpu.PrefetchScalarGridSpec(
        num_scalar_prefetch=2,
        grid=(cap // tm, r // sblk),
        in_specs=[
            pl.BlockSpec((tm, 1), lambda i, k, tf, tl: (i, 0)),
            pl.BlockSpec((sblk, d), lambda i, k, tf, tl: (jnp.minimum(tf[i] + k, jnp.maximum(tl[i], 0)), 0)),
        ],
        out_specs=pl.BlockSpec((tm, d), lambda i, k, tf, tl: (i, 0)),
    )
    return pl.pallas_call(
        functools.partial(_moe_permute_kernel, sblk=sblk),
        out_shape=jax.ShapeDtypeStruct((cap, d), BF16),
        grid_spec=grid_spec,
        compiler_params=_cparams(("arbitrary", "arbitrary")),
        name="moe_permute",
    )(tile_first, tile_last, slot_tok.reshape(cap, 1), a)


MOE_GROUP = 4


def _moe_ffn_kernel(ge_ref, g0_ref, gn_ref, x_ref, sg_ref, wg_ref, wu_ref, wd_ref, _y_in, y_ref, acc_scr, sem, *, tm):
    i = pl.program_id(0)
    f = pl.program_id(1)
    del ge_ref
    n_valid = gn_ref[i]
    wg = wg_ref[0].astype(BF16)
    wu = wu_ref[0].astype(BF16)
    wd = wd_ref[0].astype(BF16)

    def out_copy(g):
        return pltpu.make_async_copy(acc_scr.at[g], y_ref.at[pl.ds((g0_ref[i] + g) * tm, tm)], sem.at[g])

    for g in range(MOE_GROUP):
        @pl.when(g < n_valid)
        def _():
            a = x_ref[g * tm:(g + 1) * tm, :]
            gate = _dot(a, wg)
            up = _dot(a, wu)
            t = (gate * _sigmoid(gate)) * up

            @pl.when(f == 0)
            def _():
                acc_scr[g] = jnp.zeros((tm, acc_scr.shape[2]), F32)

            acc_scr[g] += _dot(t.astype(BF16), wd)

            @pl.when(f == pl.num_programs(1) - 1)
            def _():
                acc_scr[g] = acc_scr[g] * sg_ref[g * tm:(g + 1) * tm, :]
                out_copy(g).start()

    @pl.when(f == pl.num_programs(1) - 1)
    def _():
        for g in range(MOE_GROUP):
            @pl.when(g < n_valid)
            def _():
                out_copy(g).wait()


def _moe_ffn(x_sorted, slot_gate, wg, wu, wd, group_expert, group_first, group_count, n_groups, tm, tf):
    rows, d = x_sorted.shape
    cap = rows - (MOE_GROUP - 1) * tm
    ff = wg.shape[2]
    nf = ff // tf
    gm = MOE_GROUP * tm
    fidx = lambda i, f, gn: jnp.where(gn[i] > 0, f, nf - 1)
    grid_spec = pltpu.PrefetchScalarGridSpec(
        num_scalar_prefetch=3,
        grid=(n_groups, nf),
        in_specs=[
            pl.BlockSpec((pl.Element(gm), pl.Element(d)), lambda i, f, ge, g0, gn: (g0[i] * tm, 0),
                         pipeline_mode=pl.Buffered(1)),
            pl.BlockSpec((pl.Element(gm), pl.Element(1)), lambda i, f, ge, g0, gn: (g0[i] * tm, 0),
                         pipeline_mode=pl.Buffered(1)),
            pl.BlockSpec((1, d, tf), lambda i, f, ge, g0, gn: (ge[i], 0, fidx(i, f, gn))),
            pl.BlockSpec((1, d, tf), lambda i, f, ge, g0, gn: (ge[i], 0, fidx(i, f, gn))),
            pl.BlockSpec((1, tf, d), lambda i, f, ge, g0, gn: (ge[i], fidx(i, f, gn), 0)),
            pl.BlockSpec(memory_space=pl.ANY),
        ],
        out_specs=pl.BlockSpec(memory_space=pl.ANY),
        scratch_shapes=[pltpu.VMEM((MOE_GROUP, tm, d), F32), pltpu.SemaphoreType.DMA((MOE_GROUP,))],
    )
    return pl.pallas_call(
        functools.partial(_moe_ffn_kernel, tm=tm),
        out_shape=jax.ShapeDtypeStruct((cap, d), F32),
        grid_spec=grid_spec,
        input_output_aliases={8: 0},
        compiler_params=_cparams(("arbitrary", "arbitrary")),
        name="moe_ffn",
    )(group_expert, group_first, group_count, x_sorted, slot_gate.reshape(rows, 1), wg, wu, wd,
      jnp.zeros((cap, d), F32))


def _moe_combine_kernel(ws_ref, h_ref, eid_ref, dest_ref, g_ref, *refs, n_main, win):
    y_refs = refs[:N_EXPERTS]
    yp_ref, yt_ref = refs[N_EXPERTS:]
    j = pl.program_id(0)
    eid = eid_ref[...]
    dest = dest_ref[...]
    col = lax.broadcasted_iota(jnp.int32, (TAIL_ROWS, win), 1)
    acc = h_ref[...]
    for e in range(N_EXPERTS):
        start = ws_ref[j * N_EXPERTS + e] * SUBLANES
        onehot = jnp.zeros((TAIL_ROWS, win), BF16)
        for k in range(TOP_K):
            rel = jnp.where(eid[:, k:k + 1] == e, dest[:, k:k + 1] - start, -1)
            onehot = onehot + jnp.where(rel == col, 1.0, 0.0).astype(BF16)
        hi, lo = _split2(y_refs[e][...])
        acc = acc + _dot(onehot, hi) + _dot(onehot, lo)
    y = _rms(acc, g_ref[...])

    @pl.when(j < n_main)
    def _():
        yp_ref[...] = y

    @pl.when(j == n_main)
    def _():
        yt_ref[...] = y


def _moe_combine(h, y_sorted, eid, dest, win_start, g, n_main_rows, win):
    r, d = h.shape
    t = TAIL_ROWS
    n_main = n_main_rows // t
    window = lambda e: pl.BlockSpec((pl.Element(win), pl.Element(d)),
                                    lambda j, ws: (ws[j * N_EXPERTS + e] * SUBLANES, 0))
    grid_spec = pltpu.PrefetchScalarGridSpec(
        num_scalar_prefetch=1,
        grid=(r // t,),
        in_specs=[
            pl.BlockSpec((t, d), lambda j, ws: (j, 0)),
            pl.BlockSpec((t, LANES), lambda j, ws: (j, 0)),
            pl.BlockSpec((t, LANES), lambda j, ws: (j, 0)),
            pl.BlockSpec((1, d), lambda j, ws: (0, 0)),
        ] + [window(e) for e in range(N_EXPERTS)],
        out_specs=[pl.BlockSpec((t, d), lambda j, ws: (jnp.minimum(j, n_main - 1), 0)),
                   pl.BlockSpec((t, d), lambda j, ws: (0, 0))],
    )
    return pl.pallas_call(
        functools.partial(_moe_combine_kernel, n_main=n_main, win=win),
        out_shape=[jax.ShapeDtypeStruct((n_main_rows, d), F32), jax.ShapeDtypeStruct((t, d), F32)],
        grid_spec=grid_spec,
        compiler_params=_cparams(("arbitrary",)),
        name="moe_combine_norm",
    )(win_start, h, eid, dest, g.reshape(1, d), *([y_sorted] * N_EXPERTS))


def _moe(h, g_ffn, w_router, wg, wu, wd, g_final, n_main_rows, tm):
    r, d = h.shape
    eid_pad, gate_pad, a = _router(h, g_ffn, w_router, _pick_tile(r, 1024, LANES))
    eid, gate = eid_pad[:, :TOP_K], gate_pad[:, :TOP_K]
    n_slots = r * TOP_K
    big = jnp.int32(2 ** 30)
    e_flat = eid.reshape(n_slots)
    onehot = (e_flat[:, None] == jnp.arange(N_EXPERTS, dtype=jnp.int32)[None, :]).astype(jnp.int32)
    rank = jnp.take_along_axis(jnp.cumsum(onehot, axis=0), e_flat[:, None], axis=1)[:, 0] - 1
    counts = jnp.sum(onehot, axis=0)
    padded = (counts + tm - 1) // tm * tm
    pends = jnp.cumsum(padded)
    pstarts = pends - padded
    dest = (pstarts[e_flat] + rank).astype(jnp.int32)
    n_tiles = (n_slots + N_EXPERTS * (tm - 1)) // tm
    cap = n_tiles * tm
    extra = (MOE_GROUP - 1) * tm
    slot_tok = jnp.full((cap + extra,), -1, jnp.int32).at[dest].set(jnp.arange(n_slots, dtype=jnp.int32) // TOP_K)
    slot_gate = jnp.zeros((cap + extra,), F32).at[dest].set(gate.reshape(n_slots))
    tiles_e = padded // tm
    groups_e = (tiles_e + MOE_GROUP - 1) // MOE_GROUP
    gends = jnp.cumsum(groups_e)
    n_groups = n_tiles // MOE_GROUP + N_EXPERTS
    gi = jnp.arange(n_groups, dtype=jnp.int32)
    gi_used = jnp.minimum(gi, gends[-1] - 1)
    group_expert = jnp.minimum(
        jnp.sum((gends[None, :] <= gi_used[:, None]).astype(jnp.int32), axis=1), N_EXPERTS - 1)
    within = gi_used - (gends - groups_e)[group_expert]
    group_first = ((pstarts // tm)[group_expert] + within * MOE_GROUP).astype(jnp.int32)
    group_count = jnp.where(gi < gends[-1],
                            jnp.clip(tiles_e[group_expert] - within * MOE_GROUP, 0, MOE_GROUP), 0).astype(jnp.int32)
    sblk = tm
    tok_tiles = slot_tok.reshape(n_tiles + MOE_GROUP - 1, tm)
    tok_max = jnp.max(tok_tiles, axis=1)
    tok_min = jnp.min(jnp.where(tok_tiles >= 0, tok_tiles, big), axis=1)
    tile_first = jnp.where(tok_max >= 0, tok_min // sblk, 1).astype(jnp.int32)
    tile_last = jnp.where(tok_max >= 0, tok_max // sblk, 0).astype(jnp.int32)
    win = TAIL_ROWS + SUBLANES
    n_tok_tiles = r // TAIL_ROWS
    dest_e = jnp.where(onehot == 1, dest[:, None], big).reshape(n_tok_tiles, TAIL_ROWS * TOP_K, N_EXPERTS)
    first_row = jnp.min(dest_e, axis=1)
    win_start = jnp.clip(first_row // SUBLANES, 0, (cap - win) // SUBLANES).reshape(-1).astype(jnp.int32)

    x_sorted = _moe_permute(a, slot_tok, tile_first, tile_last, tm, sblk)
    y_sorted = _moe_ffn(x_sorted, slot_gate, wg, wu, wd, group_expert.astype(jnp.int32), group_first, group_count,
                        n_groups, tm, _pick_tile(wg.shape[2], 256, LANES))
    dest_pad = jnp.pad(dest.reshape(r, TOP_K), ((0, 0), (0, LANES - TOP_K)))
    return _moe_combine(h, y_sorted, eid_pad, dest_pad, win_start, g_final, n_main_rows, win)


def kernel(x_prompt, x_sample, cache_k0, cache_v0, state_conv0, cache_k1, cache_v1, state_hgrn1, page_table, meta_tokens, norm_mix, norm_ffn, norm_final, w_in0, lambda_q1, lambda_k1, lambda_q2, lambda_k2, subln_g0, conv_w0, conv_b0, conv_ln_g0, conv_ln_b0, w_out0, ffn_gate0, ffn_up0, ffn_down0, w_in1, hgrn_lb_logits, hgrn_norm_g1, w_out1, router1, moe_gate1, moe_up1, moe_down1):
    b, s, d = x_prompt.shape
    db, ds, _ = x_sample.shape
    n_pages = page_table.shape[1]
    page = cache_k0.shape[1]
    assert d == 2 * W_MIX and ds <= SAMPLE_ROWS and N_META + SAMPLE_ROWS * db <= TAIL_ROWS
    assert s % 256 == 0
    bs = b * s
    r = bs + TAIL_ROWS
    tm_proj = _pick_tile(r, 1664, 16)
    tm_out = _pick_tile(r, 1040, 16)
    tm_ffn = _pick_tile(r, 832, 16)
    tm_moe = _pick_tile(r, 768, LANES)
    tq_diff, tq_sb = (512 if s % 512 == 0 else 256), 256

    xs = jnp.pad(x_sample, ((0, 0), (0, SAMPLE_ROWS - ds), (0, 0))).reshape(db * SAMPLE_ROWS, d)
    tail = jnp.concatenate(
        [meta_tokens.astype(F32), xs, jnp.zeros((TAIL_ROWS - N_META - db * SAMPLE_ROWS, d), F32)], axis=0)
    x_flat = jnp.concatenate([x_prompt.reshape(bs, d), tail], axis=0)

    pos_tail = jnp.concatenate([
        jnp.arange(N_META, dtype=jnp.int32),
        jnp.tile(n_pages * page + jnp.arange(SAMPLE_ROWS, dtype=jnp.int32), db),
        jnp.zeros((TAIL_ROWS - N_META - db * SAMPLE_ROWS,), jnp.int32)])
    tables = _rope_tables(jnp.concatenate([N_META + jnp.arange(s, dtype=jnp.int32), pos_tail]))
    lamp = jnp.stack([lambda_q1, lambda_k1, lambda_q2, lambda_k2]).astype(F32)
    g_sub = subln_g0.reshape(1, HEAD_DIM)

    def sample_rows(x):
        return x[N_META:N_META + db * SAMPLE_ROWS].reshape(db, SAMPLE_ROWS, N_HEADS, HEAD_DIM)[:, :ds]

    proj0 = _norm_matmul(x_flat, norm_mix[0], w_in0, tm_proj, 512)
    k0p, v0p, k0_tail = _kv_layout(proj0, b, s, tables)
    extra0 = (tables[0], tables[1], lamp, g_sub)
    mix0 = _attn_prompt(proj0, k0p, v0p, "diff", 0, b, s, tq_diff, extra0)
    mix0 = _attn_tail(mix0, proj0, k0_tail, (0, 0), cache_k0, cache_v0, page_table, "diff", 0, b, s, ds, extra0)
    mix0, conv0_p = _conv_prompt(mix0, proj0, conv_w0, conv_b0, conv_ln_g0, conv_ln_b0, b, s, 128)
    mix0, conv0_s = _conv_tail(mix0, proj0, state_conv0, conv_w0, conv_b0, conv_ln_g0, conv_ln_b0, b, s, ds)
    h = _matmul_residual(mix0, w_out0, x_flat, tm_out, 512)
    h = _ffn(h, norm_ffn[0], ffn_gate0, ffn_up0, ffn_down0, tm_ffn, _pick_tile(ffn_gate0.shape[1], 256, LANES))

    proj1 = _norm_matmul(h, norm_mix[1], w_in1, tm_proj, 512)
    k1p, v1p = _kv_layout(proj1, b, s, None)
    tb = bs // TAIL_ROWS
    mix1 = _attn_prompt(proj1, k1p, v1p, "sb", 1, b, s, tq_sb, None)
    mix1 = _attn_tail(mix1, proj1, proj1, (tb, 1), cache_k1, cache_v1, page_table, "sb", 1, b, s, ds, None)
    mix1, st_meta, hgrn_s = _hgrn_tail(mix1, proj1, hgrn_lb_logits, hgrn_norm_g1, state_hgrn1, b, s, ds)
    mix1, hgrn_p = _hgrn_prompt(mix1, proj1, hgrn_lb_logits, hgrn_norm_g1, st_meta, b, s, 256)
    h = _matmul_residual(mix1, w_out1, h, tm_out, 512)
    y_main, y_tail = _moe(h, norm_ffn[1], router1, moe_gate1, moe_up1, moe_down1, norm_final, bs, tm_moe)

    y_prompt = y_main.reshape(b, s, d)
    y_sample = y_tail[N_META:N_META + db * SAMPLE_ROWS].reshape(db, SAMPLE_ROWS, d)[:, :ds]
    heads = lambda x: x.reshape(b, N_META + s, N_HEADS, HEAD_DIM)
    tail0 = proj0[bs:]
    tail1 = proj1[bs:]
    return (y_prompt, y_sample,
            heads(k0p), heads(v0p), conv0_p, heads(k1p), heads(v1p), hgrn_p,
            sample_rows(k0_tail), sample_rows(tail0[:, 2 * W_MIX:3 * W_MIX]), conv0_s,
            sample_rows(tail1[:, W_MIX:2 * W_MIX]), sample_rows(tail1[:, 2 * W_MIX:3 * W_MIX]), hgrn_s)
```

```python
import functools
import math

import jax
import jax.numpy as jnp
from jax import lax
from jax.experimental import pallas as pl
from jax.experimental.pallas import tpu as pltpu

F32 = jnp.float32
BF16 = jnp.bfloat16

N_META = 16
EPS = 1e-6
ROPE_THETA = 10000.0
N_HEADS = 8
HEAD_DIM = 128
W_MIX = N_HEADS * HEAD_DIM
DK_A = 64
CONV_W = 31
N_EXPERTS = 8
TOP_K = 2
HALO = 32

LANES = 128
SUBLANES = 8
TAIL_ROWS = 128
SAMPLE_ROWS = 8
V7X_VMEM_BYTES = 64 * 1024 * 1024
VMEM_LIMIT_BYTES = 56 * 1024 * 1024
PAGES_PER_STEP = 8


def _lambda_init(layer):
    return 0.8 - 0.6 * math.exp(-0.3 * layer)


def _cparams(semantics, vmem=VMEM_LIMIT_BYTES):
    return pltpu.CompilerParams(dimension_semantics=semantics, vmem_limit_bytes=vmem)


def _pick_tile(n, cap, mult):
    best = None
    for t in range(mult, min(n, cap) + 1, mult):
        if n % t == 0:
            best = t
    assert best is not None, (n, cap, mult)
    return best


def _dot(a, b):
    return jnp.dot(a, b, preferred_element_type=F32)


def _dot_nt(a, b):
    return lax.dot_general(a, b, (((1,), (1,)), ((), ())), preferred_element_type=F32)


def _sigmoid(x):
    return 1.0 / (1.0 + jnp.exp(-x))


def _log_sigmoid(x):
    return jnp.minimum(x, 0.0) - jnp.log(1.0 + jnp.exp(-jnp.abs(x)))


def _rms(x, g):
    return x * lax.rsqrt(jnp.mean(x * x, axis=-1, keepdims=True) + EPS) * g


def _split2(x):
    hi = x.astype(BF16)
    lo = (x - hi.astype(F32)).astype(BF16)
    return hi, lo


def _dot3(a, b):
    return _dot(a[0], b[0]) + _dot(a[1], b[0]) + _dot(a[0], b[1])


def _dot3_nt(a, b):
    return _dot_nt(a[0], b[0]) + _dot_nt(a[1], b[0]) + _dot_nt(a[0], b[1])


def _lhs_store(a_scr, a, precise):
    if precise:
        hi, lo = _split2(a)
        a_scr[0] = hi
        a_scr[1] = lo
    else:
        a_scr[0] = a.astype(BF16)


def _lhs_load(a_scr, precise):
    return (a_scr[0], a_scr[1]) if precise else a_scr[0]


def _mm(a, w, precise):
    if not precise:
        return _dot(a, w.astype(BF16))
    a_hi, a_lo = a
    w_hi, w_lo = _split2(w)
    return _dot(a_hi, w_hi) + _dot(a_lo, w_hi) + _dot(a_hi, w_lo)


def _proj_kernel(x_ref, g_ref, w_ref, *rest, precise):
    o_ref, a_scr = rest[-2:]

    @pl.when(pl.program_id(1) == 0)
    def _():
        _lhs_store(a_scr, _rms(x_ref[...], g_ref[...]), precise)

    o_ref[...] = _mm(_lhs_load(a_scr, precise), w_ref[...], precise)


def _outproj_kernel(x_ref, w_ref, r_ref, *rest, precise):
    o_ref, a_scr = rest[-2:]

    @pl.when(pl.program_id(1) == 0)
    def _():
        _lhs_store(a_scr, x_ref[...], precise)

    o_ref[...] = r_ref[...] + _mm(_lhs_load(a_scr, precise), w_ref[...], precise)


def _ffn_kernel(h_ref, g_ref, wg_ref, wu_ref, wd_ref, *rest, precise):
    o_ref, a_scr = rest[-2:]
    f = pl.program_id(1)

    @pl.when(f == 0)
    def _():
        h = h_ref[...]
        _lhs_store(a_scr, _rms(h, g_ref[...]), precise)
        o_ref[...] = h

    a = _lhs_load(a_scr, precise)
    gate = _mm(a, wg_ref[...], precise)
    up = _mm(a, wu_ref[...], precise)
    t = (gate * _sigmoid(gate)) * up
    o_ref[...] += _mm(_split2(t) if precise else t.astype(BF16), wd_ref[...], precise)


def _dense_call(kernel_fn, name, x, consts, weights, res, out_width, out_tile, tm, n_steps, prev, single_rows):
    r, d = x.shape
    precise = prev is not None
    if precise:
        tm, row0, n_row_tiles = TAIL_ROWS, (r - TAIL_ROWS) // TAIL_ROWS, 1
    else:
        row0, n_row_tiles = 0, r // tm
    if out_tile == out_width:
        out_map = lambda i, j: (row0 + i, 0)
    else:
        out_map = lambda i, j: (row0 + i, j)
    row_mode = dict(pipeline_mode=pl.Buffered(1)) if single_rows else {}
    in_specs = [pl.BlockSpec((tm, d), lambda i, j: (row0 + i, 0), **row_mode)]
    args = [x]
    for arr in consts:
        in_specs.append(pl.BlockSpec(arr.shape, lambda i, j: (0, 0)))
        args.append(arr)
    for arr, shape, imap in weights:
        in_specs.append(pl.BlockSpec(shape, lambda i, j, imap=imap: imap(j)))
        args.append(arr)
    if res is not None:
        in_specs.append(pl.BlockSpec((tm, out_tile), out_map))
        args.append(res)
    aliases = {}
    if precise:
        in_specs.append(pl.BlockSpec(memory_space=pl.ANY))
        args.append(prev)
        aliases = {len(args) - 1: 0}
    return pl.pallas_call(
        functools.partial(kernel_fn, precise=precise),
        out_shape=jax.ShapeDtypeStruct((r, out_width), F32),
        grid=(n_row_tiles, n_steps),
        in_specs=in_specs,
        out_specs=pl.BlockSpec((tm, out_tile), out_map, **(row_mode if out_tile == out_width else {})),
        scratch_shapes=[pltpu.VMEM((2 if precise else 1, tm, d), BF16)],
        input_output_aliases=aliases,
        compiler_params=_cparams(("parallel", "arbitrary")),
        name=name + ("_tail" if precise else ""),
    )(*args)


def _norm_matmul(x, g, w, tm, tn):
    d, n = w.shape
    run = lambda prev: _dense_call(_proj_kernel, "norm_matmul", x, [g.reshape(1, d)],
                                   [(w, (d, tn), lambda j: (0, j))], None, n, tn, tm, n // tn, prev, False)
    return run(run(None))


def _matmul_residual(x, w, res, tm, tn):
    d, n = w.shape
    run = lambda prev: _dense_call(_outproj_kernel, "matmul_residual", x, [],
                                   [(w, (d, tn), lambda j: (0, j))], res, n, tn, tm, n // tn, prev, False)
    return run(run(None))


def _ffn(h, g, wg, wu, wd, tm, tf):
    d, ff = wg.shape
    weights = [(wg, (d, tf), lambda j: (0, j)), (wu, (d, tf), lambda j: (0, j)), (wd, (tf, d), lambda j: (j, 0))]
    run = lambda prev: _dense_call(_ffn_kernel, "ffn_swiglu", h, [g.reshape(1, d)], weights, None,
                                   d, d, tm, ff // tf, prev, True)
    return run(run(None))


def _rope_tables(pos):
    half = DK_A // 2
    inv = ROPE_THETA ** (-jnp.arange(half, dtype=F32) * 2.0 / DK_A)
    ang = pos.astype(F32)[:, None] * inv[None, :]
    cos = jnp.cos(ang)
    sin = jnp.sin(ang)
    cos_t = jnp.concatenate([cos, cos, cos, cos], axis=1)
    sin_t = jnp.concatenate([-sin, sin, -sin, sin], axis=1)
    return cos_t, sin_t


def _rope(x, cos_t, sin_t):
    half = DK_A // 2
    lane = lax.broadcasted_iota(jnp.int32, x.shape, 1)
    first = (lane % DK_A) < half
    partner = jnp.where(first, pltpu.roll(x, LANES - half, axis=1), pltpu.roll(x, half, axis=1))
    return x * cos_t + partner * sin_t


def _kv_layout_kernel(*refs, rope, s):
    if rope:
        (k_ref, kt_ref, v_ref, vt_ref, cos_ref, sin_ref, cost_ref, sint_ref, kp_ref, vp_ref, ktr_ref) = refs
        k = _rope(k_ref[...], cos_ref[...], sin_ref[...])
        kt = _rope(kt_ref[...], cost_ref[...], sint_ref[...])
        ktr_ref[...] = kt
    else:
        (k_ref, kt_ref, v_ref, vt_ref, kp_ref, vp_ref) = refs
        k = k_ref[...]
        kt = kt_ref[...]
    kp_ref[0, 0:N_META, :] = kt[0:N_META]
    kp_ref[0, N_META:N_META + s, :] = k
    vp_ref[0, 0:N_META, :] = vt_ref[0:N_META, :]
    vp_ref[0, N_META:N_META + s, :] = v_ref[...]


def _kv_layout(proj, b, s, tables):
    tb = (b * s) // TAIL_ROWS
    kcol, vcol = W_MIX // HEAD_DIM, 2 * W_MIX // HEAD_DIM
    rope = tables is not None
    in_specs = [
        pl.BlockSpec((s, HEAD_DIM), lambda h, bb: (bb, kcol + h)),
        pl.BlockSpec((TAIL_ROWS, HEAD_DIM), lambda h, bb: (tb, kcol + h)),
        pl.BlockSpec((s, HEAD_DIM), lambda h, bb: (bb, vcol + h)),
        pl.BlockSpec((TAIL_ROWS, HEAD_DIM), lambda h, bb: (tb, vcol + h)),
    ]
    args = [proj, proj, proj, proj]
    out_shape = [jax.ShapeDtypeStruct((b, N_META + s, W_MIX), F32)] * 2
    out_specs = [pl.BlockSpec((1, N_META + s, HEAD_DIM), lambda h, bb: (bb, 0, h))] * 2
    if rope:
        cos_t, sin_t = tables
        in_specs += [
            pl.BlockSpec((s, LANES), lambda h, bb: (0, 0)),
            pl.BlockSpec((s, LANES), lambda h, bb: (0, 0)),
            pl.BlockSpec((TAIL_ROWS, LANES), lambda h, bb: (s // TAIL_ROWS, 0)),
            pl.BlockSpec((TAIL_ROWS, LANES), lambda h, bb: (s // TAIL_ROWS, 0)),
        ]
        args += [cos_t, sin_t, cos_t, sin_t]
        out_shape = out_shape + [jax.ShapeDtypeStruct((TAIL_ROWS, W_MIX), F32)]
        out_specs = out_specs + [pl.BlockSpec((TAIL_ROWS, HEAD_DIM), lambda h, bb: (0, h))]
    return pl.pallas_call(
        functools.partial(_kv_layout_kernel, rope=rope, s=s),
        out_shape=out_shape,
        grid=(N_HEADS, b),
        in_specs=in_specs,
        out_specs=out_specs,
        compiler_params=_cparams(("arbitrary", "arbitrary")),
        name="kv_layout_rope" if rope else "kv_layout",
    )(*args)


def _diff_lambda(lamp_ref, layer):
    lp = lamp_ref[...]
    d1 = jnp.sum(lp[0:1] * lp[1:2], axis=-1, keepdims=True)
    d2 = jnp.sum(lp[2:3] * lp[3:4], axis=-1, keepdims=True)
    return jnp.exp(d1) - jnp.exp(d2) + _lambda_init(layer)


def _diff_queries_f32(q):
    lane = lax.broadcasted_iota(jnp.int32, q.shape, 1)
    qs = q * (DK_A ** -0.5)
    q1 = jnp.where(lane < DK_A, qs, 0.0)
    q2 = jnp.where(lane >= DK_A, qs, 0.0)
    return jnp.concatenate([q1, q2], axis=0)


def _diff_queries(q):
    return _diff_queries_f32(q).astype(BF16)


def _diff_update(carry, qs, kc, vc, mask2):
    m, l, acc = carry
    sc = _dot_nt(qs, kc.astype(BF16))
    if mask2 is not None:
        sc = jnp.where(mask2, sc, -jnp.inf)
    m_new = jnp.maximum(m, jnp.max(sc, axis=-1, keepdims=True))
    alpha = jnp.exp(m - m_new)
    p = jnp.exp(sc - m_new)
    l = alpha * l + jnp.sum(p, axis=-1, keepdims=True)
    acc = alpha * acc + _dot(p.astype(BF16), vc.astype(BF16))
    return m_new, l, acc


def _diff_finish(carry, n, lam, g, layer):
    _, l, acc = carry
    o = acc[:n] / l[:n] - lam * (acc[n:] / l[n:])
    return _rms(o, g) * (1.0 - _lambda_init(layer))


def _upper_ones(n):
    j = lax.broadcasted_iota(jnp.int32, (n, n), 0)
    s = lax.broadcasted_iota(jnp.int32, (n, n), 1)
    return jnp.where(j > s, 1.0, 0.0).astype(BF16)


def _later_ones(n):
    s = lax.broadcasted_iota(jnp.int32, (n, n), 0)
    j = lax.broadcasted_iota(jnp.int32, (n, n), 1)
    return jnp.where(j > s, 1.0, 0.0).astype(BF16)


SB_SCAN = 256


def _sb_update(carry, qb, kc, vc, mask, u):
    c, acc = carry
    n = kc.shape[0]
    z = _dot_nt(qb, kc.astype(BF16)) * (HEAD_DIM ** -0.5)
    ls = _log_sigmoid(z)
    l1m = ls - z
    if mask is not None:
        l1m = jnp.where(mask, l1m, 0.0)
    blk = u.shape[0]
    parts = [None] * (n // blk)
    for i in reversed(range(n // blk)):
        sl = slice(blk * i, blk * (i + 1))
        hi, lo = _split2(l1m[:, sl])
        parts[i] = jnp.exp(ls[:, sl] + (_dot(hi, u) + _dot(lo, u)) + c)
        c = c + jnp.sum(l1m[:, sl], axis=-1, keepdims=True)
    a = parts[0] if len(parts) == 1 else jnp.concatenate(parts, axis=1)
    if mask is not None:
        a = jnp.where(mask, a, 0.0)
    acc = acc + _dot(a.astype(BF16), vc.astype(BF16))
    return c, acc


def _attn_prompt_kernel(*refs, mode, tq, layer):
    if mode == "diff":
        q_ref, k_ref, v_ref, cos_ref, sin_ref, lamp_ref, g_ref, o_ref = refs
    else:
        q_ref, k_ref, v_ref, o_ref = refs
    i = pl.program_id(2)
    rows = lax.broadcasted_iota(jnp.int32, (tq, tq), 0)
    cols = lax.broadcasted_iota(jnp.int32, (tq, tq), 1)

    def chunk(j):
        start = pl.multiple_of(N_META + j * tq, SUBLANES)
        return k_ref[0, pl.ds(start, tq), :], v_ref[0, pl.ds(start, tq), :]

    k_meta = k_ref[0, 0:N_META, :]
    v_meta = v_ref[0, 0:N_META, :]

    if mode == "diff":
        qs = _diff_queries(_rope(q_ref[...], cos_ref[...], sin_ref[...]))
        carry = (jnp.full((2 * tq, 1), -jnp.inf, F32), jnp.zeros((2 * tq, 1), F32),
                 jnp.zeros((2 * tq, HEAD_DIM), F32))
        carry = _diff_update(carry, qs, k_meta, v_meta, None)

        def body(j, c):
            kc, vc = chunk(j)
            return _diff_update(c, qs, kc, vc, None)

        carry = lax.fori_loop(0, i, body, carry)
        kc, vc = chunk(i)
        causal = cols <= rows
        carry = _diff_update(carry, qs, kc, vc, jnp.concatenate([causal, causal], axis=0))
        o_ref[...] = _diff_finish(carry, tq, _diff_lambda(lamp_ref, layer), g_ref[...], layer)
    else:
        qb = q_ref[...].astype(BF16)
        u = _upper_ones(min(tq, SB_SCAN))
        carry = (jnp.zeros((tq, 1), F32), jnp.zeros((tq, HEAD_DIM), F32))
        kc, vc = chunk(i)
        carry = _sb_update(carry, qb, kc, vc, cols < rows, u)

        def body(it, c):
            kc, vc = chunk(i - 1 - it)
            return _sb_update(c, qb, kc, vc, None, u)

        carry = lax.fori_loop(0, i, body, carry)
        carry = _sb_update(carry, qb, k_meta, v_meta, None, _upper_ones(N_META))
        o_ref[...] = carry[1]


def _attn_prompt(proj, kp, vp, mode, layer, b, s, tq, extra):
    r = proj.shape[0]
    nq = s // tq
    in_specs = [
        pl.BlockSpec((tq, HEAD_DIM), lambda bb, h, i: (bb * nq + i, h)),
        pl.BlockSpec((1, N_META + s, HEAD_DIM), lambda bb, h, i: (bb, 0, h)),
        pl.BlockSpec((1, N_META + s, HEAD_DIM), lambda bb, h, i: (bb, 0, h)),
    ]
    args = [proj, kp, vp]
    if mode == "diff":
        cos_t, sin_t, lamp, g = extra
        in_specs += [
            pl.BlockSpec((tq, LANES), lambda bb, h, i: (i, 0)),
            pl.BlockSpec((tq, LANES), lambda bb, h, i: (i, 0)),
            pl.BlockSpec(lamp.shape, lambda bb, h, i: (0, 0)),
            pl.BlockSpec((1, HEAD_DIM), lambda bb, h, i: (0, 0)),
        ]
        args += [cos_t, sin_t, lamp, g]
    return pl.pallas_call(
        functools.partial(_attn_prompt_kernel, mode=mode, tq=tq, layer=layer),
        out_shape=jax.ShapeDtypeStruct((r, 2 * W_MIX), F32),
        grid=(b, N_HEADS, nq),
        in_specs=in_specs,
        out_specs=pl.BlockSpec((tq, HEAD_DIM), lambda bb, h, i: (bb * nq + i, h)),
        compiler_params=_cparams(("parallel", "parallel", "arbitrary")),
        name="attn_prompt_" + mode,
    )(*args)


def _attn_tail_kernel(pt_ref, *refs, mode, layer, ds, n_steps):
    del pt_ref
    pps = PAGES_PER_STEP
    if mode == "diff":
        (q_ref, k_ref, v_ref, cos_ref, sin_ref, lamp_ref, g_ref) = refs[:7]
        refs = refs[7:]
    else:
        (q_ref, k_ref, v_ref) = refs[:3]
        refs = refs[3:]
    kc_refs = refs[:pps]
    vc_refs = refs[pps:2 * pps]
    _mix_in, o_ref, q_scr, m_scr, l_scr, acc_scr = refs[2 * pps:]
    b = pl.program_id(0)
    st = pl.program_id(1)
    row0 = pl.multiple_of(N_META + SAMPLE_ROWS * b, SUBLANES)

    def head(x, h):
        return x[:, HEAD_DIM * h:HEAD_DIM * (h + 1)]

    @pl.when((b == 0) & (st == 0))
    def _():
        o_ref[...] = jnp.zeros_like(o_ref)
        rows = lax.broadcasted_iota(jnp.int32, (N_META, N_META), 0)
        cols = lax.broadcasted_iota(jnp.int32, (N_META, N_META), 1)
        qm = q_ref[0:N_META, :]
        km = k_ref[0:N_META, :]
        vm = v_ref[0:N_META, :]
        for h in range(N_HEADS):
            if mode == "diff":
                qs = _diff_queries(_rope(head(qm, h), cos_ref[0:N_META, :], sin_ref[0:N_META, :]))
                causal = cols <= rows
                carry = (jnp.full((2 * N_META, 1), -jnp.inf, F32), jnp.zeros((2 * N_META, 1), F32),
                         jnp.zeros((2 * N_META, HEAD_DIM), F32))
                carry = _diff_update(carry, qs, head(km, h), head(vm, h),
                                     jnp.concatenate([causal, causal], axis=0))
                o = _diff_finish(carry, N_META, _diff_lambda(lamp_ref, layer), g_ref[...], layer)
            else:
                carry = (jnp.zeros((N_META, 1), F32), jnp.zeros((N_META, HEAD_DIM), F32))
                carry = _sb_update(carry, head(qm, h).astype(BF16), head(km, h), head(vm, h),
                                   cols < rows, _upper_ones(N_META))
                o = carry[1]
            o_ref[0:N_META, HEAD_DIM * h:HEAD_DIM * (h + 1)] = o

    def scores(kflat):
        k_hi, k_lo = _split2(kflat)
        qcat = q_scr[...]
        return _dot_nt(qcat, jnp.concatenate([k_hi, k_hi], axis=1)) + _dot_nt(qcat[:, :HEAD_DIM], k_lo)

    def weighted_values(p, vflat):
        p_hi, p_lo = _split2(p)
        v_hi, v_lo = _split2(vflat)
        r = _dot(jnp.concatenate([p_hi, p_lo], axis=0), jnp.concatenate([v_hi, v_lo], axis=1))
        return r[:LANES, :HEAD_DIM] + r[:LANES, HEAD_DIM:] + r[LANES:, :HEAD_DIM]

    def cumsum_and_total(l1m):
        hi, lo = _split2(l1m)
        ones = jnp.ones((LANES, LANES), BF16)
        r = _dot(jnp.concatenate([hi, lo], axis=0), jnp.concatenate([_upper_ones(LANES), ones], axis=1))
        return r[:LANES, :LANES] + r[LANES:, :LANES], r[:LANES, LANES:] + r[LANES:, LANES:]

    def flat(ref):
        return ref[0].reshape(LANES * N_HEADS, HEAD_DIM)

    col = lax.broadcasted_iota(jnp.int32, (LANES, LANES), 0)
    lane = lax.broadcasted_iota(jnp.int32, (LANES, LANES), 1)
    page_lane = lax.broadcasted_iota(jnp.int32, (LANES, LANES * N_HEADS), 1)
    page_col = lax.broadcasted_iota(jnp.int32, (LANES, LANES * N_HEADS), 0)
    own_head = (page_lane % N_HEADS) == (page_col // 16)

    @pl.when(st == 0)
    def _():
        q8 = q_ref[pl.ds(row0, SAMPLE_ROWS), :]
        blocks = []
        for h in range(N_HEADS):
            qh = head(q8, h)
            if mode == "diff":
                qh = _diff_queries_f32(_rope(qh, cos_ref[pl.ds(row0, SAMPLE_ROWS), :],
                                             sin_ref[pl.ds(row0, SAMPLE_ROWS), :]))
            else:
                qh = jnp.concatenate([qh, jnp.zeros_like(qh)], axis=0)
            blocks.append(qh)
        q_hi, q_lo = _split2(jnp.concatenate(blocks, axis=0))
        q_scr[...] = jnp.concatenate([q_hi, q_lo], axis=1)

        k8 = k_ref[pl.ds(row0, SAMPLE_ROWS), :]
        v8 = v_ref[pl.ds(row0, SAMPLE_ROWS), :]
        pad = jnp.zeros((LANES - SAMPLE_ROWS * N_HEADS, HEAD_DIM), F32)
        k_new = jnp.concatenate([head(k8, h) for h in range(N_HEADS)] + [pad], axis=0)
        v_new = jnp.concatenate([head(v8, h) for h in range(N_HEADS)] + [pad], axis=0)
        tok = lane % SAMPLE_ROWS
        qrow = col % SAMPLE_ROWS
        ok = (lane // SAMPLE_ROWS == col // 16) & (tok < ds)
        sc = scores(k_new)
        if mode == "diff":
            sc = jnp.where(ok & (tok <= qrow), sc, -jnp.inf)
            m = jnp.max(sc, axis=1, keepdims=True)
            p = jnp.exp(sc - m)
            m_scr[...] = jnp.broadcast_to(m, (LANES, LANES))
            l_scr[...] = jnp.broadcast_to(jnp.sum(p, axis=1, keepdims=True), (LANES, LANES))
        else:
            ok = ok & (tok < qrow)
            z = sc * (HEAD_DIM ** -0.5)
            ls = _log_sigmoid(z)
            between, total = cumsum_and_total(jnp.where(ok, ls - z, 0.0))
            p = jnp.where(ok, jnp.exp(ls + between), 0.0)
            m_scr[...] = total
            l_scr[...] = jnp.zeros_like(l_scr)
        acc_scr[...] = weighted_values(p, v_new)

    if mode == "diff":
        scs = [jnp.where(own_head, scores(flat(kc_refs[j])), -jnp.inf) for j in range(pps)]
        m_old = m_scr[:, 0:1]
        m_new = m_old
        for sc in scs:
            m_new = jnp.maximum(m_new, jnp.max(sc, axis=1, keepdims=True))
        alpha = jnp.exp(m_old - m_new)
        l = alpha * l_scr[:, 0:1]
        acc = alpha * acc_scr[...]
        for j in range(pps):
            p = jnp.exp(scs[j] - m_new)
            l = l + jnp.sum(p, axis=1, keepdims=True)
            acc = acc + weighted_values(p, flat(vc_refs[j]))
        m_scr[...] = jnp.broadcast_to(m_new, (LANES, LANES))
        l_scr[...] = jnp.broadcast_to(l, (LANES, LANES))
        acc_scr[...] = acc
    else:
        running = m_scr[...]
        acc = acc_scr[...]
        n_blk = N_HEADS
        for j in reversed(range(pps)):
            z = scores(flat(kc_refs[j])) * (HEAD_DIM ** -0.5)
            ls = _log_sigmoid(z)
            l1m = jnp.where(own_head, ls - z, 0.0)
            ps = [None] * n_blk
            for i in reversed(range(n_blk)):
                blk = slice(LANES * i, LANES * (i + 1))
                between, total = cumsum_and_total(l1m[:, blk])
                ps[i] = jnp.where(own_head[:, blk], jnp.exp(ls[:, blk] + between + running), 0.0)
                running = running + total
            acc = acc + weighted_values(jnp.concatenate(ps, axis=1), flat(vc_refs[j]))
        m_scr[...] = running
        acc_scr[...] = acc

    @pl.when(st == n_steps - 1)
    def _():
        acc = acc_scr[...]
        real = lax.broadcasted_iota(jnp.int32, (SAMPLE_ROWS, HEAD_DIM), 0) < ds
        outs = []
        if mode == "diff":
            acc = acc / l_scr[:, 0:1]
            lam = _diff_lambda(lamp_ref, layer)
            for h in range(N_HEADS):
                o = acc[16 * h:16 * h + 8] - lam * acc[16 * h + 8:16 * h + 16]
                o = _rms(o, g_ref[...]) * (1.0 - _lambda_init(layer))
                outs.append(jnp.where(real, o, 0.0))
        else:
            for h in range(N_HEADS):
                outs.append(jnp.where(real, acc[16 * h:16 * h + 8], 0.0))
        o_ref[pl.ds(row0, SAMPLE_ROWS), :] = jnp.concatenate(outs, axis=1)


def _attn_tail(mix, proj, k_tail, k_tail_col, cache_k, cache_v, page_table, mode, layer, b, s, ds, extra):
    r = proj.shape[0]
    tb = (b * s) // TAIL_ROWS
    db, n_pages = page_table.shape
    pps = PAGES_PER_STEP
    assert n_pages % pps == 0
    n_steps = n_pages // pps
    page = cache_k.shape[1]
    assert page == LANES and cache_k.shape[2:] == (N_HEADS, HEAD_DIM)
    page_block = (1, page, N_HEADS, HEAD_DIM)
    vcol = 2

    in_specs = [
        pl.BlockSpec((TAIL_ROWS, W_MIX), lambda bb, st, pt: (tb, 0)),
        pl.BlockSpec((TAIL_ROWS, W_MIX), lambda bb, st, pt: k_tail_col),
        pl.BlockSpec((TAIL_ROWS, W_MIX), lambda bb, st, pt: (tb, vcol)),
    ]
    args = [proj, k_tail, proj]
    if mode == "diff":
        cos_t, sin_t, lamp, g = extra
        in_specs += [
            pl.BlockSpec((TAIL_ROWS, LANES), lambda bb, st, pt: (s // TAIL_ROWS, 0)),
            pl.BlockSpec((TAIL_ROWS, LANES), lambda bb, st, pt: (s // TAIL_ROWS, 0)),
            pl.BlockSpec(lamp.shape, lambda bb, st, pt: (0, 0)),
            pl.BlockSpec((1, HEAD_DIM), lambda bb, st, pt: (0, 0)),
        ]
        args += [cos_t, sin_t, lamp, g]

    def page_spec(j):
        if mode == "diff":
            return pl.BlockSpec(page_block, lambda bb, st, pt: (pt[bb, st * pps + j], 0, 0, 0))
        return pl.BlockSpec(page_block, lambda bb, st, pt: (pt[bb, n_pages - (st + 1) * pps + j], 0, 0, 0))

    in_specs += [page_spec(j) for j in range(pps)] + [page_spec(j) for j in range(pps)]
    args += [cache_k] * pps + [cache_v] * pps
    in_specs.append(pl.BlockSpec(memory_space=pl.ANY))
    args.append(mix)
    n_in = len(args)

    grid_spec = pltpu.PrefetchScalarGridSpec(
        num_scalar_prefetch=1,
        grid=(db, n_steps),
        in_specs=in_specs,
        out_specs=pl.BlockSpec((TAIL_ROWS, W_MIX), lambda bb, st, pt: (tb, 0)),
        scratch_shapes=[
            pltpu.VMEM((LANES, 2 * HEAD_DIM), BF16),
            pltpu.VMEM((LANES, LANES), F32),
            pltpu.VMEM((LANES, LANES), F32),
            pltpu.VMEM((LANES, HEAD_DIM), F32),
        ],
    )
    return pl.pallas_call(
        functools.partial(_attn_tail_kernel, mode=mode, layer=layer, ds=ds, n_steps=n_steps),
        out_shape=jax.ShapeDtypeStruct((r, 2 * W_MIX), F32),
        grid_spec=grid_spec,
        input_output_aliases={n_in: 0},
        compiler_params=_cparams(("arbitrary", "arbitrary")),
        name="attn_tail_" + mode,
    )(page_table, *args)


def _glu(val, gate):
    return val * _sigmoid(gate)


def _dwconv_rows(buf_ref, w_ref, n_rows, row_strip):
    off = HALO - (CONV_W - 1)
    strips = []
    for r0 in range(0, n_rows, row_strip):
        cols = []
        for c0 in range(0, W_MIX, LANES):
            acc = jnp.zeros((row_strip, LANES), F32)
            for k in range(CONV_W):
                acc = acc + w_ref[k:k + 1, c0:c0 + LANES] * buf_ref[r0 + off + k:r0 + off + k + row_strip, c0:c0 + LANES]
            cols.append(acc)
        strips.append(jnp.concatenate(cols, axis=1))
    return strips[0] if len(strips) == 1 else jnp.concatenate(strips, axis=0)


def _ln_silu(y, g, b):
    yc = y - jnp.mean(y, axis=-1, keepdims=True)
    yn = yc * lax.rsqrt(jnp.mean(yc * yc, axis=-1, keepdims=True) + EPS) * g + b
    return yn * _sigmoid(yn)


def _conv_prompt_kernel(val_ref, gate_ref, tval_ref, tgate_ref, w_ref, b_ref, g_ref, beta_ref, _mix_in,
                        o_ref, cp_ref, buf_scr, *, t):
    i = pl.program_id(1)

    @pl.when(i == 0)
    def _():
        buf_scr[0:HALO - N_META, :] = jnp.zeros((HALO - N_META, W_MIX), F32)
        buf_scr[HALO - N_META:HALO, :] = _glu(tval_ref[0:N_META, :], tgate_ref[0:N_META, :])

    @pl.when(i > 0)
    def _():
        buf_scr[0:HALO, :] = buf_scr[t:t + HALO, :]

    buf_scr[HALO:HALO + t, :] = _glu(val_ref[...], gate_ref[...])
    y = _dwconv_rows(buf_scr, w_ref, t, 64) + b_ref[...]
    o_ref[...] = _ln_silu(y, g_ref[...], beta_ref[...])

    @pl.when(i == pl.num_programs(1) - 1)
    def _():
        cp_ref[0] = buf_scr[HALO + t - (CONV_W - 1):HALO + t, :]


def _conv_prompt(mix, proj, w, bias, g, beta, b, s, t):
    r = proj.shape[0]
    tb = (b * s) // TAIL_ROWS
    nb = s // t
    vcol, gcol = 3, 4
    row = lambda x: x.reshape(1, W_MIX)
    full = lambda bb, i: (0, 0)
    return pl.pallas_call(
        functools.partial(_conv_prompt_kernel, t=t),
        out_shape=[jax.ShapeDtypeStruct((r, 2 * W_MIX), F32),
                   jax.ShapeDtypeStruct((b, CONV_W - 1, W_MIX), F32)],
        grid=(b, nb),
        in_specs=[
            pl.BlockSpec((t, W_MIX), lambda bb, i: (bb * nb + i, vcol)),
            pl.BlockSpec((t, W_MIX), lambda bb, i: (bb * nb + i, gcol)),
            pl.BlockSpec((TAIL_ROWS, W_MIX), lambda bb, i: (tb, vcol)),
            pl.BlockSpec((TAIL_ROWS, W_MIX), lambda bb, i: (tb, gcol)),
            pl.BlockSpec((CONV_W, W_MIX), full),
            pl.BlockSpec((1, W_MIX), full),
            pl.BlockSpec((1, W_MIX), full),
            pl.BlockSpec((1, W_MIX), full),
            pl.BlockSpec(memory_space=pl.ANY),
        ],
        out_specs=[pl.BlockSpec((t, W_MIX), lambda bb, i: (bb * nb + i, 1)),
                   pl.BlockSpec((1, CONV_W - 1, W_MIX), lambda bb, i: (bb, 0, 0))],
        scratch_shapes=[pltpu.VMEM((HALO + t, W_MIX), F32)],
        input_output_aliases={8: 0},
        compiler_params=_cparams(("parallel", "arbitrary")),
        name="conv_prompt",
    )(proj, proj, proj, proj, w, row(bias), row(g), row(beta), mix)


def _conv_tail_kernel(tval_ref, tgate_ref, st_ref, w_ref, b_ref, g_ref, beta_ref, _mix_in,
                      o_ref, cs_ref, buf_scr, *, db, ds):
    u = _glu(tval_ref[...], tgate_ref[...])
    o_ref[...] = jnp.zeros_like(o_ref)
    keep = CONV_W - 1

    def finish(n_rows):
        y = _dwconv_rows(buf_scr, w_ref, n_rows, n_rows) + b_ref[...]
        return _ln_silu(y, g_ref[...], beta_ref[...])

    buf_scr[0:HALO, :] = jnp.zeros((HALO, W_MIX), F32)
    buf_scr[HALO:HALO + N_META, :] = u[0:N_META]
    o_ref[0:N_META, :] = finish(N_META)
    real = lax.broadcasted_iota(jnp.int32, (SAMPLE_ROWS, W_MIX), 0) < ds
    for b in range(db):
        r0 = N_META + SAMPLE_ROWS * b
        buf_scr[0:HALO, :] = st_ref[b]
        buf_scr[HALO:HALO + SAMPLE_ROWS, :] = u[r0:r0 + SAMPLE_ROWS]
        o_ref[r0:r0 + SAMPLE_ROWS, :] = jnp.where(real, finish(SAMPLE_ROWS), 0.0)
        cs_ref[b] = buf_scr[HALO + ds - keep:HALO + ds, :]


def _conv_tail(mix, proj, state, w, bias, g, beta, b, s, ds):
    r = proj.shape[0]
    tb = (b * s) // TAIL_ROWS
    db = state.shape[0]
    keep = CONV_W - 1
    vcol, gcol = 3, 4
    state_pad = jnp.pad(state, ((0, 0), (HALO - keep, 0), (0, 0)))
    row = lambda x: x.reshape(1, W_MIX)
    full2 = lambda i: (0, 0)
    full3 = lambda i: (0, 0, 0)
    return pl.pallas_call(
        functools.partial(_conv_tail_kernel, db=db, ds=ds),
        out_shape=[jax.ShapeDtypeStruct((r, 2 * W_MIX), F32),
                   jax.ShapeDtypeStruct((db, keep, W_MIX), F32)],
        grid=(1,),
        in_specs=[
            pl.BlockSpec((TAIL_ROWS, W_MIX), lambda i: (tb, vcol)),
            pl.BlockSpec((TAIL_ROWS, W_MIX), lambda i: (tb, gcol)),
            pl.BlockSpec((db, HALO, W_MIX), full3),
            pl.BlockSpec((CONV_W, W_MIX), full2),
            pl.BlockSpec((1, W_MIX), full2),
            pl.BlockSpec((1, W_MIX), full2),
            pl.BlockSpec((1, W_MIX), full2),
            pl.BlockSpec(memory_space=pl.ANY),
        ],
        out_specs=[pl.BlockSpec((TAIL_ROWS, W_MIX), lambda i: (tb, 1)),
                   pl.BlockSpec((db, keep, W_MIX), full3)],
        scratch_shapes=[pltpu.VMEM((HALO + N_META, W_MIX), F32)],
        input_output_aliases={7: 0},
        compiler_params=_cparams(("arbitrary",)),
        name="conv_tail",
    )(proj, proj, state_pad, w, row(bias), row(g), row(beta), mix)


def _group_ref(cb, c, lvl):
    grp, half = 2 << lvl, 1 << lvl
    if grp >= 2 * SUBLANES:
        x3 = cb.reshape(c // grp, grp, LANES)
        ref = x3[:, half - 1:half, :]
        return jnp.broadcast_to(ref, (c // grp, grp, LANES)).reshape(c, LANES)
    x3 = cb.reshape(c // SUBLANES, SUBLANES, LANES)
    sub = lax.broadcasted_iota(jnp.int32, x3.shape, 1)
    out = None
    for g0 in range(0, SUBLANES, grp):
        ref = jnp.broadcast_to(x3[:, g0 + half - 1:g0 + half, :], x3.shape)
        out = ref if out is None else jnp.where(sub >= g0, ref, out)
    return out.reshape(c, LANES)


def _pad_rows(a):
    c = a.shape[0]
    if c % LANES == 0:
        return a
    return jnp.concatenate([a, jnp.zeros((LANES - c % LANES, a.shape[1]), a.dtype)], axis=0)


def _hgrn_chunk(hq, hf, hi, lb, st, c, valid, precise=False):
    if precise:
        mm_nt = lambda x, y: _dot3_nt(_split2(x), _split2(y))
        mm = lambda x, y: _dot3(_split2(x), _split2(y))
    else:
        mm_nt = lambda x, y: _dot_nt(x.astype(BF16), y.astype(BF16))
        mm = lambda x, y: _dot(x.astype(BF16), y.astype(BF16))
    f = lb + (1.0 - lb) * _sigmoid(hf)
    logf = jnp.log(f)
    k = 1.0 - f
    if valid is not None:
        logf = jnp.where(valid, logf, 0.0)
        k = jnp.where(valid, k, 0.0)
    row = lax.broadcasted_iota(jnp.int32, (c, LANES), 0)
    cb = logf
    sh = 1
    while sh < c:
        cb = cb + jnp.where(row >= sh, pltpu.roll(cb, sh, axis=0), 0.0)
        sh *= 2
    rows = lax.broadcasted_iota(jnp.int32, (c, c), 0)
    cols = lax.broadcasted_iota(jnp.int32, (c, c), 1)
    a = jnp.where(rows == cols, mm_nt(hq, k), 0.0)
    lvl = 0
    while (1 << lvl) < c:
        ref = _group_ref(cb, c, lvl)
        right = ((row >> lvl) & 1) == 1
        qe = jnp.where(right, hq * jnp.exp(jnp.minimum(cb - ref, 0.0)), 0.0)
        ke = jnp.where(right, 0.0, k * jnp.exp(jnp.minimum(ref - cb, 0.0)))
        a = a + jnp.where((rows >> (lvl + 1)) == (cols >> (lvl + 1)), mm_nt(qe, ke), 0.0)
        lvl += 1
    o = mm(a, hi) + mm_nt(hq * jnp.exp(cb), st)
    last = cb[c - 1:c, :]
    st_new = st * jnp.exp(last) + mm(jnp.transpose(_pad_rows(hi)), _pad_rows(k * jnp.exp(last - cb)))
    return o, st_new


def _hgrn_lb(lbl_ref):
    lg = lbl_ref[...]
    m = jnp.max(lg, axis=0, keepdims=True)
    e = jnp.exp(lg - m)
    sm = e / jnp.sum(e, axis=0, keepdims=True)
    return (sm[0:1] + sm[1:2]) - sm[0:1]


def _hgrn_out(o, g, hg):
    return _rms(o, g) * (hg * _sigmoid(hg))


def _hgrn_prompt_kernel(hq_ref, hf_ref, hi_ref, hg_ref, lbl_ref, g_ref, st0_ref, _mix_in, o_ref, s_ref, st_scr, *, c):
    ci = pl.program_id(2)

    @pl.when(ci == 0)
    def _():
        st_scr[...] = st0_ref[0]

    o, st = _hgrn_chunk(hq_ref[...], hf_ref[...], hi_ref[...], _hgrn_lb(lbl_ref), st_scr[...], c, None)
    st_scr[...] = st
    o_ref[...] = _hgrn_out(o, g_ref[...], hg_ref[...])

    @pl.when(ci == pl.num_programs(2) - 1)
    def _():
        s_ref[0, 0] = jnp.transpose(st)


def _hgrn_cols():
    base = 3 * W_MIX // HEAD_DIM
    return base, base + N_HEADS, base + 2 * N_HEADS, base + 3 * N_HEADS


def _hgrn_prompt(mix, proj, lb_logits, g, st_meta, b, s, c):
    r = proj.shape[0]
    nc = s // c
    qc, fc, ic, gc = _hgrn_cols()
    blk = lambda col: pl.BlockSpec((c, HEAD_DIM), lambda bb, h, ci: (bb * nc + ci, col + h))
    return pl.pallas_call(
        functools.partial(_hgrn_prompt_kernel, c=c),
        out_shape=[jax.ShapeDtypeStruct((r, 2 * W_MIX), F32),
                   jax.ShapeDtypeStruct((b, N_HEADS, HEAD_DIM, HEAD_DIM), F32)],
        grid=(b, N_HEADS, nc),
        in_specs=[
            blk(qc), blk(fc), blk(ic), blk(gc),
            pl.BlockSpec((2, HEAD_DIM), lambda bb, h, ci: (0, h)),
            pl.BlockSpec((1, HEAD_DIM), lambda bb, h, ci: (0, 0)),
            pl.BlockSpec((1, HEAD_DIM, HEAD_DIM), lambda bb, h, ci: (h, 0, 0)),
            pl.BlockSpec(memory_space=pl.ANY),
        ],
        out_specs=[pl.BlockSpec((c, HEAD_DIM), lambda bb, h, ci: (bb * nc + ci, N_HEADS + h)),
                   pl.BlockSpec((1, 1, HEAD_DIM, HEAD_DIM), lambda bb, h, ci: (bb, h, 0, 0))],
        scratch_shapes=[pltpu.VMEM((HEAD_DIM, HEAD_DIM), F32)],
        input_output_aliases={7: 0},
        compiler_params=_cparams(("parallel", "parallel", "arbitrary")),
        name="hgrn_prompt",
    )(proj, proj, proj, proj, lb_logits, g.reshape(1, HEAD_DIM), st_meta, mix)


def _hgrn_tail_kernel(hq_ref, hf_ref, hi_ref, hg_ref, lbl_ref, g_ref, s0_ref, _mix_in,
                      o_ref, stm_ref, sn_ref, st_scr, *, db, ds):
    j = pl.program_id(1)
    c = 2 * SAMPLE_ROWS
    row = lax.broadcasted_iota(jnp.int32, (c, HEAD_DIM), 0)
    lb = _hgrn_lb(lbl_ref)

    @pl.when(j == 0)
    def _():
        o, st = _hgrn_chunk(hq_ref[...], hf_ref[...], hi_ref[...], lb, jnp.zeros((HEAD_DIM, HEAD_DIM), F32), c, None)
        stm_ref[0] = st
        o_ref[...] = _hgrn_out(o, g_ref[...], hg_ref[...])

    @pl.when((j > 0) & (j <= db // 2))
    def _():
        outs = []
        for half in range(2):
            valid = (row >= half * SAMPLE_ROWS) & (row < half * SAMPLE_ROWS + ds)
            st0 = jnp.transpose(s0_ref[half, 0])
            o, st = _hgrn_chunk(hq_ref[...], hf_ref[...], hi_ref[...], lb, st0, c, valid, precise=True)
            sn_ref[half, 0] = jnp.transpose(st)
            outs.append(jnp.where(valid, o, 0.0))
        o_ref[...] = _hgrn_out(outs[0] + outs[1], g_ref[...], hg_ref[...])

    @pl.when(j > db // 2)
    def _():
        o_ref[...] = jnp.zeros_like(o_ref)


def _hgrn_tail(mix, proj, lb_logits, g, state, b, s, ds):
    r = proj.shape[0]
    db = state.shape[0]
    assert db % 2 == 0
    c = 2 * SAMPLE_ROWS
    row_blk0 = (b * s) // c
    n_chunks = TAIL_ROWS // c
    qc, fc, ic, gc = _hgrn_cols()
    blk = lambda col: pl.BlockSpec((c, HEAD_DIM), lambda h, j: (row_blk0 + j, col + h))
    pair = lambda h, j: (jnp.clip(j - 1, 0, db // 2 - 1), h, 0, 0)
    return pl.pallas_call(
        functools.partial(_hgrn_tail_kernel, db=db, ds=ds),
        out_shape=[jax.ShapeDtypeStruct((r, 2 * W_MIX), F32),
                   jax.ShapeDtypeStruct((N_HEADS, HEAD_DIM, HEAD_DIM), F32),
                   jax.ShapeDtypeStruct((db, N_HEADS, HEAD_DIM, HEAD_DIM), F32)],
        grid=(N_HEADS, n_chunks),
        in_specs=[
            blk(qc), blk(fc), blk(ic), blk(gc),
            pl.BlockSpec((2, HEAD_DIM), lambda h, j: (0, h)),
            pl.BlockSpec((1, HEAD_DIM), lambda h, j: (0, 0)),
            pl.BlockSpec((2, 1, HEAD_DIM, HEAD_DIM), pair),
            pl.BlockSpec(memory_space=pl.ANY),
        ],
        out_specs=[pl.BlockSpec((c, HEAD_DIM), lambda h, j: (row_blk0 + j, N_HEADS + h)),
                   pl.BlockSpec((1, HEAD_DIM, HEAD_DIM), lambda h, j: (h, 0, 0)),
                   pl.BlockSpec((2, 1, HEAD_DIM, HEAD_DIM), pair)],
        scratch_shapes=[pltpu.VMEM((HEAD_DIM, HEAD_DIM), F32)],
        input_output_aliases={7: 0},
        compiler_params=_cparams(("arbitrary", "arbitrary")),
        name="hgrn_tail",
    )(proj, proj, proj, proj, lb_logits, g.reshape(1, HEAD_DIM), state, mix)


def _router_kernel(h_ref, g_ref, w_ref, eid_ref, gate_ref, a_ref):
    a = _rms(h_ref[...], g_ref[...])
    a_ref[...] = a.astype(BF16)
    logits = jnp.dot(a, w_ref[...], preferred_element_type=F32, precision=lax.Precision.HIGHEST)
    lane = lax.broadcasted_iota(jnp.int32, logits.shape, 1)
    logits = jnp.where(lane < N_EXPERTS, logits, -jnp.inf)
    m1 = jnp.max(logits, axis=-1, keepdims=True)
    i1 = jnp.min(jnp.where(logits == m1, lane, LANES), axis=-1, keepdims=True)
    rest = jnp.where(lane == i1, -jnp.inf, logits)
    m2 = jnp.max(rest, axis=-1, keepdims=True)
    i2 = jnp.min(jnp.where(rest == m2, lane, LANES), axis=-1, keepdims=True)
    e2 = jnp.exp(m2 - m1)
    g1 = 1.0 / (1.0 + e2)
    g2 = e2 / (1.0 + e2)
    eid_ref[...] = jnp.where(lane == 0, i1, jnp.where(lane == 1, i2, 0))
    gate_ref[...] = jnp.where(lane == 0, g1, jnp.where(lane == 1, g2, 0.0))


def _router(h, g, w_router, tm):
    r, d = h.shape
    w_pad = jnp.pad(w_router, ((0, 0), (0, LANES - N_EXPERTS)))
    eid, gate, a = pl.pallas_call(
        _router_kernel,
        out_shape=[jax.ShapeDtypeStruct((r, LANES), jnp.int32), jax.ShapeDtypeStruct((r, LANES), F32),
                   jax.ShapeDtypeStruct((r, d), BF16)],
        grid=(r // tm,),
        in_specs=[
            pl.BlockSpec((tm, d), lambda i: (i, 0)),
            pl.BlockSpec((1, d), lambda i: (0, 0)),
            pl.BlockSpec((d, LANES), lambda i: (0, 0)),
        ],
        out_specs=[pl.BlockSpec((tm, LANES), lambda i: (i, 0)), pl.BlockSpec((tm, LANES), lambda i: (i, 0)),
                   pl.BlockSpec((tm, d), lambda i: (i, 0))],
        compiler_params=_cparams(("parallel",)),
        name="moe_router",
    )(h, g.reshape(1, d), w_pad)
    return eid, gate, a


def _moe_permute_kernel(first_ref, last_ref, tok_ref, a_ref, o_ref, *, sblk):
    i = pl.program_id(0)
    k = pl.program_id(1)
    blk = first_ref[i] + k

    @pl.when(k == 0)
    def _():
        o_ref[...] = jnp.zeros_like(o_ref)

    @pl.when(blk <= last_ref[i])
    def _():
        col = lax.broadcasted_iota(jnp.int32, (o_ref.shape[0], sblk), 1)
        onehot = jnp.where(tok_ref[...] - blk * sblk == col, 1.0, 0.0).astype(BF16)
        o_ref[...] += _dot(onehot, a_ref[...]).astype(BF16)


def _moe_permute(a, slot_tok, tile_first, tile_last, tm, sblk):
    r, d = a.shape
    cap = slot_tok.shape[0]
    grid_spec = pltpu.PrefetchScalarGridSpec(
        num_scalar_prefetch=2,
        grid=(cap // tm, r // sblk),
        in_specs=[
            pl.BlockSpec((tm, 1), lambda i, k, tf, tl: (i, 0)),
            pl.BlockSpec((sblk, d), lambda i, k, tf, tl: (jnp.minimum(tf[i] + k, jnp.maximum(tl[i], 0)), 0)),
        ],
        out_specs=pl.BlockSpec((tm, d), lambda i, k, tf, tl: (i, 0)),
    )
    return pl.pallas_call(
        functools.partial(_moe_permute_kernel, sblk=sblk),
        out_shape=jax.ShapeDtypeStruct((cap, d), BF16),
        grid_spec=grid_spec,
        compiler_params=_cparams(("arbitrary", "arbitrary")),
        name="moe_permute",
    )(tile_first, tile_last, slot_tok.reshape(cap, 1), a)


MOE_GROUP = 2


def _moe_ffn_kernel(ge_ref, g0_ref, gn_ref, x_ref, sg_ref, wg_ref, wu_ref, wd_ref, _y_in, y_ref, acc_scr, sem, *, tm):
    i = pl.program_id(0)
    f = pl.program_id(1)
    del ge_ref
    n_valid = gn_ref[i]
    wg = wg_ref[0].astype(BF16)
    wu = wu_ref[0].astype(BF16)
    wd = wd_ref[0].astype(BF16)

    def out_copy(g):
        return pltpu.make_async_copy(acc_scr.at[g], y_ref.at[pl.ds((g0_ref[i] + g) * tm, tm)], sem.at[g])

    for g in range(MOE_GROUP):
        @pl.when(g < n_valid)
        def _():
            a = x_ref[g * tm:(g + 1) * tm, :]
            gate = _dot(a, wg)
            up = _dot(a, wu)
            t = (gate * _sigmoid(gate)) * up

            @pl.when(f == 0)
            def _():
                acc_scr[g] = jnp.zeros((tm, acc_scr.shape[2]), F32)

            acc_scr[g] += _dot(t.astype(BF16), wd)

            @pl.when(f == pl.num_programs(1) - 1)
            def _():
                acc_scr[g] = acc_scr[g] * sg_ref[g * tm:(g + 1) * tm, :]
                out_copy(g).start()

    @pl.when(f == pl.num_programs(1) - 1)
    def _():
        for g in range(MOE_GROUP):
            @pl.when(g < n_valid)
            def _():
                out_copy(g).wait()


def _moe_ffn(x_sorted, slot_gate, wg, wu, wd, group_expert, group_first, group_count, n_groups, tm, tf):
    rows, d = x_sorted.shape
    cap = rows - (MOE_GROUP - 1) * tm
    ff = wg.shape[2]
    nf = ff // tf
    gm = MOE_GROUP * tm
    fidx = lambda i, f, gn: jnp.where(gn[i] > 0, f, nf - 1)
    grid_spec = pltpu.PrefetchScalarGridSpec(
        num_scalar_prefetch=3,
        grid=(n_groups, nf),
        in_specs=[
            pl.BlockSpec((pl.Element(gm), pl.Element(d)), lambda i, f, ge, g0, gn: (g0[i] * tm, 0),
                         pipeline_mode=pl.Buffered(1)),
            pl.BlockSpec((pl.Element(gm), pl.Element(1)), lambda i, f, ge, g0, gn: (g0[i] * tm, 0),
                         pipeline_mode=pl.Buffered(1)),
            pl.BlockSpec((1, d, tf), lambda i, f, ge, g0, gn: (ge[i], 0, fidx(i, f, gn))),
            pl.BlockSpec((1, d, tf), lambda i, f, ge, g0, gn: (ge[i], 0, fidx(i, f, gn))),
            pl.BlockSpec((1, tf, d), lambda i, f, ge, g0, gn: (ge[i], fidx(i, f, gn), 0)),
            pl.BlockSpec(memory_space=pl.ANY),
        ],
        out_specs=pl.BlockSpec(memory_space=pl.ANY),
        scratch_shapes=[pltpu.VMEM((MOE_GROUP, tm, d), F32), pltpu.SemaphoreType.DMA((MOE_GROUP,))],
    )
    return pl.pallas_call(
        functools.partial(_moe_ffn_kernel, tm=tm),
        out_shape=jax.ShapeDtypeStruct((cap, d), F32),
        grid_spec=grid_spec,
        input_output_aliases={8: 0},
        compiler_params=_cparams(("arbitrary", "arbitrary")),
        name="moe_ffn",
    )(group_expert, group_first, group_count, x_sorted, slot_gate.reshape(rows, 1), wg, wu, wd,
      jnp.zeros((cap, d), F32))


def _moe_combine_kernel(ws_ref, h_ref, eid_ref, dest_ref, g_ref, *refs, n_main, win):
    y_refs = refs[:N_EXPERTS]
    yp_ref, yt_ref = refs[N_EXPERTS:]
    j = pl.program_id(0)
    eid = eid_ref[...]
    dest = dest_ref[...]
    col = lax.broadcasted_iota(jnp.int32, (TAIL_ROWS, win), 1)
    acc = h_ref[...]
    for e in range(N_EXPERTS):
        start = ws_ref[j * N_EXPERTS + e] * SUBLANES
        onehot = jnp.zeros((TAIL_ROWS, win), BF16)
        for k in range(TOP_K):
            rel = jnp.where(eid[:, k:k + 1] == e, dest[:, k:k + 1] - start, -1)
            onehot = onehot + jnp.where(rel == col, 1.0, 0.0).astype(BF16)
        hi, lo = _split2(y_refs[e][...])
        acc = acc + _dot(onehot, hi) + _dot(onehot, lo)
    y = _rms(acc, g_ref[...])

    @pl.when(j < n_main)
    def _():
        yp_ref[...] = y

    @pl.when(j == n_main)
    def _():
        yt_ref[...] = y


def _moe_combine(h, y_sorted, eid, dest, win_start, g, n_main_rows, win):
    r, d = h.shape
    t = TAIL_ROWS
    n_main = n_main_rows // t
    window = lambda e: pl.BlockSpec((pl.Element(win), pl.Element(d)),
                                    lambda j, ws: (ws[j * N_EXPERTS + e] * SUBLANES, 0))
    grid_spec = pltpu.PrefetchScalarGridSpec(
        num_scalar_prefetch=1,
        grid=(r // t,),
        in_specs=[
            pl.BlockSpec((t, d), lambda j, ws: (j, 0)),
            pl.BlockSpec((t, LANES), lambda j, ws: (j, 0)),
            pl.BlockSpec((t, LANES), lambda j, ws: (j, 0)),
            pl.BlockSpec((1, d), lambda j, ws: (0, 0)),
        ] + [window(e) for e in range(N_EXPERTS)],
        out_specs=[pl.BlockSpec((t, d), lambda j, ws: (jnp.minimum(j, n_main - 1), 0)),
                   pl.BlockSpec((t, d), lambda j, ws: (0, 0))],
    )
    return pl.pallas_call(
        functools.partial(_moe_combine_kernel, n_main=n_main, win=win),
        out_shape=[jax.ShapeDtypeStruct((n_main_rows, d), F32), jax.ShapeDtypeStruct((t, d), F32)],
        grid_spec=grid_spec,
        compiler_params=_cparams(("arbitrary",)),
        name="moe_combine_norm",
    )(win_start, h, eid, dest, g.reshape(1, d), *([y_sorted] * N_EXPERTS))


def _moe(h, g_ffn, w_router, wg, wu, wd, g_final, n_main_rows, tm):
    r, d = h.shape
    eid_pad, gate_pad, a = _router(h, g_ffn, w_router, _pick_tile(r, 1024, LANES))
    eid, gate = eid_pad[:, :TOP_K], gate_pad[:, :TOP_K]
    n_slots = r * TOP_K
    big = jnp.int32(2 ** 30)
    e_flat = eid.reshape(n_slots)
    onehot = (e_flat[:, None] == jnp.arange(N_EXPERTS, dtype=jnp.int32)[None, :]).astype(jnp.int32)
    rank = jnp.take_along_axis(jnp.cumsum(onehot, axis=0), e_flat[:, None], axis=1)[:, 0] - 1
    counts = jnp.sum(onehot, axis=0)
    padded = (counts + tm - 1) // tm * tm
    pends = jnp.cumsum(padded)
    pstarts = pends - padded
    dest = (pstarts[e_flat] + rank).astype(jnp.int32)
    n_tiles = (n_slots + N_EXPERTS * (tm - 1)) // tm
    cap = n_tiles * tm
    extra = (MOE_GROUP - 1) * tm
    slot_tok = jnp.full((cap + extra,), -1, jnp.int32).at[dest].set(jnp.arange(n_slots, dtype=jnp.int32) // TOP_K)
    slot_gate = jnp.zeros((cap + extra,), F32).at[dest].set(gate.reshape(n_slots))
    tiles_e = padded // tm
    groups_e = (tiles_e + MOE_GROUP - 1) // MOE_GROUP
    gends = jnp.cumsum(groups_e)
    n_groups = n_tiles // MOE_GROUP + N_EXPERTS
    gi = jnp.arange(n_groups, dtype=jnp.int32)
    gi_used = jnp.minimum(gi, gends[-1] - 1)
    group_expert = jnp.minimum(
        jnp.sum((gends[None, :] <= gi_used[:, None]).astype(jnp.int32), axis=1), N_EXPERTS - 1)
    within = gi_used - (gends - groups_e)[group_expert]
    group_first = ((pstarts // tm)[group_expert] + within * MOE_GROUP).astype(jnp.int32)
    group_count = jnp.where(gi < gends[-1],
                            jnp.clip(tiles_e[group_expert] - within * MOE_GROUP, 0, MOE_GROUP), 0).astype(jnp.int32)
    sblk = tm
    tok_tiles = slot_tok.reshape(n_tiles + MOE_GROUP - 1, tm)
    tok_max = jnp.max(tok_tiles, axis=1)
    tok_min = jnp.min(jnp.where(tok_tiles >= 0, tok_tiles, big), axis=1)
    tile_first = jnp.where(tok_max >= 0, tok_min // sblk, 1).astype(jnp.int32)
    tile_last = jnp.where(tok_max >= 0, tok_max // sblk, 0).astype(jnp.int32)
    win = TAIL_ROWS + SUBLANES
    n_tok_tiles = r // TAIL_ROWS
    dest_e = jnp.where(onehot == 1, dest[:, None], big).reshape(n_tok_tiles, TAIL_ROWS * TOP_K, N_EXPERTS)
    first_row = jnp.min(dest_e, axis=1)
    win_start = jnp.clip(first_row // SUBLANES, 0, (cap - win) // SUBLANES).reshape(-1).astype(jnp.int32)

    x_sorted = _moe_permute(a, slot_tok, tile_first, tile_last, tm, sblk)
    y_sorted = _moe_ffn(x_sorted, slot_gate, wg, wu, wd, group_expert.astype(jnp.int32), group_first, group_count,
                        n_groups, tm, _pick_tile(wg.shape[2], 512, LANES))
    dest_pad = jnp.pad(dest.reshape(r, TOP_K), ((0, 0), (0, LANES - TOP_K)))
    return _moe_combine(h, y_sorted, eid_pad, dest_pad, win_start, g_final, n_main_rows, win)


def kernel(x_prompt, x_sample, cache_k0, cache_v0, state_conv0, cache_k1, cache_v1, state_hgrn1, page_table, meta_tokens, norm_mix, norm_ffn, norm_final, w_in0, lambda_q1, lambda_k1, lambda_q2, lambda_k2, subln_g0, conv_w0, conv_b0, conv_ln_g0, conv_ln_b0, w_out0, ffn_gate0, ffn_up0, ffn_down0, w_in1, hgrn_lb_logits, hgrn_norm_g1, w_out1, router1, moe_gate1, moe_up1, moe_down1):
    b, s, d = x_prompt.shape
    db, ds, _ = x_sample.shape
    n_pages = page_table.shape[1]
    page = cache_k0.shape[1]
    assert d == 2 * W_MIX and ds <= SAMPLE_ROWS and N_META + SAMPLE_ROWS * db <= TAIL_ROWS
    assert s % 256 == 0
    bs = b * s
    r = bs + TAIL_ROWS
    tm_proj = _pick_tile(r, 1664, 16)
    tm_out = _pick_tile(r, 1040, 16)
    tm_ffn = _pick_tile(r, 832, 16)
    tm_moe = _pick_tile(r, 768, LANES)
    tq = 512 if s % 512 == 0 else 256

    xs = jnp.pad(x_sample, ((0, 0), (0, SAMPLE_ROWS - ds), (0, 0))).reshape(db * SAMPLE_ROWS, d)
    tail = jnp.concatenate(
        [meta_tokens.astype(F32), xs, jnp.zeros((TAIL_ROWS - N_META - db * SAMPLE_ROWS, d), F32)], axis=0)
    x_flat = jnp.concatenate([x_prompt.reshape(bs, d), tail], axis=0)

    pos_tail = jnp.concatenate([
        jnp.arange(N_META, dtype=jnp.int32),
        jnp.tile(n_pages * page + jnp.arange(SAMPLE_ROWS, dtype=jnp.int32), db),
        jnp.zeros((TAIL_ROWS - N_META - db * SAMPLE_ROWS,), jnp.int32)])
    tables = _rope_tables(jnp.concatenate([N_META + jnp.arange(s, dtype=jnp.int32), pos_tail]))
    lamp = jnp.stack([lambda_q1, lambda_k1, lambda_q2, lambda_k2]).astype(F32)
    g_sub = subln_g0.reshape(1, HEAD_DIM)

    def sample_rows(x):
        return x[N_META:N_META + db * SAMPLE_ROWS].reshape(db, SAMPLE_ROWS, N_HEADS, HEAD_DIM)[:, :ds]

    proj0 = _norm_matmul(x_flat, norm_mix[0], w_in0, tm_proj, 512)
    k0p, v0p, k0_tail = _kv_layout(proj0, b, s, tables)
    extra0 = (tables[0], tables[1], lamp, g_sub)
    mix0 = _attn_prompt(proj0, k0p, v0p, "diff", 0, b, s, tq, extra0)
    mix0 = _attn_tail(mix0, proj0, k0_tail, (0, 0), cache_k0, cache_v0, page_table, "diff", 0, b, s, ds, extra0)
    mix0, conv0_p = _conv_prompt(mix0, proj0, conv_w0, conv_b0, conv_ln_g0, conv_ln_b0, b, s, 128)
    mix0, conv0_s = _conv_tail(mix0, proj0, state_conv0, conv_w0, conv_b0, conv_ln_g0, conv_ln_b0, b, s, ds)
    h = _matmul_residual(mix0, w_out0, x_flat, tm_out, 512)
    h = _ffn(h, norm_ffn[0], ffn_gate0, ffn_up0, ffn_down0, tm_ffn, _pick_tile(ffn_gate0.shape[1], 512, LANES))

    proj1 = _norm_matmul(h, norm_mix[1], w_in1, tm_proj, 512)
    k1p, v1p = _kv_layout(proj1, b, s, None)
    tb = bs // TAIL_ROWS
    mix1 = _attn_prompt(proj1, k1p, v1p, "sb", 1, b, s, tq, None)
    mix1 = _attn_tail(mix1, proj1, proj1, (tb, 1), cache_k1, cache_v1, page_table, "sb", 1, b, s, ds, None)
    mix1, st_meta, hgrn_s = _hgrn_tail(mix1, proj1, hgrn_lb_logits, hgrn_norm_g1, state_hgrn1, b, s, ds)
    mix1, hgrn_p = _hgrn_prompt(mix1, proj1, hgrn_lb_logits, hgrn_norm_g1, st_meta, b, s, 256)
    h = _matmul_residual(mix1, w_out1, h, tm_out, 512)
    y_main, y_tail = _moe(h, norm_ffn[1], router1, moe_gate1, moe_up1, moe_down1, norm_final, bs, tm_moe)

    y_prompt = y_main.reshape(b, s, d)
    y_sample = y_tail[N_META:N_META + db * SAMPLE_ROWS].reshape(db, SAMPLE_ROWS, d)[:, :ds]
    heads = lambda x: x.reshape(b, N_META + s, N_HEADS, HEAD_DIM)
    tail0 = proj0[bs:]
    tail1 = proj1[bs:]
    return (y_prompt, y_sample,
            heads(k0p), heads(v0p), conv0_p, heads(k1p), heads(v1p), hgrn_p,
            sample_rows(k0_tail), sample_rows(tail0[:, 2 * W_MIX:3 * W_MIX]), conv0_s,
            sample_rows(tail1[:, W_MIX:2 * W_MIX]), sample_rows(tail1[:, 2 * W_MIX:3 * W_MIX]), hgrn_s)
```

```python
import functools
import math

import jax
import jax.numpy as jnp
from jax import lax
from jax.experimental import pallas as pl
from jax.experimental.pallas import tpu as pltpu

F32 = jnp.float32
BF16 = jnp.bfloat16

N_META = 16
EPS = 1e-6
ROPE_THETA = 10000.0
N_HEADS = 8
HEAD_DIM = 128
W_MIX = N_HEADS * HEAD_DIM
DK_A = 64
CONV_W = 31
N_EXPERTS = 8
TOP_K = 2
HALO = 32

LANES = 128
SUBLANES = 8
TAIL_ROWS = 128
SAMPLE_ROWS = 8
V7X_VMEM_BYTES = 64 * 1024 * 1024
VMEM_LIMIT_BYTES = 56 * 1024 * 1024
PAGES_PER_STEP = 8


def _lambda_init(layer):
    return 0.8 - 0.6 * math.exp(-0.3 * layer)


def _cparams(semantics, vmem=VMEM_LIMIT_BYTES):
    return pltpu.CompilerParams(dimension_semantics=semantics, vmem_limit_bytes=vmem)


def _pick_tile(n, cap, mult):
    best = None
    for t in range(mult, min(n, cap) + 1, mult):
        if n % t == 0:
            best = t
    assert best is not None, (n, cap, mult)
    return best


def _dot(a, b):
    return jnp.dot(a, b, preferred_element_type=F32)


def _dot_nt(a, b):
    return lax.dot_general(a, b, (((1,), (1,)), ((), ())), preferred_element_type=F32)


def _sigmoid(x):
    return 1.0 / (1.0 + jnp.exp(-x))


def _log_sigmoid(x):
    return jnp.minimum(x, 0.0) - jnp.log(1.0 + jnp.exp(-jnp.abs(x)))


def _rms(x, g):
    return x * lax.rsqrt(jnp.mean(x * x, axis=-1, keepdims=True) + EPS) * g


def _split2(x):
    hi = x.astype(BF16)
    lo = (x - hi.astype(F32)).astype(BF16)
    return hi, lo


def _dot3(a, b):
    return _dot(a[0], b[0]) + _dot(a[1], b[0]) + _dot(a[0], b[1])


def _dot3_nt(a, b):
    return _dot_nt(a[0], b[0]) + _dot_nt(a[1], b[0]) + _dot_nt(a[0], b[1])


def _lhs_store(a_scr, a, precise):
    if precise:
        hi, lo = _split2(a)
        a_scr[0] = hi
        a_scr[1] = lo
    else:
        a_scr[0] = a.astype(BF16)


def _lhs_load(a_scr, precise):
    return (a_scr[0], a_scr[1]) if precise else a_scr[0]


def _mm(a, w, precise):
    if not precise:
        return _dot(a, w.astype(BF16))
    a_hi, a_lo = a
    w_hi, w_lo = _split2(w)
    return _dot(a_hi, w_hi) + _dot(a_lo, w_hi) + _dot(a_hi, w_lo)


def _proj_kernel(x_ref, g_ref, w_ref, *rest, precise):
    o_ref, a_scr = rest[-2:]

    @pl.when(pl.program_id(1) == 0)
    def _():
        _lhs_store(a_scr, _rms(x_ref[...], g_ref[...]), precise)

    o_ref[...] = _mm(_lhs_load(a_scr, precise), w_ref[...], precise)


def _outproj_kernel(x_ref, w_ref, r_ref, *rest, precise):
    o_ref, a_scr = rest[-2:]

    @pl.when(pl.program_id(1) == 0)
    def _():
        _lhs_store(a_scr, x_ref[...], precise)

    o_ref[...] = r_ref[...] + _mm(_lhs_load(a_scr, precise), w_ref[...], precise)


def _ffn_kernel(h_ref, g_ref, wg_ref, wu_ref, wd_ref, *rest, precise):
    o_ref, a_scr = rest[-2:]
    f = pl.program_id(1)

    @pl.when(f == 0)
    def _():
        h = h_ref[...]
        _lhs_store(a_scr, _rms(h, g_ref[...]), precise)
        o_ref[...] = h

    a = _lhs_load(a_scr, precise)
    gate = _mm(a, wg_ref[...], precise)
    up = _mm(a, wu_ref[...], precise)
    t = (gate * _sigmoid(gate)) * up
    o_ref[...] += _mm(_split2(t) if precise else t.astype(BF16), wd_ref[...], precise)


def _dense_call(kernel_fn, name, x, consts, weights, res, out_width, out_tile, tm, n_steps, prev, single_rows):
    r, d = x.shape
    precise = prev is not None
    if precise:
        tm, row0, n_row_tiles = TAIL_ROWS, (r - TAIL_ROWS) // TAIL_ROWS, 1
    else:
        row0, n_row_tiles = 0, r // tm
    if out_tile == out_width:
        out_map = lambda i, j: (row0 + i, 0)
    else:
        out_map = lambda i, j: (row0 + i, j)
    row_mode = dict(pipeline_mode=pl.Buffered(1)) if single_rows else {}
    in_specs = [pl.BlockSpec((tm, d), lambda i, j: (row0 + i, 0), **row_mode)]
    args = [x]
    for arr in consts:
        in_specs.append(pl.BlockSpec(arr.shape, lambda i, j: (0, 0)))
        args.append(arr)
    for arr, shape, imap in weights:
        in_specs.append(pl.BlockSpec(shape, lambda i, j, imap=imap: imap(j)))
        args.append(arr)
    if res is not None:
        in_specs.append(pl.BlockSpec((tm, out_tile), out_map))
        args.append(res)
    aliases = {}
    if precise:
        in_specs.append(pl.BlockSpec(memory_space=pl.ANY))
        args.append(prev)
        aliases = {len(args) - 1: 0}
    return pl.pallas_call(
        functools.partial(kernel_fn, precise=precise),
        out_shape=jax.ShapeDtypeStruct((r, out_width), F32),
        grid=(n_row_tiles, n_steps),
        in_specs=in_specs,
        out_specs=pl.BlockSpec((tm, out_tile), out_map, **(row_mode if out_tile == out_width else {})),
        scratch_shapes=[pltpu.VMEM((2 if precise else 1, tm, d), BF16)],
        input_output_aliases=aliases,
        compiler_params=_cparams(("parallel", "arbitrary")),
        name=name + ("_tail" if precise else ""),
    )(*args)


def _norm_matmul(x, g, w, tm, tn):
    d, n = w.shape
    run = lambda prev: _dense_call(_proj_kernel, "norm_matmul", x, [g.reshape(1, d)],
                                   [(w, (d, tn), lambda j: (0, j))], None, n, tn, tm, n // tn, prev, False)
    return run(run(None))


def _matmul_residual(x, w, res, tm, tn):
    d, n = w.shape
    run = lambda prev: _dense_call(_outproj_kernel, "matmul_residual", x, [],
                                   [(w, (d, tn), lambda j: (0, j))], res, n, tn, tm, n // tn, prev, False)
    return run(run(None))


def _ffn(h, g, wg, wu, wd, tm, tf):
    d, ff = wg.shape
    weights = [(wg, (d, tf), lambda j: (0, j)), (wu, (d, tf), lambda j: (0, j)), (wd, (tf, d), lambda j: (j, 0))]
    run = lambda prev: _dense_call(_ffn_kernel, "ffn_swiglu", h, [g.reshape(1, d)], weights, None,
                                   d, d, tm, ff // tf, prev, True)
    return run(run(None))


def _rope_tables(pos):
    half = DK_A // 2
    inv = ROPE_THETA ** (-jnp.arange(half, dtype=F32) * 2.0 / DK_A)
    ang = pos.astype(F32)[:, None] * inv[None, :]
    cos = jnp.cos(ang)
    sin = jnp.sin(ang)
    cos_t = jnp.concatenate([cos, cos, cos, cos], axis=1)
    sin_t = jnp.concatenate([-sin, sin, -sin, sin], axis=1)
    return cos_t, sin_t


def _rope(x, cos_t, sin_t):
    half = DK_A // 2
    lane = lax.broadcasted_iota(jnp.int32, x.shape, 1)
    first = (lane % DK_A) < half
    partner = jnp.where(first, pltpu.roll(x, LANES - half, axis=1), pltpu.roll(x, half, axis=1))
    return x * cos_t + partner * sin_t


def _kv_layout_kernel(*refs, rope, s):
    if rope:
        (k_ref, kt_ref, v_ref, vt_ref, cos_ref, sin_ref, cost_ref, sint_ref, kp_ref, vp_ref, ktr_ref) = refs
        k = _rope(k_ref[...], cos_ref[...], sin_ref[...])
        kt = _rope(kt_ref[...], cost_ref[...], sint_ref[...])
        ktr_ref[...] = kt
    else:
        (k_ref, kt_ref, v_ref, vt_ref, kp_ref, vp_ref) = refs
        k = k_ref[...]
        kt = kt_ref[...]
    kp_ref[0, 0:N_META, :] = kt[0:N_META]
    kp_ref[0, N_META:N_META + s, :] = k
    vp_ref[0, 0:N_META, :] = vt_ref[0:N_META, :]
    vp_ref[0, N_META:N_META + s, :] = v_ref[...]


def _kv_layout(proj, b, s, tables):
    tb = (b * s) // TAIL_ROWS
    kcol, vcol = W_MIX // HEAD_DIM, 2 * W_MIX // HEAD_DIM
    rope = tables is not None
    in_specs = [
        pl.BlockSpec((s, HEAD_DIM), lambda h, bb: (bb, kcol + h)),
        pl.BlockSpec((TAIL_ROWS, HEAD_DIM), lambda h, bb: (tb, kcol + h)),
        pl.BlockSpec((s, HEAD_DIM), lambda h, bb: (bb, vcol + h)),
        pl.BlockSpec((TAIL_ROWS, HEAD_DIM), lambda h, bb: (tb, vcol + h)),
    ]
    args = [proj, proj, proj, proj]
    out_shape = [jax.ShapeDtypeStruct((b, N_META + s, W_MIX), F32)] * 2
    out_specs = [pl.BlockSpec((1, N_META + s, HEAD_DIM), lambda h, bb: (bb, 0, h))] * 2
    if rope:
        cos_t, sin_t = tables
        in_specs += [
            pl.BlockSpec((s, LANES), lambda h, bb: (0, 0)),
            pl.BlockSpec((s, LANES), lambda h, bb: (0, 0)),
            pl.BlockSpec((TAIL_ROWS, LANES), lambda h, bb: (s // TAIL_ROWS, 0)),
            pl.BlockSpec((TAIL_ROWS, LANES), lambda h, bb: (s // TAIL_ROWS, 0)),
        ]
        args += [cos_t, sin_t, cos_t, sin_t]
        out_shape = out_shape + [jax.ShapeDtypeStruct((TAIL_ROWS, W_MIX), F32)]
        out_specs = out_specs + [pl.BlockSpec((TAIL_ROWS, HEAD_DIM), lambda h, bb: (0, h))]
    return pl.pallas_call(
        functools.partial(_kv_layout_kernel, rope=rope, s=s),
        out_shape=out_shape,
        grid=(N_HEADS, b),
        in_specs=in_specs,
        out_specs=out_specs,
        compiler_params=_cparams(("arbitrary", "arbitrary")),
        name="kv_layout_rope" if rope else "kv_layout",
    )(*args)


def _diff_lambda(lamp_ref, layer):
    lp = lamp_ref[...]
    d1 = jnp.sum(lp[0:1] * lp[1:2], axis=-1, keepdims=True)
    d2 = jnp.sum(lp[2:3] * lp[3:4], axis=-1, keepdims=True)
    return jnp.exp(d1) - jnp.exp(d2) + _lambda_init(layer)


def _diff_queries_f32(q):
    lane = lax.broadcasted_iota(jnp.int32, q.shape, 1)
    qs = q * (DK_A ** -0.5)
    q1 = jnp.where(lane < DK_A, qs, 0.0)
    q2 = jnp.where(lane >= DK_A, qs, 0.0)
    return jnp.concatenate([q1, q2], axis=0)


def _diff_queries(q):
    return _diff_queries_f32(q).astype(BF16)


def _diff_update(carry, qs, kc, vc, mask2):
    m, l, acc = carry
    sc = _dot_nt(qs, kc.astype(BF16))
    if mask2 is not None:
        sc = jnp.where(mask2, sc, -jnp.inf)
    m_new = jnp.maximum(m, jnp.max(sc, axis=-1, keepdims=True))
    alpha = jnp.exp(m - m_new)
    p = jnp.exp(sc - m_new)
    l = alpha * l + jnp.sum(p, axis=-1, keepdims=True)
    acc = alpha * acc + _dot(p.astype(BF16), vc.astype(BF16))
    return m_new, l, acc


def _diff_finish(carry, n, lam, g, layer):
    _, l, acc = carry
    o = acc[:n] / l[:n] - lam * (acc[n:] / l[n:])
    return _rms(o, g) * (1.0 - _lambda_init(layer))


def _upper_ones(n):
    j = lax.broadcasted_iota(jnp.int32, (n, n), 0)
    s = lax.broadcasted_iota(jnp.int32, (n, n), 1)
    return jnp.where(j > s, 1.0, 0.0).astype(BF16)


def _later_ones(n):
    s = lax.broadcasted_iota(jnp.int32, (n, n), 0)
    j = lax.broadcasted_iota(jnp.int32, (n, n), 1)
    return jnp.where(j > s, 1.0, 0.0).astype(BF16)


SB_SCAN = 256


def _sb_update(carry, qb, kc, vc, mask, u):
    c, acc = carry
    n = kc.shape[0]
    z = _dot_nt(qb, kc.astype(BF16)) * (HEAD_DIM ** -0.5)
    ls = _log_sigmoid(z)
    l1m = ls - z
    if mask is not None:
        l1m = jnp.where(mask, l1m, 0.0)
    blk = u.shape[0]
    parts = [None] * (n // blk)
    for i in reversed(range(n // blk)):
        sl = slice(blk * i, blk * (i + 1))
        hi, lo = _split2(l1m[:, sl])
        parts[i] = jnp.exp(ls[:, sl] + (_dot(hi, u) + _dot(lo, u)) + c)
        c = c + jnp.sum(l1m[:, sl], axis=-1, keepdims=True)
    a = parts[0] if len(parts) == 1 else jnp.concatenate(parts, axis=1)
    if mask is not None:
        a = jnp.where(mask, a, 0.0)
    acc = acc + _dot(a.astype(BF16), vc.astype(BF16))
    return c, acc


def _attn_prompt_kernel(*refs, mode, tq, layer):
    if mode == "diff":
        q_ref, k_ref, v_ref, cos_ref, sin_ref, lamp_ref, g_ref, o_ref = refs
    else:
        q_ref, k_ref, v_ref, o_ref = refs
    i = pl.program_id(2)
    rows = lax.broadcasted_iota(jnp.int32, (tq, tq), 0)
    cols = lax.broadcasted_iota(jnp.int32, (tq, tq), 1)

    def chunk(j):
        start = pl.multiple_of(N_META + j * tq, SUBLANES)
        return k_ref[0, pl.ds(start, tq), :], v_ref[0, pl.ds(start, tq), :]

    k_meta = k_ref[0, 0:N_META, :]
    v_meta = v_ref[0, 0:N_META, :]

    if mode == "diff":
        qs = _diff_queries(_rope(q_ref[...], cos_ref[...], sin_ref[...]))
        carry = (jnp.full((2 * tq, 1), -jnp.inf, F32), jnp.zeros((2 * tq, 1), F32),
                 jnp.zeros((2 * tq, HEAD_DIM), F32))
        carry = _diff_update(carry, qs, k_meta, v_meta, None)

        def body(j, c):
            kc, vc = chunk(j)
            return _diff_update(c, qs, kc, vc, None)

        carry = lax.fori_loop(0, i, body, carry)
        kc, vc = chunk(i)
        causal = cols <= rows
        carry = _diff_update(carry, qs, kc, vc, jnp.concatenate([causal, causal], axis=0))
        o_ref[...] = _diff_finish(carry, tq, _diff_lambda(lamp_ref, layer), g_ref[...], layer)
    else:
        qb = q_ref[...].astype(BF16)
        u = _upper_ones(min(tq, SB_SCAN))
        carry = (jnp.zeros((tq, 1), F32), jnp.zeros((tq, HEAD_DIM), F32))
        kc, vc = chunk(i)
        carry = _sb_update(carry, qb, kc, vc, cols < rows, u)

        def body(it, c):
            kc, vc = chunk(i - 1 - it)
            return _sb_update(c, qb, kc, vc, None, u)

        carry = lax.fori_loop(0, i, body, carry)
        carry = _sb_update(carry, qb, k_meta, v_meta, None, _upper_ones(N_META))
        o_ref[...] = carry[1]


def _attn_prompt(proj, kp, vp, mode, layer, b, s, tq, extra):
    r = proj.shape[0]
    nq = s // tq
    in_specs = [
        pl.BlockSpec((tq, HEAD_DIM), lambda bb, h, i: (bb * nq + i, h)),
        pl.BlockSpec((1, N_META + s, HEAD_DIM), lambda bb, h, i: (bb, 0, h)),
        pl.BlockSpec((1, N_META + s, HEAD_DIM), lambda bb, h, i: (bb, 0, h)),
    ]
    args = [proj, kp, vp]
    if mode == "diff":
        cos_t, sin_t, lamp, g = extra
        in_specs += [
            pl.BlockSpec((tq, LANES), lambda bb, h, i: (i, 0)),
            pl.BlockSpec((tq, LANES), lambda bb, h, i: (i, 0)),
            pl.BlockSpec(lamp.shape, lambda bb, h, i: (0, 0)),
            pl.BlockSpec((1, HEAD_DIM), lambda bb, h, i: (0, 0)),
        ]
        args += [cos_t, sin_t, lamp, g]
    return pl.pallas_call(
        functools.partial(_attn_prompt_kernel, mode=mode, tq=tq, layer=layer),
        out_shape=jax.ShapeDtypeStruct((r, 2 * W_MIX), F32),
        grid=(b, N_HEADS, nq),
        in_specs=in_specs,
        out_specs=pl.BlockSpec((tq, HEAD_DIM), lambda bb, h, i: (bb * nq + i, h)),
        compiler_params=_cparams(("parallel", "parallel", "arbitrary")),
        name="attn_prompt_" + mode,
    )(*args)


def _tail_columns(mode, ds):
    if mode == "diff":
        rq = 4 if ds <= 4 else SAMPLE_ROWS
        return rq, 2 * rq
    return SAMPLE_ROWS, SAMPLE_ROWS


def _attn_tail_kernel(pt_ref, *refs, mode, layer, ds, n_steps):
    del pt_ref
    pps = PAGES_PER_STEP
    if mode == "diff":
        (q_ref, k_ref, v_ref, cos_ref, sin_ref, lamp_ref, g_ref) = refs[:7]
        refs = refs[7:]
    else:
        (q_ref, k_ref, v_ref) = refs[:3]
        refs = refs[3:]
    kc_refs = refs[:pps]
    vc_refs = refs[pps:2 * pps]
    _mix_in, o_ref, q_scr, m_scr, l_scr, acc_scr = refs[2 * pps:]
    b = pl.program_id(0)
    st = pl.program_id(1)
    row0 = pl.multiple_of(N_META + SAMPLE_ROWS * b, SUBLANES)

    def head(x, h):
        return x[:, HEAD_DIM * h:HEAD_DIM * (h + 1)]

    @pl.when((b == 0) & (st == 0))
    def _():
        o_ref[...] = jnp.zeros_like(o_ref)
        rows = lax.broadcasted_iota(jnp.int32, (N_META, N_META), 0)
        cols = lax.broadcasted_iota(jnp.int32, (N_META, N_META), 1)
        qm = q_ref[0:N_META, :]
        km = k_ref[0:N_META, :]
        vm = v_ref[0:N_META, :]
        for h in range(N_HEADS):
            if mode == "diff":
                qs = _diff_queries(_rope(head(qm, h), cos_ref[0:N_META, :], sin_ref[0:N_META, :]))
                causal = cols <= rows
                carry = (jnp.full((2 * N_META, 1), -jnp.inf, F32), jnp.zeros((2 * N_META, 1), F32),
                         jnp.zeros((2 * N_META, HEAD_DIM), F32))
                carry = _diff_update(carry, qs, head(km, h), head(vm, h),
                                     jnp.concatenate([causal, causal], axis=0))
                o = _diff_finish(carry, N_META, _diff_lambda(lamp_ref, layer), g_ref[...], layer)
            else:
                carry = (jnp.zeros((N_META, 1), F32), jnp.zeros((N_META, HEAD_DIM), F32))
                carry = _sb_update(carry, head(qm, h).astype(BF16), head(km, h), head(vm, h),
                                   cols < rows, _upper_ones(N_META))
                o = carry[1]
            o_ref[0:N_META, HEAD_DIM * h:HEAD_DIM * (h + 1)] = o

    rq, cph = _tail_columns(mode, ds)
    nc = N_HEADS * cph

    def scores(kflat):
        k_hi, k_lo = _split2(kflat)
        r = _dot_nt(q_scr[...], k_hi)
        return r[:nc] + r[nc:] + _dot_nt(q_scr[0:nc, :], k_lo)

    def weighted_values(p, vflat):
        p_hi, p_lo = _split2(p)
        v_hi, v_lo = _split2(vflat)
        r = _dot(jnp.concatenate([p_hi, p_lo], axis=0), jnp.concatenate([v_hi, v_lo], axis=1))
        return r[:nc, :HEAD_DIM] + r[:nc, HEAD_DIM:] + r[nc:, :HEAD_DIM]

    def cumsum_and_total(l1m):
        hi, lo = _split2(l1m)
        ones = jnp.ones((LANES, LANES), BF16)
        r = _dot(jnp.concatenate([hi, lo], axis=0), jnp.concatenate([_upper_ones(LANES), ones], axis=1))
        return r[:nc, :LANES] + r[nc:, :LANES], r[:nc, LANES:] + r[nc:, LANES:]

    def flat(ref):
        return ref[0].reshape(LANES * N_HEADS, HEAD_DIM)

    col = lax.broadcasted_iota(jnp.int32, (nc, LANES), 0)
    lane = lax.broadcasted_iota(jnp.int32, (nc, LANES), 1)
    page_lane = lax.broadcasted_iota(jnp.int32, (nc, LANES * N_HEADS), 1)
    page_col = lax.broadcasted_iota(jnp.int32, (nc, LANES * N_HEADS), 0)
    own_head = (page_lane % N_HEADS) == (page_col // cph)

    @pl.when(st == 0)
    def _():
        q8 = q_ref[pl.ds(row0, SAMPLE_ROWS), :]
        blocks = []
        for h in range(N_HEADS):
            qh = head(q8, h)
            if mode == "diff":
                qh = _diff_queries_f32(_rope(qh, cos_ref[pl.ds(row0, SAMPLE_ROWS), :],
                                             sin_ref[pl.ds(row0, SAMPLE_ROWS), :]))
                qh = jnp.concatenate([qh[0:rq], qh[SAMPLE_ROWS:SAMPLE_ROWS + rq]], axis=0)
            blocks.append(qh)
        q_hi, q_lo = _split2(jnp.concatenate(blocks, axis=0))
        q_scr[0:nc, :] = q_hi
        q_scr[nc:2 * nc, :] = q_lo

        k8 = k_ref[pl.ds(row0, SAMPLE_ROWS), :]
        v8 = v_ref[pl.ds(row0, SAMPLE_ROWS), :]
        pad = jnp.zeros((LANES - SAMPLE_ROWS * N_HEADS, HEAD_DIM), F32)
        k_new = jnp.concatenate([head(k8, h) for h in range(N_HEADS)] + [pad], axis=0)
        v_new = jnp.concatenate([head(v8, h) for h in range(N_HEADS)] + [pad], axis=0)
        tok = lane % SAMPLE_ROWS
        qrow = col % rq
        ok = (lane // SAMPLE_ROWS == col // cph) & (tok < ds)
        sc = scores(k_new)
        if mode == "diff":
            sc = jnp.where(ok & (tok <= qrow), sc, -jnp.inf)
            m = jnp.max(sc, axis=1, keepdims=True)
            p = jnp.exp(sc - m)
            m_scr[...] = jnp.broadcast_to(m, (nc, LANES))
            l_scr[...] = jnp.broadcast_to(jnp.sum(p, axis=1, keepdims=True), (nc, LANES))
        else:
            ok = ok & (tok < qrow)
            z = sc * (HEAD_DIM ** -0.5)
            ls = _log_sigmoid(z)
            between, total = cumsum_and_total(jnp.where(ok, ls - z, 0.0))
            p = jnp.where(ok, jnp.exp(ls + between), 0.0)
            m_scr[...] = total
            l_scr[...] = jnp.zeros_like(l_scr)
        acc_scr[...] = weighted_values(p, v_new)

    if mode == "diff":
        m = m_scr[:, 0:1]
        l = l_scr[:, 0:1]
        acc = acc_scr[...]
        for j in range(pps):
            sc = jnp.where(own_head, scores(flat(kc_refs[j])), -jnp.inf)
            m_new = jnp.maximum(m, jnp.max(sc, axis=1, keepdims=True))
            alpha = jnp.exp(m - m_new)
            p = jnp.exp(sc - m_new)
            l = alpha * l + jnp.sum(p, axis=1, keepdims=True)
            acc = alpha * acc + weighted_values(p, flat(vc_refs[j]))
            m = m_new
        m_scr[...] = jnp.broadcast_to(m, (nc, LANES))
        l_scr[...] = jnp.broadcast_to(l, (nc, LANES))
        acc_scr[...] = acc
    else:
        running = m_scr[...]
        acc = acc_scr[...]
        n_blk = N_HEADS
        for j in reversed(range(pps)):
            z = scores(flat(kc_refs[j])) * (HEAD_DIM ** -0.5)
            ls = _log_sigmoid(z)
            l1m = jnp.where(own_head, ls - z, 0.0)
            ps = [None] * n_blk
            for i in reversed(range(n_blk)):
                blk = slice(LANES * i, LANES * (i + 1))
                between, total = cumsum_and_total(l1m[:, blk])
                ps[i] = jnp.where(own_head[:, blk], jnp.exp(ls[:, blk] + between + running), 0.0)
                running = running + total
            acc = acc + weighted_values(jnp.concatenate(ps, axis=1), flat(vc_refs[j]))
        m_scr[...] = running
        acc_scr[...] = acc

    @pl.when(st == n_steps - 1)
    def _():
        acc = acc_scr[...]
        real = lax.broadcasted_iota(jnp.int32, (SAMPLE_ROWS, HEAD_DIM), 0) < ds
        outs = []
        if mode == "diff":
            acc = acc / l_scr[:, 0:1]
            lam = _diff_lambda(lamp_ref, layer)
            for h in range(N_HEADS):
                o = acc[cph * h:cph * h + rq] - lam * acc[cph * h + rq:cph * h + 2 * rq]
                o = _rms(o, g_ref[...]) * (1.0 - _lambda_init(layer))
                if rq < SAMPLE_ROWS:
                    o = jnp.concatenate([o, jnp.zeros((SAMPLE_ROWS - rq, HEAD_DIM), F32)], axis=0)
                outs.append(jnp.where(real, o, 0.0))
        else:
            for h in range(N_HEADS):
                outs.append(jnp.where(real, acc[cph * h:cph * (h + 1)], 0.0))
        o_ref[pl.ds(row0, SAMPLE_ROWS), :] = jnp.concatenate(outs, axis=1)


def _attn_tail(mix, proj, k_tail, k_tail_col, cache_k, cache_v, page_table, mode, layer, b, s, ds, extra):
    r = proj.shape[0]
    tb = (b * s) // TAIL_ROWS
    db, n_pages = page_table.shape
    pps = PAGES_PER_STEP
    assert n_pages % pps == 0
    n_steps = n_pages // pps
    page = cache_k.shape[1]
    assert page == LANES and cache_k.shape[2:] == (N_HEADS, HEAD_DIM)
    page_block = (1, page, N_HEADS, HEAD_DIM)
    vcol = 2
    nc = N_HEADS * _tail_columns(mode, ds)[1]

    in_specs = [
        pl.BlockSpec((TAIL_ROWS, W_MIX), lambda bb, st, pt: (tb, 0)),
        pl.BlockSpec((TAIL_ROWS, W_MIX), lambda bb, st, pt: k_tail_col),
        pl.BlockSpec((TAIL_ROWS, W_MIX), lambda bb, st, pt: (tb, vcol)),
    ]
    args = [proj, k_tail, proj]
    if mode == "diff":
        cos_t, sin_t, lamp, g = extra
        in_specs += [
            pl.BlockSpec((TAIL_ROWS, LANES), lambda bb, st, pt: (s // TAIL_ROWS, 0)),
            pl.BlockSpec((TAIL_ROWS, LANES), lambda bb, st, pt: (s // TAIL_ROWS, 0)),
            pl.BlockSpec(lamp.shape, lambda bb, st, pt: (0, 0)),
            pl.BlockSpec((1, HEAD_DIM), lambda bb, st, pt: (0, 0)),
        ]
        args += [cos_t, sin_t, lamp, g]

    def page_spec(j):
        if mode == "diff":
            return pl.BlockSpec(page_block, lambda bb, st, pt: (pt[bb, st * pps + j], 0, 0, 0))
        return pl.BlockSpec(page_block, lambda bb, st, pt: (pt[bb, n_pages - (st + 1) * pps + j], 0, 0, 0))

    in_specs += [page_spec(j) for j in range(pps)] + [page_spec(j) for j in range(pps)]
    args += [cache_k] * pps + [cache_v] * pps
    in_specs.append(pl.BlockSpec(memory_space=pl.ANY))
    args.append(mix)
    n_in = len(args)

    grid_spec = pltpu.PrefetchScalarGridSpec(
        num_scalar_prefetch=1,
        grid=(db, n_steps),
        in_specs=in_specs,
        out_specs=pl.BlockSpec((TAIL_ROWS, W_MIX), lambda bb, st, pt: (tb, 0)),
        scratch_shapes=[
            pltpu.VMEM((2 * nc, HEAD_DIM), BF16),
            pltpu.VMEM((nc, LANES), F32),
            pltpu.VMEM((nc, LANES), F32),
            pltpu.VMEM((nc, HEAD_DIM), F32),
        ],
    )
    return pl.pallas_call(
        functools.partial(_attn_tail_kernel, mode=mode, layer=layer, ds=ds, n_steps=n_steps),
        out_shape=jax.ShapeDtypeStruct((r, 2 * W_MIX), F32),
        grid_spec=grid_spec,
        input_output_aliases={n_in: 0},
        compiler_params=_cparams(("arbitrary", "arbitrary")),
        name="attn_tail_" + mode,
    )(page_table, *args)


def _glu(val, gate):
    return val * _sigmoid(gate)


def _dwconv_rows(buf_ref, w_ref, n_rows, row_strip):
    off = HALO - (CONV_W - 1)
    strips = []
    for r0 in range(0, n_rows, row_strip):
        cols = []
        for c0 in range(0, W_MIX, LANES):
            acc = jnp.zeros((row_strip, LANES), F32)
            for k in range(CONV_W):
                acc = acc + w_ref[k:k + 1, c0:c0 + LANES] * buf_ref[r0 + off + k:r0 + off + k + row_strip, c0:c0 + LANES]
            cols.append(acc)
        strips.append(jnp.concatenate(cols, axis=1))
    return strips[0] if len(strips) == 1 else jnp.concatenate(strips, axis=0)


def _ln_silu(y, g, b):
    yc = y - jnp.mean(y, axis=-1, keepdims=True)
    yn = yc * lax.rsqrt(jnp.mean(yc * yc, axis=-1, keepdims=True) + EPS) * g + b
    return yn * _sigmoid(yn)


def _conv_prompt_kernel(val_ref, gate_ref, tval_ref, tgate_ref, w_ref, b_ref, g_ref, beta_ref, _mix_in,
                        o_ref, cp_ref, buf_scr, *, t):
    i = pl.program_id(1)

    @pl.when(i == 0)
    def _():
        buf_scr[0:HALO - N_META, :] = jnp.zeros((HALO - N_META, W_MIX), F32)
        buf_scr[HALO - N_META:HALO, :] = _glu(tval_ref[0:N_META, :], tgate_ref[0:N_META, :])

    @pl.when(i > 0)
    def _():
        buf_scr[0:HALO, :] = buf_scr[t:t + HALO, :]

    buf_scr[HALO:HALO + t, :] = _glu(val_ref[...], gate_ref[...])
    y = _dwconv_rows(buf_scr, w_ref, t, 64) + b_ref[...]
    o_ref[...] = _ln_silu(y, g_ref[...], beta_ref[...])

    @pl.when(i == pl.num_programs(1) - 1)
    def _():
        cp_ref[0] = buf_scr[HALO + t - (CONV_W - 1):HALO + t, :]


def _conv_prompt(mix, proj, w, bias, g, beta, b, s, t):
    r = proj.shape[0]
    tb = (b * s) // TAIL_ROWS
    nb = s // t
    vcol, gcol = 3, 4
    row = lambda x: x.reshape(1, W_MIX)
    full = lambda bb, i: (0, 0)
    return pl.pallas_call(
        functools.partial(_conv_prompt_kernel, t=t),
        out_shape=[jax.ShapeDtypeStruct((r, 2 * W_MIX), F32),
                   jax.ShapeDtypeStruct((b, CONV_W - 1, W_MIX), F32)],
        grid=(b, nb),
        in_specs=[
            pl.BlockSpec((t, W_MIX), lambda bb, i: (bb * nb + i, vcol)),
            pl.BlockSpec((t, W_MIX), lambda bb, i: (bb * nb + i, gcol)),
            pl.BlockSpec((TAIL_ROWS, W_MIX), lambda bb, i: (tb, vcol)),
            pl.BlockSpec((TAIL_ROWS, W_MIX), lambda bb, i: (tb, gcol)),
            pl.BlockSpec((CONV_W, W_MIX), full),
            pl.BlockSpec((1, W_MIX), full),
            pl.BlockSpec((1, W_MIX), full),
            pl.BlockSpec((1, W_MIX), full),
            pl.BlockSpec(memory_space=pl.ANY),
        ],
        out_specs=[pl.BlockSpec((t, W_MIX), lambda bb, i: (bb * nb + i, 1)),
                   pl.BlockSpec((1, CONV_W - 1, W_MIX), lambda bb, i: (bb, 0, 0))],
        scratch_shapes=[pltpu.VMEM((HALO + t, W_MIX), F32)],
        input_output_aliases={8: 0},
        compiler_params=_cparams(("parallel", "arbitrary")),
        name="conv_prompt",
    )(proj, proj, proj, proj, w, row(bias), row(g), row(beta), mix)


def _conv_tail_kernel(tval_ref, tgate_ref, st_ref, w_ref, b_ref, g_ref, beta_ref, _mix_in,
                      o_ref, cs_ref, buf_scr, *, db, ds):
    u = _glu(tval_ref[...], tgate_ref[...])
    o_ref[...] = jnp.zeros_like(o_ref)
    keep = CONV_W - 1

    def finish(n_rows):
        y = _dwconv_rows(buf_scr, w_ref, n_rows, n_rows) + b_ref[...]
        return _ln_silu(y, g_ref[...], beta_ref[...])

    buf_scr[0:HALO, :] = jnp.zeros((HALO, W_MIX), F32)
    buf_scr[HALO:HALO + N_META, :] = u[0:N_META]
    o_ref[0:N_META, :] = finish(N_META)
    real = lax.broadcasted_iota(jnp.int32, (SAMPLE_ROWS, W_MIX), 0) < ds
    for b in range(db):
        r0 = N_META + SAMPLE_ROWS * b
        buf_scr[0:HALO, :] = st_ref[b]
        buf_scr[HALO:HALO + SAMPLE_ROWS, :] = u[r0:r0 + SAMPLE_ROWS]
        o_ref[r0:r0 + SAMPLE_ROWS, :] = jnp.where(real, finish(SAMPLE_ROWS), 0.0)
        cs_ref[b] = buf_scr[HALO + ds - keep:HALO + ds, :]


def _conv_tail(mix, proj, state, w, bias, g, beta, b, s, ds):
    r = proj.shape[0]
    tb = (b * s) // TAIL_ROWS
    db = state.shape[0]
    keep = CONV_W - 1
    vcol, gcol = 3, 4
    state_pad = jnp.pad(state, ((0, 0), (HALO - keep, 0), (0, 0)))
    row = lambda x: x.reshape(1, W_MIX)
    full2 = lambda i: (0, 0)
    full3 = lambda i: (0, 0, 0)
    return pl.pallas_call(
        functools.partial(_conv_tail_kernel, db=db, ds=ds),
        out_shape=[jax.ShapeDtypeStruct((r, 2 * W_MIX), F32),
                   jax.ShapeDtypeStruct((db, keep, W_MIX), F32)],
        grid=(1,),
        in_specs=[
            pl.BlockSpec((TAIL_ROWS, W_MIX), lambda i: (tb, vcol)),
            pl.BlockSpec((TAIL_ROWS, W_MIX), lambda i: (tb, gcol)),
            pl.BlockSpec((db, HALO, W_MIX), full3),
            pl.BlockSpec((CONV_W, W_MIX), full2),
            pl.BlockSpec((1, W_MIX), full2),
            pl.BlockSpec((1, W_MIX), full2),
            pl.BlockSpec((1, W_MIX), full2),
            pl.BlockSpec(memory_space=pl.ANY),
        ],
        out_specs=[pl.BlockSpec((TAIL_ROWS, W_MIX), lambda i: (tb, 1)),
                   pl.BlockSpec((db, keep, W_MIX), full3)],
        scratch_shapes=[pltpu.VMEM((HALO + N_META, W_MIX), F32)],
        input_output_aliases={7: 0},
        compiler_params=_cparams(("arbitrary",)),
        name="conv_tail",
    )(proj, proj, state_pad, w, row(bias), row(g), row(beta), mix)


def _group_ref(cb, c, lvl):
    grp, half = 2 << lvl, 1 << lvl
    if grp >= 2 * SUBLANES:
        x3 = cb.reshape(c // grp, grp, LANES)
        ref = x3[:, half - 1:half, :]
        return jnp.broadcast_to(ref, (c // grp, grp, LANES)).reshape(c, LANES)
    x3 = cb.reshape(c // SUBLANES, SUBLANES, LANES)
    sub = lax.broadcasted_iota(jnp.int32, x3.shape, 1)
    out = None
    for g0 in range(0, SUBLANES, grp):
        ref = jnp.broadcast_to(x3[:, g0 + half - 1:g0 + half, :], x3.shape)
        out = ref if out is None else jnp.where(sub >= g0, ref, out)
    return out.reshape(c, LANES)


def _pad_rows(a):
    c = a.shape[0]
    if c % LANES == 0:
        return a
    return jnp.concatenate([a, jnp.zeros((LANES - c % LANES, a.shape[1]), a.dtype)], axis=0)


def _hgrn_chunk(hq, hf, hi, lb, st, c, valid, precise=False):
    if precise:
        mm_nt = lambda x, y: _dot3_nt(_split2(x), _split2(y))
        mm = lambda x, y: _dot3(_split2(x), _split2(y))
    else:
        mm_nt = lambda x, y: _dot_nt(x.astype(BF16), y.astype(BF16))
        mm = lambda x, y: _dot(x.astype(BF16), y.astype(BF16))
    f = lb + (1.0 - lb) * _sigmoid(hf)
    logf = jnp.log(f)
    k = 1.0 - f
    if valid is not None:
        logf = jnp.where(valid, logf, 0.0)
        k = jnp.where(valid, k, 0.0)
    row = lax.broadcasted_iota(jnp.int32, (c, LANES), 0)
    cb = logf
    sh = 1
    while sh < c:
        cb = cb + jnp.where(row >= sh, pltpu.roll(cb, sh, axis=0), 0.0)
        sh *= 2
    rows = lax.broadcasted_iota(jnp.int32, (c, c), 0)
    cols = lax.broadcasted_iota(jnp.int32, (c, c), 1)
    a = jnp.where(rows == cols, mm_nt(hq, k), 0.0)
    lvl = 0
    while (1 << lvl) < c:
        ref = _group_ref(cb, c, lvl)
        right = ((row >> lvl) & 1) == 1
        qe = jnp.where(right, hq * jnp.exp(jnp.minimum(cb - ref, 0.0)), 0.0)
        ke = jnp.where(right, 0.0, k * jnp.exp(jnp.minimum(ref - cb, 0.0)))
        a = a + jnp.where((rows >> (lvl + 1)) == (cols >> (lvl + 1)), mm_nt(qe, ke), 0.0)
        lvl += 1
    o = mm(a, hi) + mm_nt(hq * jnp.exp(cb), st)
    last = cb[c - 1:c, :]
    st_new = st * jnp.exp(last) + mm(jnp.transpose(_pad_rows(hi)), _pad_rows(k * jnp.exp(last - cb)))
    return o, st_new


def _hgrn_lb(lbl_ref):
    lg = lbl_ref[...]
    m = jnp.max(lg, axis=0, keepdims=True)
    e = jnp.exp(lg - m)
    sm = e / jnp.sum(e, axis=0, keepdims=True)
    return (sm[0:1] + sm[1:2]) - sm[0:1]


def _hgrn_out(o, g, hg):
    return _rms(o, g) * (hg * _sigmoid(hg))


def _hgrn_prompt_kernel(hq_ref, hf_ref, hi_ref, hg_ref, lbl_ref, g_ref, st0_ref, _mix_in, o_ref, s_ref, st_scr, *, c):
    ci = pl.program_id(2)

    @pl.when(ci == 0)
    def _():
        st_scr[...] = st0_ref[0]

    o, st = _hgrn_chunk(hq_ref[...], hf_ref[...], hi_ref[...], _hgrn_lb(lbl_ref), st_scr[...], c, None)
    st_scr[...] = st
    o_ref[...] = _hgrn_out(o, g_ref[...], hg_ref[...])

    @pl.when(ci == pl.num_programs(2) - 1)
    def _():
        s_ref[0, 0] = jnp.transpose(st)


def _hgrn_cols():
    base = 3 * W_MIX // HEAD_DIM
    return base, base + N_HEADS, base + 2 * N_HEADS, base + 3 * N_HEADS


def _hgrn_prompt(mix, proj, lb_logits, g, st_meta, b, s, c):
    r = proj.shape[0]
    nc = s // c
    qc, fc, ic, gc = _hgrn_cols()
    blk = lambda col: pl.BlockSpec((c, HEAD_DIM), lambda bb, h, ci: (bb * nc + ci, col + h))
    return pl.pallas_call(
        functools.partial(_hgrn_prompt_kernel, c=c),
        out_shape=[jax.ShapeDtypeStruct((r, 2 * W_MIX), F32),
                   jax.ShapeDtypeStruct((b, N_HEADS, HEAD_DIM, HEAD_DIM), F32)],
        grid=(b, N_HEADS, nc),
        in_specs=[
            blk(qc), blk(fc), blk(ic), blk(gc),
            pl.BlockSpec((2, HEAD_DIM), lambda bb, h, ci: (0, h)),
            pl.BlockSpec((1, HEAD_DIM), lambda bb, h, ci: (0, 0)),
            pl.BlockSpec((1, HEAD_DIM, HEAD_DIM), lambda bb, h, ci: (h, 0, 0)),
            pl.BlockSpec(memory_space=pl.ANY),
        ],
        out_specs=[pl.BlockSpec((c, HEAD_DIM), lambda bb, h, ci: (bb * nc + ci, N_HEADS + h)),
                   pl.BlockSpec((1, 1, HEAD_DIM, HEAD_DIM), lambda bb, h, ci: (bb, h, 0, 0))],
        scratch_shapes=[pltpu.VMEM((HEAD_DIM, HEAD_DIM), F32)],
        input_output_aliases={7: 0},
        compiler_params=_cparams(("parallel", "parallel", "arbitrary")),
        name="hgrn_prompt",
    )(proj, proj, proj, proj, lb_logits, g.reshape(1, HEAD_DIM), st_meta, mix)


def _hgrn_tail_kernel(hq_ref, hf_ref, hi_ref, hg_ref, lbl_ref, g_ref, s0_ref, _mix_in,
                      o_ref, stm_ref, sn_ref, st_scr, *, db, ds):
    j = pl.program_id(1)
    c = 2 * SAMPLE_ROWS
    row = lax.broadcasted_iota(jnp.int32, (c, HEAD_DIM), 0)
    lb = _hgrn_lb(lbl_ref)

    @pl.when(j == 0)
    def _():
        o, st = _hgrn_chunk(hq_ref[...], hf_ref[...], hi_ref[...], lb, jnp.zeros((HEAD_DIM, HEAD_DIM), F32), c, None)
        stm_ref[0] = st
        o_ref[...] = _hgrn_out(o, g_ref[...], hg_ref[...])

    @pl.when((j > 0) & (j <= db // 2))
    def _():
        outs = []
        for half in range(2):
            valid = (row >= half * SAMPLE_ROWS) & (row < half * SAMPLE_ROWS + ds)
            st0 = jnp.transpose(s0_ref[half, 0])
            o, st = _hgrn_chunk(hq_ref[...], hf_ref[...], hi_ref[...], lb, st0, c, valid, precise=True)
            sn_ref[half, 0] = jnp.transpose(st)
            outs.append(jnp.where(valid, o, 0.0))
        o_ref[...] = _hgrn_out(outs[0] + outs[1], g_ref[...], hg_ref[...])

    @pl.when(j > db // 2)
    def _():
        o_ref[...] = jnp.zeros_like(o_ref)


def _hgrn_tail(mix, proj, lb_logits, g, state, b, s, ds):
    r = proj.shape[0]
    db = state.shape[0]
    assert db % 2 == 0
    c = 2 * SAMPLE_ROWS
    row_blk0 = (b * s) // c
    n_chunks = TAIL_ROWS // c
    qc, fc, ic, gc = _hgrn_cols()
    blk = lambda col: pl.BlockSpec((c, HEAD_DIM), lambda h, j: (row_blk0 + j, col + h))
    pair = lambda h, j: (jnp.clip(j - 1, 0, db // 2 - 1), h, 0, 0)
    return pl.pallas_call(
        functools.partial(_hgrn_tail_kernel, db=db, ds=ds),
        out_shape=[jax.ShapeDtypeStruct((r, 2 * W_MIX), F32),
                   jax.ShapeDtypeStruct((N_HEADS, HEAD_DIM, HEAD_DIM), F32),
                   jax.ShapeDtypeStruct((db, N_HEADS, HEAD_DIM, HEAD_DIM), F32)],
        grid=(N_HEADS, n_chunks),
        in_specs=[
            blk(qc), blk(fc), blk(ic), blk(gc),
            pl.BlockSpec((2, HEAD_DIM), lambda h, j: (0, h)),
            pl.BlockSpec((1, HEAD_DIM), lambda h, j: (0, 0)),
            pl.BlockSpec((2, 1, HEAD_DIM, HEAD_DIM), pair),
            pl.BlockSpec(memory_space=pl.ANY),
        ],
        out_specs=[pl.BlockSpec((c, HEAD_DIM), lambda h, j: (row_blk0 + j, N_HEADS + h)),
                   pl.BlockSpec((1, HEAD_DIM, HEAD_DIM), lambda h, j: (h, 0, 0)),
                   pl.BlockSpec((2, 1, HEAD_DIM, HEAD_DIM), pair)],
        scratch_shapes=[pltpu.VMEM((HEAD_DIM, HEAD_DIM), F32)],
        input_output_aliases={7: 0},
        compiler_params=_cparams(("arbitrary", "arbitrary")),
        name="hgrn_tail",
    )(proj, proj, proj, proj, lb_logits, g.reshape(1, HEAD_DIM), state, mix)


def _router_kernel(h_ref, g_ref, w_ref, eid_ref, gate_ref, a_ref):
    a = _rms(h_ref[...], g_ref[...])
    a_ref[...] = a.astype(BF16)
    logits = jnp.dot(a, w_ref[...], preferred_element_type=F32, precision=lax.Precision.HIGHEST)
    lane = lax.broadcasted_iota(jnp.int32, logits.shape, 1)
    logits = jnp.where(lane < N_EXPERTS, logits, -jnp.inf)
    m1 = jnp.max(logits, axis=-1, keepdims=True)
    i1 = jnp.min(jnp.where(logits == m1, lane, LANES), axis=-1, keepdims=True)
    rest = jnp.where(lane == i1, -jnp.inf, logits)
    m2 = jnp.max(rest, axis=-1, keepdims=True)
    i2 = jnp.min(jnp.where(rest == m2, lane, LANES), axis=-1, keepdims=True)
    e2 = jnp.exp(m2 - m1)
    g1 = 1.0 / (1.0 + e2)
    g2 = e2 / (1.0 + e2)
    eid_ref[...] = jnp.where(lane == 0, i1, jnp.where(lane == 1, i2, 0))
    gate_ref[...] = jnp.where(lane == 0, g1, jnp.where(lane == 1, g2, 0.0))


def _router(h, g, w_router, tm):
    r, d = h.shape
    w_pad = jnp.pad(w_router, ((0, 0), (0, LANES - N_EXPERTS)))
    eid, gate, a = pl.pallas_call(
        _router_kernel,
        out_shape=[jax.ShapeDtypeStruct((r, LANES), jnp.int32), jax.ShapeDtypeStruct((r, LANES), F32),
                   jax.ShapeDtypeStruct((r, d), BF16)],
        grid=(r // tm,),
        in_specs=[
            pl.BlockSpec((tm, d), lambda i: (i, 0)),
            pl.BlockSpec((1, d), lambda i: (0, 0)),
            pl.BlockSpec((d, LANES), lambda i: (0, 0)),
        ],
        out_specs=[pl.BlockSpec((tm, LANES), lambda i: (i, 0)), pl.BlockSpec((tm, LANES), lambda i: (i, 0)),
                   pl.BlockSpec((tm, d), lambda i: (i, 0))],
        compiler_params=_cparams(("parallel",)),
        name="moe_router",
    )(h, g.reshape(1, d), w_pad)
    return eid, gate, a


def _moe_permute_kernel(first_ref, last_ref, tok_ref, a_ref, o_ref, *, sblk):
    i = pl.program_id(0)
    k = pl.program_id(1)
    blk = first_ref[i] + k

    @pl.when(k == 0)
    def _():
        o_ref[...] = jnp.zeros_like(o_ref)

    @pl.when(blk <= last_ref[i])
    def _():
        col = lax.broadcasted_iota(jnp.int32, (o_ref.shape[0], sblk), 1)
        onehot = jnp.where(tok_ref[...] - blk * sblk == col, 1.0, 0.0).astype(BF16)
        o_ref[...] += _dot(onehot, a_ref[...]).astype(BF16)


def _moe_permute(a, slot_tok, tile_first, tile_last, tm, sblk):
    r, d = a.shape
    cap = slot_tok.shape[0]
    grid_spec = pltpu.PrefetchScalarGridSpec(
        num_scalar_prefetch=2,
        grid=(cap // tm, r // sblk),
        in_specs=[
            pl.BlockSpec((tm, 1), lambda i, k, tf, tl: (i, 0)),
            pl.BlockSpec((sblk, d), lambda i, k, tf, tl: (jnp.minimum(tf[i] + k, jnp.maximum(tl[i], 0)), 0)),
        ],
        out_specs=pl.BlockSpec((tm, d), lambda i, k, tf, tl: (i, 0)),
    )
    return pl.pallas_call(
        functools.partial(_moe_permute_kernel, sblk=sblk),
        out_shape=jax.ShapeDtypeStruct((cap, d), BF16),
        grid_spec=grid_spec,
        compiler_params=_cparams(("arbitrary", "arbitrary")),
        name="moe_permute",
    )(tile_first, tile_last, slot_tok.reshape(cap, 1), a)


MOE_GROUP = 2


def _moe_ffn_kernel(ge_ref, g0_ref, gn_ref, x_ref, sg_ref, wg_ref, wu_ref, wd_ref, _y_in, y_ref, acc_scr, sem, *, tm):
    i = pl.program_id(0)
    f = pl.program_id(1)
    del ge_ref
    n_valid = gn_ref[i]
    wg = wg_ref[0].astype(BF16)
    wu = wu_ref[0].astype(BF16)
    wd = wd_ref[0].astype(BF16)

    def out_copy(g):
        return pltpu.make_async_copy(acc_scr.at[g], y_ref.at[pl.ds((g0_ref[i] + g) * tm, tm)], sem.at[g])

    for g in range(MOE_GROUP):
        @pl.when(g < n_valid)
        def _():
            a = x_ref[g * tm:(g + 1) * tm, :]
            gate = _dot(a, wg)
            up = _dot(a, wu)
            t = (gate * _sigmoid(gate)) * up

            @pl.when(f == 0)
            def _():
                acc_scr[g] = jnp.zeros((tm, acc_scr.shape[2]), F32)

            acc_scr[g] += _dot(t.astype(BF16), wd)

            @pl.when(f == pl.num_programs(1) - 1)
            def _():
                acc_scr[g] = acc_scr[g] * sg_ref[g * tm:(g + 1) * tm, :]
                out_copy(g).start()

    @pl.when(f == pl.num_programs(1) - 1)
    def _():
        for g in range(MOE_GROUP):
            @pl.when(g < n_valid)
            def _():
                out_copy(g).wait()


def _moe_ffn(x_sorted, slot_gate, wg, wu, wd, group_expert, group_first, group_count, n_groups, tm, tf):
    rows, d = x_sorted.shape
    cap = rows - (MOE_GROUP - 1) * tm
    ff = wg.shape[2]
    nf = ff // tf
    gm = MOE_GROUP * tm
    fidx = lambda i, f, gn: jnp.where(gn[i] > 0, f, nf - 1)
    grid_spec = pltpu.PrefetchScalarGridSpec(
        num_scalar_prefetch=3,
        grid=(n_groups, nf),
        in_specs=[
            pl.BlockSpec((pl.Element(gm), pl.Element(d)), lambda i, f, ge, g0, gn: (g0[i] * tm, 0),
                         pipeline_mode=pl.Buffered(1)),
            pl.BlockSpec((pl.Element(gm), pl.Element(1)), lambda i, f, ge, g0, gn: (g0[i] * tm, 0),
                         pipeline_mode=pl.Buffered(1)),
            pl.BlockSpec((1, d, tf), lambda i, f, ge, g0, gn: (ge[i], 0, fidx(i, f, gn))),
            pl.BlockSpec((1, d, tf), lambda i, f, ge, g0, gn: (ge[i], 0, fidx(i, f, gn))),
            pl.BlockSpec((1, tf, d), lambda i, f, ge, g0, gn: (ge[i], fidx(i, f, gn), 0)),
            pl.BlockSpec(memory_space=pl.ANY),
        ],
        out_specs=pl.BlockSpec(memory_space=pl.ANY),
        scratch_shapes=[pltpu.VMEM((MOE_GROUP, tm, d), F32), pltpu.SemaphoreType.DMA((MOE_GROUP,))],
    )
    return pl.pallas_call(
        functools.partial(_moe_ffn_kernel, tm=tm),
        out_shape=jax.ShapeDtypeStruct((cap, d), F32),
        grid_spec=grid_spec,
        input_output_aliases={8: 0},
        compiler_params=_cparams(("arbitrary", "arbitrary")),
        name="moe_ffn",
    )(group_expert, group_first, group_count, x_sorted, slot_gate.reshape(rows, 1), wg, wu, wd,
      jnp.zeros((cap, d), F32))


def _moe_combine_kernel(ws_ref, h_ref, eid_ref, dest_ref, g_ref, *refs, n_main, win):
    y_refs = refs[:N_EXPERTS]
    yp_ref, yt_ref = refs[N_EXPERTS:]
    j = pl.program_id(0)
    eid = eid_ref[...]
    dest = dest_ref[...]
    col = lax.broadcasted_iota(jnp.int32, (TAIL_ROWS, win), 1)
    acc = h_ref[...]
    for e in range(N_EXPERTS):
        start = ws_ref[j * N_EXPERTS + e] * SUBLANES
        onehot = jnp.zeros((TAIL_ROWS, win), BF16)
        for k in range(TOP_K):
            rel = jnp.where(eid[:, k:k + 1] == e, dest[:, k:k + 1] - start, -1)
            onehot = onehot + jnp.where(rel == col, 1.0, 0.0).astype(BF16)
        hi, lo = _split2(y_refs[e][...])
        acc = acc + _dot(onehot, hi) + _dot(onehot, lo)
    y = _rms(acc, g_ref[...])

    @pl.when(j < n_main)
    def _():
        yp_ref[...] = y

    @pl.when(j == n_main)
    def _():
        yt_ref[...] = y


def _moe_combine(h, y_sorted, eid, dest, win_start, g, n_main_rows, win):
    r, d = h.shape
    t = TAIL_ROWS
    n_main = n_main_rows // t
    window = lambda e: pl.BlockSpec((pl.Element(win), pl.Element(d)),
                                    lambda j, ws: (ws[j * N_EXPERTS + e] * SUBLANES, 0))
    grid_spec = pltpu.PrefetchScalarGridSpec(
        num_scalar_prefetch=1,
        grid=(r // t,),
        in_specs=[
            pl.BlockSpec((t, d), lambda j, ws: (j, 0)),
            pl.BlockSpec((t, LANES), lambda j, ws: (j, 0)),
            pl.BlockSpec((t, LANES), lambda j, ws: (j, 0)),
            pl.BlockSpec((1, d), lambda j, ws: (0, 0)),
        ] + [window(e) for e in range(N_EXPERTS)],
        out_specs=[pl.BlockSpec((t, d), lambda j, ws: (jnp.minimum(j, n_main - 1), 0)),
                   pl.BlockSpec((t, d), lambda j, ws: (0, 0))],
    )
    return pl.pallas_call(
        functools.partial(_moe_combine_kernel, n_main=n_main, win=win),
        out_shape=[jax.ShapeDtypeStruct((n_main_rows, d), F32), jax.ShapeDtypeStruct((t, d), F32)],
        grid_spec=grid_spec,
        compiler_params=_cparams(("arbitrary",)),
        name="moe_combine_norm",
    )(win_start, h, eid, dest, g.reshape(1, d), *([y_sorted] * N_EXPERTS))


def _moe(h, g_ffn, w_router, wg, wu, wd, g_final, n_main_rows, tm):
    r, d = h.shape
    eid_pad, gate_pad, a = _router(h, g_ffn, w_router, _pick_tile(r, 1024, LANES))
    eid, gate = eid_pad[:, :TOP_K], gate_pad[:, :TOP_K]
    n_slots = r * TOP_K
    big = jnp.int32(2 ** 30)
    e_flat = eid.reshape(n_slots)
    onehot = (e_flat[:, None] == jnp.arange(N_EXPERTS, dtype=jnp.int32)[None, :]).astype(jnp.int32)
    rank = jnp.take_along_axis(jnp.cumsum(onehot, axis=0), e_flat[:, None], axis=1)[:, 0] - 1
    counts = jnp.sum(onehot, axis=0)
    padded = (counts + tm - 1) // tm * tm
    pends = jnp.cumsum(padded)
    pstarts = pends - padded
    dest = (pstarts[e_flat] + rank).astype(jnp.int32)
    n_tiles = (n_slots + N_EXPERTS * (tm - 1)) // tm
    cap = n_tiles * tm
    extra = (MOE_GROUP - 1) * tm
    slot_tok = jnp.full((cap + extra,), -1, jnp.int32).at[dest].set(jnp.arange(n_slots, dtype=jnp.int32) // TOP_K)
    slot_gate = jnp.zeros((cap + extra,), F32).at[dest].set(gate.reshape(n_slots))
    tiles_e = padded // tm
    groups_e = (tiles_e + MOE_GROUP - 1) // MOE_GROUP
    gends = jnp.cumsum(groups_e)
    n_groups = n_tiles // MOE_GROUP + N_EXPERTS
    gi = jnp.arange(n_groups, dtype=jnp.int32)
    gi_used = jnp.minimum(gi, gends[-1] - 1)
    group_expert = jnp.minimum(
        jnp.sum((gends[None, :] <= gi_used[:, None]).astype(jnp.int32), axis=1), N_EXPERTS - 1)
    within = gi_used - (gends - groups_e)[group_expert]
    group_first = ((pstarts // tm)[group_expert] + within * MOE_GROUP).astype(jnp.int32)
    group_count = jnp.where(gi < gends[-1],
                            jnp.clip(tiles_e[group_expert] - within * MOE_GROUP, 0, MOE_GROUP), 0).astype(jnp.int32)
    sblk = tm
    tok_tiles = slot_tok.reshape(n_tiles + MOE_GROUP - 1, tm)
    tok_max = jnp.max(tok_tiles, axis=1)
    tok_min = jnp.min(jnp.where(tok_tiles >= 0, tok_tiles, big), axis=1)
    tile_first = jnp.where(tok_max >= 0, tok_min // sblk, 1).astype(jnp.int32)
    tile_last = jnp.where(tok_max >= 0, tok_max // sblk, 0).astype(jnp.int32)
    win = TAIL_ROWS + SUBLANES
    n_tok_tiles = r // TAIL_ROWS
    dest_e = jnp.where(onehot == 1, dest[:, None], big).reshape(n_tok_tiles, TAIL_ROWS * TOP_K, N_EXPERTS)
    first_row = jnp.min(dest_e, axis=1)
    win_start = jnp.clip(first_row // SUBLANES, 0, (cap - win) // SUBLANES).reshape(-1).astype(jnp.int32)

    x_sorted = _moe_permute(a, slot_tok, tile_first, tile_last, tm, sblk)
    y_sorted = _moe_ffn(x_sorted, slot_gate, wg, wu, wd, group_expert.astype(jnp.int32), group_first, group_count,
                        n_groups, tm, _pick_tile(wg.shape[2], 512, LANES))
    dest_pad = jnp.pad(dest.reshape(r, TOP_K), ((0, 0), (0, LANES - TOP_K)))
    return _moe_combine(h, y_sorted, eid_pad, dest_pad, win_start, g_final, n_main_rows, win)


def kernel(x_prompt, x_sample, cache_k0, cache_v0, state_conv0, cache_k1, cache_v1, state_hgrn1, page_table, meta_tokens, norm_mix, norm_ffn, norm_final, w_in0, lambda_q1, lambda_k1, lambda_q2, lambda_k2, subln_g0, conv_w0, conv_b0, conv_ln_g0, conv_ln_b0, w_out0, ffn_gate0, ffn_up0, ffn_down0, w_in1, hgrn_lb_logits, hgrn_norm_g1, w_out1, router1, moe_gate1, moe_up1, moe_down1):
    b, s, d = x_prompt.shape
    db, ds, _ = x_sample.shape
    n_pages = page_table.shape[1]
    page = cache_k0.shape[1]
    assert d == 2 * W_MIX and ds <= SAMPLE_ROWS and N_META + SAMPLE_ROWS * db <= TAIL_ROWS
    assert s % 256 == 0
    bs = b * s
    r = bs + TAIL_ROWS
    tm_proj = _pick_tile(r, 1664, 16)
    tm_out = _pick_tile(r, 1040, 16)
    tm_ffn = _pick_tile(r, 832, 16)
    tm_moe = _pick_tile(r, 768, LANES)
    tq = 512 if s % 512 == 0 else 256

    xs = jnp.pad(x_sample, ((0, 0), (0, SAMPLE_ROWS - ds), (0, 0))).reshape(db * SAMPLE_ROWS, d)
    tail = jnp.concatenate(
        [meta_tokens.astype(F32), xs, jnp.zeros((TAIL_ROWS - N_META - db * SAMPLE_ROWS, d), F32)], axis=0)
    x_flat = jnp.concatenate([x_prompt.reshape(bs, d), tail], axis=0)

    pos_tail = jnp.concatenate([
        jnp.arange(N_META, dtype=jnp.int32),
        jnp.tile(n_pages * page + jnp.arange(SAMPLE_ROWS, dtype=jnp.int32), db),
        jnp.zeros((TAIL_ROWS - N_META - db * SAMPLE_ROWS,), jnp.int32)])
    tables = _rope_tables(jnp.concatenate([N_META + jnp.arange(s, dtype=jnp.int32), pos_tail]))
    lamp = jnp.stack([lambda_q1, lambda_k1, lambda_q2, lambda_k2]).astype(F32)
    g_sub = subln_g0.reshape(1, HEAD_DIM)

    def sample_rows(x):
        return x[N_META:N_META + db * SAMPLE_ROWS].reshape(db, SAMPLE_ROWS, N_HEADS, HEAD_DIM)[:, :ds]

    proj0 = _norm_matmul(x_flat, norm_mix[0], w_in0, tm_proj, 512)
    k0p, v0p, k0_tail = _kv_layout(proj0, b, s, tables)
    extra0 = (tables[0], tables[1], lamp, g_sub)
    mix0 = _attn_prompt(proj0, k0p, v0p, "diff", 0, b, s, tq, extra0)
    mix0 = _attn_tail(mix0, proj0, k0_tail, (0, 0), cache_k0, cache_v0, page_table, "diff", 0, b, s, ds, extra0)
    mix0, conv0_p = _conv_prompt(mix0, proj0, conv_w0, conv_b0, conv_ln_g0, conv_ln_b0, b, s, 128)
    mix0, conv0_s = _conv_tail(mix0, proj0, state_conv0, conv_w0, conv_b0, conv_ln_g0, conv_ln_b0, b, s, ds)
    h = _matmul_residual(mix0, w_out0, x_flat, tm_out, 512)
    h = _ffn(h, norm_ffn[0], ffn_gate0, ffn_up0, ffn_down0, tm_ffn, _pick_tile(ffn_gate0.shape[1], 512, LANES))

    proj1 = _norm_matmul(h, norm_mix[1], w_in1, tm_proj, 512)
    k1p, v1p = _kv_layout(proj1, b, s, None)
    tb = bs // TAIL_ROWS
    mix1 = _attn_prompt(proj1, k1p, v1p, "sb", 1, b, s, tq, None)
    mix1 = _attn_tail(mix1, proj1, proj1, (tb, 1), cache_k1, cache_v1, page_table, "sb", 1, b, s, ds, None)
    mix1, st_meta, hgrn_s = _hgrn_tail(mix1, proj1, hgrn_lb_logits, hgrn_norm_g1, state_hgrn1, b, s, ds)
    mix1, hgrn_p = _hgrn_prompt(mix1, proj1, hgrn_lb_logits, hgrn_norm_g1, st_meta, b, s, 256)
    h = _matmul_residual(mix1, w_out1, h, tm_out, 512)
    y_main, y_tail = _moe(h, norm_ffn[1], router1, moe_gate1, moe_up1, moe_down1, norm_final, bs, tm_moe)

    y_prompt = y_main.reshape(b, s, d)
    y_sample = y_tail[N_META:N_META + db * SAMPLE_ROWS].reshape(db, SAMPLE_ROWS, d)[:, :ds]
    heads = lambda x: x.reshape(b, N_META + s, N_HEADS, HEAD_DIM)
    tail0 = proj0[bs:]
    tail1 = proj1[bs:]
    return (y_prompt, y_sample,
            heads(k0p), heads(v0p), conv0_p, heads(k1p), heads(v1p), hgrn_p,
            sample_rows(k0_tail), sample_rows(tail0[:, 2 * W_MIX:3 * W_MIX]), conv0_s,
            sample_rows(tail1[:, W_MIX:2 * W_MIX]), sample_rows(tail1[:, 2 * W_MIX:3 * W_MIX]), hgrn_s)
```

```python
import functools
import math

import jax
import jax.numpy as jnp
from jax import lax
from jax.experimental import pallas as pl
from jax.experimental.pallas import tpu as pltpu

F32 = jnp.float32
BF16 = jnp.bfloat16

N_META = 16
EPS = 1e-6
ROPE_THETA = 10000.0
N_HEADS = 8
HEAD_DIM = 128
W_MIX = N_HEADS * HEAD_DIM
DK_A = 64
CONV_W = 31
N_EXPERTS = 8
TOP_K = 2
HALO = 32

LANES = 128
SUBLANES = 8
TAIL_ROWS = 128
SAMPLE_ROWS = 8
V7X_VMEM_BYTES = 64 * 1024 * 1024
VMEM_LIMIT_BYTES = 56 * 1024 * 1024
PAGES_PER_STEP = 8


def _lambda_init(layer):
    return 0.8 - 0.6 * math.exp(-0.3 * layer)


def _cparams(semantics, vmem=VMEM_LIMIT_BYTES):
    return pltpu.CompilerParams(dimension_semantics=semantics, vmem_limit_bytes=vmem)


def _pick_tile(n, cap, mult):
    best = None
    for t in range(mult, min(n, cap) + 1, mult):
        if n % t == 0:
            best = t
    assert best is not None, (n, cap, mult)
    return best


def _dot(a, b):
    return jnp.dot(a, b, preferred_element_type=F32)


def _dot_nt(a, b):
    return lax.dot_general(a, b, (((1,), (1,)), ((), ())), preferred_element_type=F32)


def _sigmoid(x):
    return 1.0 / (1.0 + jnp.exp(-x))


def _log_sigmoid(x):
    return jnp.minimum(x, 0.0) - jnp.log(1.0 + jnp.exp(-jnp.abs(x)))


def _rms(x, g):
    return x * lax.rsqrt(jnp.mean(x * x, axis=-1, keepdims=True) + EPS) * g


def _split2(x):
    hi = x.astype(BF16)
    lo = (x - hi.astype(F32)).astype(BF16)
    return hi, lo


def _dot3(a, b):
    return _dot(a[0], b[0]) + _dot(a[1], b[0]) + _dot(a[0], b[1])


def _dot3_nt(a, b):
    return _dot_nt(a[0], b[0]) + _dot_nt(a[1], b[0]) + _dot_nt(a[0], b[1])


def _lhs_store(a_scr, a, precise):
    if precise:
        hi, lo = _split2(a)
        a_scr[0] = hi
        a_scr[1] = lo
    else:
        a_scr[0] = a.astype(BF16)


def _lhs_load(a_scr, precise):
    return (a_scr[0], a_scr[1]) if precise else a_scr[0]


def _mm(a, w, precise):
    if not precise:
        return _dot(a, w.astype(BF16))
    a_hi, a_lo = a
    w_hi, w_lo = _split2(w)
    return _dot(a_hi, w_hi) + _dot(a_lo, w_hi) + _dot(a_hi, w_lo)


def _proj_kernel(x_ref, g_ref, w_ref, *rest, precise):
    o_ref, a_scr = rest[-2:]

    @pl.when(pl.program_id(1) == 0)
    def _():
        _lhs_store(a_scr, _rms(x_ref[...], g_ref[...]), precise)

    o_ref[...] = _mm(_lhs_load(a_scr, precise), w_ref[...], precise)


def _outproj_kernel(x_ref, w_ref, r_ref, *rest, precise):
    o_ref, a_scr = rest[-2:]

    @pl.when(pl.program_id(1) == 0)
    def _():
        _lhs_store(a_scr, x_ref[...], precise)

    o_ref[...] = r_ref[...] + _mm(_lhs_load(a_scr, precise), w_ref[...], precise)


def _ffn_kernel(h_ref, g_ref, wg_ref, wu_ref, wd_ref, *rest, precise):
    o_ref, a_scr = rest[-2:]
    f = pl.program_id(1)

    @pl.when(f == 0)
    def _():
        h = h_ref[...]
        _lhs_store(a_scr, _rms(h, g_ref[...]), precise)
        o_ref[...] = h

    a = _lhs_load(a_scr, precise)
    gate = _mm(a, wg_ref[...], precise)
    up = _mm(a, wu_ref[...], precise)
    t = (gate * _sigmoid(gate)) * up
    o_ref[...] += _mm(_split2(t) if precise else t.astype(BF16), wd_ref[...], precise)


def _dense_call(kernel_fn, name, x, consts, weights, res, out_width, out_tile, tm, n_steps, prev, single_rows):
    r, d = x.shape
    precise = prev is not None
    if precise:
        tm, row0, n_row_tiles = TAIL_ROWS, (r - TAIL_ROWS) // TAIL_ROWS, 1
    else:
        row0, n_row_tiles = 0, r // tm
    if out_tile == out_width:
        out_map = lambda i, j: (row0 + i, 0)
    else:
        out_map = lambda i, j: (row0 + i, j)
    row_mode = dict(pipeline_mode=pl.Buffered(1)) if single_rows else {}
    in_specs = [pl.BlockSpec((tm, d), lambda i, j: (row0 + i, 0), **row_mode)]
    args = [x]
    for arr in consts:
        in_specs.append(pl.BlockSpec(arr.shape, lambda i, j: (0, 0)))
        args.append(arr)
    for arr, shape, imap in weights:
        in_specs.append(pl.BlockSpec(shape, lambda i, j, imap=imap: imap(j)))
        args.append(arr)
    if res is not None:
        in_specs.append(pl.BlockSpec((tm, out_tile), out_map))
        args.append(res)
    aliases = {}
    if precise:
        in_specs.append(pl.BlockSpec(memory_space=pl.ANY))
        args.append(prev)
        aliases = {len(args) - 1: 0}
    return pl.pallas_call(
        functools.partial(kernel_fn, precise=precise),
        out_shape=jax.ShapeDtypeStruct((r, out_width), F32),
        grid=(n_row_tiles, n_steps),
        in_specs=in_specs,
        out_specs=pl.BlockSpec((tm, out_tile), out_map, **(row_mode if out_tile == out_width else {})),
        scratch_shapes=[pltpu.VMEM((2 if precise else 1, tm, d), BF16)],
        input_output_aliases=aliases,
        compiler_params=_cparams(("parallel", "arbitrary")),
        name=name + ("_tail" if precise else ""),
    )(*args)


def _norm_matmul(x, g, w, tm, tn):
    d, n = w.shape
    run = lambda prev: _dense_call(_proj_kernel, "norm_matmul", x, [g.reshape(1, d)],
                                   [(w, (d, tn), lambda j: (0, j))], None, n, tn, tm, n // tn, prev, False)
    return run(run(None))


def _matmul_residual(x, w, res, tm, tn):
    d, n = w.shape
    run = lambda prev: _dense_call(_outproj_kernel, "matmul_residual", x, [],
                                   [(w, (d, tn), lambda j: (0, j))], res, n, tn, tm, n // tn, prev, False)
    return run(run(None))


def _ffn(h, g, wg, wu, wd, tm, tf):
    d, ff = wg.shape
    weights = [(wg, (d, tf), lambda j: (0, j)), (wu, (d, tf), lambda j: (0, j)), (wd, (tf, d), lambda j: (j, 0))]
    run = lambda prev: _dense_call(_ffn_kernel, "ffn_swiglu", h, [g.reshape(1, d)], weights, None,
                                   d, d, tm, ff // tf, prev, True)
    return run(run(None))


def _rope_tables(pos):
    half = DK_A // 2
    inv = ROPE_THETA ** (-jnp.arange(half, dtype=F32) * 2.0 / DK_A)
    ang = pos.astype(F32)[:, None] * inv[None, :]
    cos = jnp.cos(ang)
    sin = jnp.sin(ang)
    cos_t = jnp.concatenate([cos, cos, cos, cos], axis=1)
    sin_t = jnp.concatenate([-sin, sin, -sin, sin], axis=1)
    return cos_t, sin_t


def _rope(x, cos_t, sin_t):
    half = DK_A // 2
    lane = lax.broadcasted_iota(jnp.int32, x.shape, 1)
    first = (lane % DK_A) < half
    partner = jnp.where(first, pltpu.roll(x, LANES - half, axis=1), pltpu.roll(x, half, axis=1))
    return x * cos_t + partner * sin_t


def _kv_layout_kernel(*refs, rope, s):
    if rope:
        (k_ref, kt_ref, v_ref, vt_ref, cos_ref, sin_ref, cost_ref, sint_ref, kp_ref, vp_ref, ktr_ref) = refs
        k = _rope(k_ref[...], cos_ref[...], sin_ref[...])
        kt = _rope(kt_ref[...], cost_ref[...], sint_ref[...])
        ktr_ref[...] = kt
    else:
        (k_ref, kt_ref, v_ref, vt_ref, kp_ref, vp_ref) = refs
        k = k_ref[...]
        kt = kt_ref[...]
    kp_ref[0, 0:N_META, :] = kt[0:N_META]
    kp_ref[0, N_META:N_META + s, :] = k
    vp_ref[0, 0:N_META, :] = vt_ref[0:N_META, :]
    vp_ref[0, N_META:N_META + s, :] = v_ref[...]


def _kv_layout(proj, b, s, tables):
    tb = (b * s) // TAIL_ROWS
    kcol, vcol = W_MIX // HEAD_DIM, 2 * W_MIX // HEAD_DIM
    rope = tables is not None
    in_specs = [
        pl.BlockSpec((s, HEAD_DIM), lambda h, bb: (bb, kcol + h)),
        pl.BlockSpec((TAIL_ROWS, HEAD_DIM), lambda h, bb: (tb, kcol + h)),
        pl.BlockSpec((s, HEAD_DIM), lambda h, bb: (bb, vcol + h)),
        pl.BlockSpec((TAIL_ROWS, HEAD_DIM), lambda h, bb: (tb, vcol + h)),
    ]
    args = [proj, proj, proj, proj]
    out_shape = [jax.ShapeDtypeStruct((b, N_META + s, W_MIX), F32)] * 2
    out_specs = [pl.BlockSpec((1, N_META + s, HEAD_DIM), lambda h, bb: (bb, 0, h))] * 2
    if rope:
        cos_t, sin_t = tables
        in_specs += [
            pl.BlockSpec((s, LANES), lambda h, bb: (0, 0)),
            pl.BlockSpec((s, LANES), lambda h, bb: (0, 0)),
            pl.BlockSpec((TAIL_ROWS, LANES), lambda h, bb: (s // TAIL_ROWS, 0)),
            pl.BlockSpec((TAIL_ROWS, LANES), lambda h, bb: (s // TAIL_ROWS, 0)),
        ]
        args += [cos_t, sin_t, cos_t, sin_t]
        out_shape = out_shape + [jax.ShapeDtypeStruct((TAIL_ROWS, W_MIX), F32)]
        out_specs = out_specs + [pl.BlockSpec((TAIL_ROWS, HEAD_DIM), lambda h, bb: (0, h))]
    return pl.pallas_call(
        functools.partial(_kv_layout_kernel, rope=rope, s=s),
        out_shape=out_shape,
        grid=(N_HEADS, b),
        in_specs=in_specs,
        out_specs=out_specs,
        compiler_params=_cparams(("arbitrary", "arbitrary")),
        name="kv_layout_rope" if rope else "kv_layout",
    )(*args)


def _diff_lambda(lamp_ref, layer):
    lp = lamp_ref[...]
    d1 = jnp.sum(lp[0:1] * lp[1:2], axis=-1, keepdims=True)
    d2 = jnp.sum(lp[2:3] * lp[3:4], axis=-1, keepdims=True)
    return jnp.exp(d1) - jnp.exp(d2) + _lambda_init(layer)


def _diff_queries_f32(q):
    lane = lax.broadcasted_iota(jnp.int32, q.shape, 1)
    qs = q * (DK_A ** -0.5)
    q1 = jnp.where(lane < DK_A, qs, 0.0)
    q2 = jnp.where(lane >= DK_A, qs, 0.0)
    return jnp.concatenate([q1, q2], axis=0)


def _diff_queries(q):
    return _diff_queries_f32(q).astype(BF16)


def _diff_update(carry, qs, kc, vc, mask2):
    m, l, acc = carry
    sc = _dot_nt(qs, kc.astype(BF16))
    if mask2 is not None:
        sc = jnp.where(mask2, sc, -jnp.inf)
    m_new = jnp.maximum(m, jnp.max(sc, axis=-1, keepdims=True))
    alpha = jnp.exp(m - m_new)
    p = jnp.exp(sc - m_new)
    l = alpha * l + jnp.sum(p, axis=-1, keepdims=True)
    acc = alpha * acc + _dot(p.astype(BF16), vc.astype(BF16))
    return m_new, l, acc


def _diff_finish(carry, n, lam, g, layer):
    _, l, acc = carry
    o = acc[:n] / l[:n] - lam * (acc[n:] / l[n:])
    return _rms(o, g) * (1.0 - _lambda_init(layer))


def _upper_ones(n):
    j = lax.broadcasted_iota(jnp.int32, (n, n), 0)
    s = lax.broadcasted_iota(jnp.int32, (n, n), 1)
    return jnp.where(j > s, 1.0, 0.0).astype(BF16)


def _later_ones(n):
    s = lax.broadcasted_iota(jnp.int32, (n, n), 0)
    j = lax.broadcasted_iota(jnp.int32, (n, n), 1)
    return jnp.where(j > s, 1.0, 0.0).astype(BF16)


SB_SCAN = 256


def _sb_update(carry, qb, kc, vc, mask, u):
    c, acc = carry
    n = kc.shape[0]
    z = _dot_nt(qb, kc.astype(BF16)) * (HEAD_DIM ** -0.5)
    ls = _log_sigmoid(z)
    l1m = ls - z
    if mask is not None:
        l1m = jnp.where(mask, l1m, 0.0)
    blk = u.shape[0]
    parts = [None] * (n // blk)
    for i in reversed(range(n // blk)):
        sl = slice(blk * i, blk * (i + 1))
        hi, lo = _split2(l1m[:, sl])
        parts[i] = jnp.exp(ls[:, sl] + (_dot(hi, u) + _dot(lo, u)) + c)
        c = c + jnp.sum(l1m[:, sl], axis=-1, keepdims=True)
    a = parts[0] if len(parts) == 1 else jnp.concatenate(parts, axis=1)
    if mask is not None:
        a = jnp.where(mask, a, 0.0)
    acc = acc + _dot(a.astype(BF16), vc.astype(BF16))
    return c, acc


def _attn_prompt_kernel(*refs, mode, tq, layer):
    if mode == "diff":
        q_ref, k_ref, v_ref, cos_ref, sin_ref, lamp_ref, g_ref, o_ref = refs
    else:
        q_ref, k_ref, v_ref, o_ref = refs
    i = pl.program_id(2)
    rows = lax.broadcasted_iota(jnp.int32, (tq, tq), 0)
    cols = lax.broadcasted_iota(jnp.int32, (tq, tq), 1)

    def chunk(j):
        start = pl.multiple_of(N_META + j * tq, SUBLANES)
        return k_ref[0, pl.ds(start, tq), :], v_ref[0, pl.ds(start, tq), :]

    k_meta = k_ref[0, 0:N_META, :]
    v_meta = v_ref[0, 0:N_META, :]

    if mode == "diff":
        qs = _diff_queries(_rope(q_ref[...], cos_ref[...], sin_ref[...]))
        carry = (jnp.full((2 * tq, 1), -jnp.inf, F32), jnp.zeros((2 * tq, 1), F32),
                 jnp.zeros((2 * tq, HEAD_DIM), F32))
        carry = _diff_update(carry, qs, k_meta, v_meta, None)

        def body(j, c):
            kc, vc = chunk(j)
            return _diff_update(c, qs, kc, vc, None)

        carry = lax.fori_loop(0, i, body, carry)
        kc, vc = chunk(i)
        causal = cols <= rows
        carry = _diff_update(carry, qs, kc, vc, jnp.concatenate([causal, causal], axis=0))
        o_ref[...] = _diff_finish(carry, tq, _diff_lambda(lamp_ref, layer), g_ref[...], layer)
    else:
        qb = q_ref[...].astype(BF16)
        u = _upper_ones(min(tq, SB_SCAN))
        carry = (jnp.zeros((tq, 1), F32), jnp.zeros((tq, HEAD_DIM), F32))
        kc, vc = chunk(i)
        carry = _sb_update(carry, qb, kc, vc, cols < rows, u)

        def body(it, c):
            kc, vc = chunk(i - 1 - it)
            return _sb_update(c, qb, kc, vc, None, u)

        carry = lax.fori_loop(0, i, body, carry)
        carry = _sb_update(carry, qb, k_meta, v_meta, None, _upper_ones(N_META))
        o_ref[...] = carry[1]


def _attn_prompt(proj, kp, vp, mode, layer, b, s, tq, extra):
    r = proj.shape[0]
    nq = s // tq
    in_specs = [
        pl.BlockSpec((tq, HEAD_DIM), lambda bb, h, i: (bb * nq + i, h)),
        pl.BlockSpec((1, N_META + s, HEAD_DIM), lambda bb, h, i: (bb, 0, h)),
        pl.BlockSpec((1, N_META + s, HEAD_DIM), lambda bb, h, i: (bb, 0, h)),
    ]
    args = [proj, kp, vp]
    if mode == "diff":
        cos_t, sin_t, lamp, g = extra
        in_specs += [
            pl.BlockSpec((tq, LANES), lambda bb, h, i: (i, 0)),
            pl.BlockSpec((tq, LANES), lambda bb, h, i: (i, 0)),
            pl.BlockSpec(lamp.shape, lambda bb, h, i: (0, 0)),
            pl.BlockSpec((1, HEAD_DIM), lambda bb, h, i: (0, 0)),
        ]
        args += [cos_t, sin_t, lamp, g]
    return pl.pallas_call(
        functools.partial(_attn_prompt_kernel, mode=mode, tq=tq, layer=layer),
        out_shape=jax.ShapeDtypeStruct((r, 2 * W_MIX), F32),
        grid=(b, N_HEADS, nq),
        in_specs=in_specs,
        out_specs=pl.BlockSpec((tq, HEAD_DIM), lambda bb, h, i: (bb * nq + i, h)),
        compiler_params=_cparams(("parallel", "parallel", "arbitrary")),
        name="attn_prompt_" + mode,
    )(*args)


def _tail_columns(mode, ds):
    if mode == "diff":
        rq = 4 if ds <= 4 else SAMPLE_ROWS
        return rq, 2 * rq
    return SAMPLE_ROWS, SAMPLE_ROWS


def _attn_tail_kernel(pt_ref, *refs, mode, layer, ds, n_steps):
    del pt_ref
    pps = PAGES_PER_STEP
    if mode == "diff":
        (q_ref, k_ref, v_ref, cos_ref, sin_ref, lamp_ref, g_ref) = refs[:7]
        refs = refs[7:]
    else:
        (q_ref, k_ref, v_ref) = refs[:3]
        refs = refs[3:]
    kc_refs = refs[:pps]
    vc_refs = refs[pps:2 * pps]
    _mix_in, o_ref, q_scr, m_scr, l_scr, acc_scr = refs[2 * pps:]
    b = pl.program_id(0)
    st = pl.program_id(1)
    row0 = pl.multiple_of(N_META + SAMPLE_ROWS * b, SUBLANES)

    def head(x, h):
        return x[:, HEAD_DIM * h:HEAD_DIM * (h + 1)]

    @pl.when((b == 0) & (st == 0))
    def _():
        o_ref[...] = jnp.zeros_like(o_ref)
        rows = lax.broadcasted_iota(jnp.int32, (N_META, N_META), 0)
        cols = lax.broadcasted_iota(jnp.int32, (N_META, N_META), 1)
        qm = q_ref[0:N_META, :]
        km = k_ref[0:N_META, :]
        vm = v_ref[0:N_META, :]
        for h in range(N_HEADS):
            if mode == "diff":
                qs = _diff_queries(_rope(head(qm, h), cos_ref[0:N_META, :], sin_ref[0:N_META, :]))
                causal = cols <= rows
                carry = (jnp.full((2 * N_META, 1), -jnp.inf, F32), jnp.zeros((2 * N_META, 1), F32),
                         jnp.zeros((2 * N_META, HEAD_DIM), F32))
                carry = _diff_update(carry, qs, head(km, h), head(vm, h),
                                     jnp.concatenate([causal, causal], axis=0))
                o = _diff_finish(carry, N_META, _diff_lambda(lamp_ref, layer), g_ref[...], layer)
            else:
                carry = (jnp.zeros((N_META, 1), F32), jnp.zeros((N_META, HEAD_DIM), F32))
                carry = _sb_update(carry, head(qm, h).astype(BF16), head(km, h), head(vm, h),
                                   cols < rows, _upper_ones(N_META))
                o = carry[1]
            o_ref[0:N_META, HEAD_DIM * h:HEAD_DIM * (h + 1)] = o

    rq, cph = _tail_columns(mode, ds)
    nc = N_HEADS * cph

    def scores(kflat):
        k_hi, k_lo = _split2(kflat)
        r = _dot_nt(q_scr[...], k_hi)
        return r[:nc] + r[nc:] + _dot_nt(q_scr[0:nc, :], k_lo)

    def weighted_values(p, vflat):
        p_hi, p_lo = _split2(p)
        v_hi, v_lo = _split2(vflat)
        r = _dot(jnp.concatenate([p_hi, p_lo], axis=0), jnp.concatenate([v_hi, v_lo], axis=1))
        return r[:nc, :HEAD_DIM] + r[:nc, HEAD_DIM:] + r[nc:, :HEAD_DIM]

    def cumsum_and_total(l1m):
        hi, lo = _split2(l1m)
        ones = jnp.ones((LANES, LANES), BF16)
        r = _dot(jnp.concatenate([hi, lo], axis=0), jnp.concatenate([_upper_ones(LANES), ones], axis=1))
        return r[:nc, :LANES] + r[nc:, :LANES], r[:nc, LANES:] + r[nc:, LANES:]

    def flat(ref):
        return ref[0].reshape(LANES * N_HEADS, HEAD_DIM)

    col = lax.broadcasted_iota(jnp.int32, (nc, LANES), 0)
    lane = lax.broadcasted_iota(jnp.int32, (nc, LANES), 1)
    page_lane = lax.broadcasted_iota(jnp.int32, (nc, LANES * N_HEADS), 1)
    page_col = lax.broadcasted_iota(jnp.int32, (nc, LANES * N_HEADS), 0)
    own_head = (page_lane % N_HEADS) == (page_col // cph)

    @pl.when(st == 0)
    def _():
        q8 = q_ref[pl.ds(row0, SAMPLE_ROWS), :]
        blocks = []
        for h in range(N_HEADS):
            qh = head(q8, h)
            if mode == "diff":
                qh = _diff_queries_f32(_rope(qh, cos_ref[pl.ds(row0, SAMPLE_ROWS), :],
                                             sin_ref[pl.ds(row0, SAMPLE_ROWS), :]))
                qh = jnp.concatenate([qh[0:rq], qh[SAMPLE_ROWS:SAMPLE_ROWS + rq]], axis=0)
            blocks.append(qh)
        q_hi, q_lo = _split2(jnp.concatenate(blocks, axis=0))
        q_scr[0:nc, :] = q_hi
        q_scr[nc:2 * nc, :] = q_lo

        k8 = k_ref[pl.ds(row0, SAMPLE_ROWS), :]
        v8 = v_ref[pl.ds(row0, SAMPLE_ROWS), :]
        pad = jnp.zeros((LANES - SAMPLE_ROWS * N_HEADS, HEAD_DIM), F32)
        k_new = jnp.concatenate([head(k8, h) for h in range(N_HEADS)] + [pad], axis=0)
        v_new = jnp.concatenate([head(v8, h) for h in range(N_HEADS)] + [pad], axis=0)
        tok = lane % SAMPLE_ROWS
        qrow = col % rq
        ok = (lane // SAMPLE_ROWS == col // cph) & (tok < ds)
        sc = scores(k_new)
        if mode == "diff":
            sc = jnp.where(ok & (tok <= qrow), sc, -jnp.inf)
            m = jnp.max(sc, axis=1, keepdims=True)
            p = jnp.exp(sc - m)
            m_scr[...] = jnp.broadcast_to(m, (nc, LANES))
            l_scr[...] = jnp.broadcast_to(jnp.sum(p, axis=1, keepdims=True), (nc, LANES))
        else:
            ok = ok & (tok < qrow)
            z = sc * (HEAD_DIM ** -0.5)
            ls = _log_sigmoid(z)
            between, total = cumsum_and_total(jnp.where(ok, ls - z, 0.0))
            p = jnp.where(ok, jnp.exp(ls + between), 0.0)
            m_scr[...] = total
            l_scr[...] = jnp.zeros_like(l_scr)
        acc_scr[...] = weighted_values(p, v_new)

    if mode == "diff":
        m = m_scr[:, 0:1]
        l = l_scr[:, 0:1]
        acc = acc_scr[...]
        for j in range(pps):
            sc = jnp.where(own_head, scores(flat(kc_refs[j])), -jnp.inf)
            m_new = jnp.maximum(m, jnp.max(sc, axis=1, keepdims=True))
            alpha = jnp.exp(m - m_new)
            p = jnp.exp(sc - m_new)
            l = alpha * l + jnp.sum(p, axis=1, keepdims=True)
            acc = alpha * acc + weighted_values(p, flat(vc_refs[j]))
            m = m_new
        m_scr[...] = jnp.broadcast_to(m, (nc, LANES))
        l_scr[...] = jnp.broadcast_to(l, (nc, LANES))
        acc_scr[...] = acc
    else:
        running = m_scr[...]
        acc = acc_scr[...]
        n_blk = N_HEADS
        for j in reversed(range(pps)):
            z = scores(flat(kc_refs[j])) * (HEAD_DIM ** -0.5)
            ls = _log_sigmoid(z)
            l1m = jnp.where(own_head, ls - z, 0.0)
            ps = [None] * n_blk
            for i in reversed(range(n_blk)):
                blk = slice(LANES * i, LANES * (i + 1))
                between, total = cumsum_and_total(l1m[:, blk])
                ps[i] = jnp.where(own_head[:, blk], jnp.exp(ls[:, blk] + between + running), 0.0)
                running = running + total
            acc = acc + weighted_values(jnp.concatenate(ps, axis=1), flat(vc_refs[j]))
        m_scr[...] = running
        acc_scr[...] = acc

    @pl.when(st == n_steps - 1)
    def _():
        acc = acc_scr[...]
        real = lax.broadcasted_iota(jnp.int32, (SAMPLE_ROWS, HEAD_DIM), 0) < ds
        outs = []
        if mode == "diff":
            acc = acc / l_scr[:, 0:1]
            lam = _diff_lambda(lamp_ref, layer)
            for h in range(N_HEADS):
                o = acc[cph * h:cph * h + rq] - lam * acc[cph * h + rq:cph * h + 2 * rq]
                o = _rms(o, g_ref[...]) * (1.0 - _lambda_init(layer))
                if rq < SAMPLE_ROWS:
                    o = jnp.concatenate([o, jnp.zeros((SAMPLE_ROWS - rq, HEAD_DIM), F32)], axis=0)
                outs.append(jnp.where(real, o, 0.0))
        else:
            for h in range(N_HEADS):
                outs.append(jnp.where(real, acc[cph * h:cph * (h + 1)], 0.0))
        o_ref[pl.ds(row0, SAMPLE_ROWS), :] = jnp.concatenate(outs, axis=1)


def _attn_tail(mix, proj, k_tail, k_tail_col, cache_k, cache_v, page_table, mode, layer, b, s, ds, extra):
    r = proj.shape[0]
    tb = (b * s) // TAIL_ROWS
    db, n_pages = page_table.shape
    pps = PAGES_PER_STEP
    assert n_pages % pps == 0
    n_steps = n_pages // pps
    page = cache_k.shape[1]
    assert page == LANES and cache_k.shape[2:] == (N_HEADS, HEAD_DIM)
    page_block = (1, page, N_HEADS, HEAD_DIM)
    vcol = 2
    nc = N_HEADS * _tail_columns(mode, ds)[1]

    in_specs = [
        pl.BlockSpec((TAIL_ROWS, W_MIX), lambda bb, st, pt: (tb, 0)),
        pl.BlockSpec((TAIL_ROWS, W_MIX), lambda bb, st, pt: k_tail_col),
        pl.BlockSpec((TAIL_ROWS, W_MIX), lambda bb, st, pt: (tb, vcol)),
    ]
    args = [proj, k_tail, proj]
    if mode == "diff":
        cos_t, sin_t, lamp, g = extra
        in_specs += [
            pl.BlockSpec((TAIL_ROWS, LANES), lambda bb, st, pt: (s // TAIL_ROWS, 0)),
            pl.BlockSpec((TAIL_ROWS, LANES), lambda bb, st, pt: (s // TAIL_ROWS, 0)),
            pl.BlockSpec(lamp.shape, lambda bb, st, pt: (0, 0)),
            pl.BlockSpec((1, HEAD_DIM), lambda bb, st, pt: (0, 0)),
        ]
        args += [cos_t, sin_t, lamp, g]

    def page_spec(j):
        if mode == "diff":
            return pl.BlockSpec(page_block, lambda bb, st, pt: (pt[bb, st * pps + j], 0, 0, 0))
        return pl.BlockSpec(page_block, lambda bb, st, pt: (pt[bb, n_pages - (st + 1) * pps + j], 0, 0, 0))

    in_specs += [page_spec(j) for j in range(pps)] + [page_spec(j) for j in range(pps)]
    args += [cache_k] * pps + [cache_v] * pps
    in_specs.append(pl.BlockSpec(memory_space=pl.ANY))
    args.append(mix)
    n_in = len(args)

    grid_spec = pltpu.PrefetchScalarGridSpec(
        num_scalar_prefetch=1,
        grid=(db, n_steps),
        in_specs=in_specs,
        out_specs=pl.BlockSpec((TAIL_ROWS, W_MIX), lambda bb, st, pt: (tb, 0)),
        scratch_shapes=[
            pltpu.VMEM((2 * nc, HEAD_DIM), BF16),
            pltpu.VMEM((nc, LANES), F32),
            pltpu.VMEM((nc, LANES), F32),
            pltpu.VMEM((nc, HEAD_DIM), F32),
        ],
    )
    return pl.pallas_call(
        functools.partial(_attn_tail_kernel, mode=mode, layer=layer, ds=ds, n_steps=n_steps),
        out_shape=jax.ShapeDtypeStruct((r, 2 * W_MIX), F32),
        grid_spec=grid_spec,
        input_output_aliases={n_in: 0},
        compiler_params=_cparams(("arbitrary", "arbitrary")),
        name="attn_tail_" + mode,
    )(page_table, *args)


def _glu(val, gate):
    return val * _sigmoid(gate)


def _dwconv_rows(buf_ref, w_ref, n_rows, row_strip):
    off = HALO - (CONV_W - 1)
    strips = []
    for r0 in range(0, n_rows, row_strip):
        cols = []
        for c0 in range(0, W_MIX, LANES):
            acc = jnp.zeros((row_strip, LANES), F32)
            for k in range(CONV_W):
                acc = acc + w_ref[k:k + 1, c0:c0 + LANES] * buf_ref[r0 + off + k:r0 + off + k + row_strip, c0:c0 + LANES]
            cols.append(acc)
        strips.append(jnp.concatenate(cols, axis=1))
    return strips[0] if len(strips) == 1 else jnp.concatenate(strips, axis=0)


def _ln_silu(y, g, b):
    yc = y - jnp.mean(y, axis=-1, keepdims=True)
    yn = yc * lax.rsqrt(jnp.mean(yc * yc, axis=-1, keepdims=True) + EPS) * g + b
    return yn * _sigmoid(yn)


def _conv_prompt_kernel(val_ref, gate_ref, tval_ref, tgate_ref, w_ref, b_ref, g_ref, beta_ref, _mix_in,
                        o_ref, cp_ref, buf_scr, *, t):
    i = pl.program_id(1)

    @pl.when(i == 0)
    def _():
        buf_scr[0:HALO - N_META, :] = jnp.zeros((HALO - N_META, W_MIX), F32)
        buf_scr[HALO - N_META:HALO, :] = _glu(tval_ref[0:N_META, :], tgate_ref[0:N_META, :])

    @pl.when(i > 0)
    def _():
        buf_scr[0:HALO, :] = buf_scr[t:t + HALO, :]

    buf_scr[HALO:HALO + t, :] = _glu(val_ref[...], gate_ref[...])
    y = _dwconv_rows(buf_scr, w_ref, t, 64) + b_ref[...]
    o_ref[...] = _ln_silu(y, g_ref[...], beta_ref[...])

    @pl.when(i == pl.num_programs(1) - 1)
    def _():
        cp_ref[0] = buf_scr[HALO + t - (CONV_W - 1):HALO + t, :]


def _conv_prompt(mix, proj, w, bias, g, beta, b, s, t):
    r = proj.shape[0]
    tb = (b * s) // TAIL_ROWS
    nb = s // t
    vcol, gcol = 3, 4
    row = lambda x: x.reshape(1, W_MIX)
    full = lambda bb, i: (0, 0)
    return pl.pallas_call(
        functools.partial(_conv_prompt_kernel, t=t),
        out_shape=[jax.ShapeDtypeStruct((r, 2 * W_MIX), F32),
                   jax.ShapeDtypeStruct((b, CONV_W - 1, W_MIX), F32)],
        grid=(b, nb),
        in_specs=[
            pl.BlockSpec((t, W_MIX), lambda bb, i: (bb * nb + i, vcol)),
            pl.BlockSpec((t, W_MIX), lambda bb, i: (bb * nb + i, gcol)),
            pl.BlockSpec((TAIL_ROWS, W_MIX), lambda bb, i: (tb, vcol)),
            pl.BlockSpec((TAIL_ROWS, W_MIX), lambda bb, i: (tb, gcol)),
            pl.BlockSpec((CONV_W, W_MIX), full),
            pl.BlockSpec((1, W_MIX), full),
            pl.BlockSpec((1, W_MIX), full),
            pl.BlockSpec((1, W_MIX), full),
            pl.BlockSpec(memory_space=pl.ANY),
        ],
        out_specs=[pl.BlockSpec((t, W_MIX), lambda bb, i: (bb * nb + i, 1)),
                   pl.BlockSpec((1, CONV_W - 1, W_MIX), lambda bb, i: (bb, 0, 0))],
        scratch_shapes=[pltpu.VMEM((HALO + t, W_MIX), F32)],
        input_output_aliases={8: 0},
        compiler_params=_cparams(("parallel", "arbitrary")),
        name="conv_prompt",
    )(proj, proj, proj, proj, w, row(bias), row(g), row(beta), mix)


def _conv_tail_kernel(tval_ref, tgate_ref, st_ref, w_ref, b_ref, g_ref, beta_ref, _mix_in,
                      o_ref, cs_ref, buf_scr, *, db, ds):
    u = _glu(tval_ref[...], tgate_ref[...])
    o_ref[...] = jnp.zeros_like(o_ref)
    keep = CONV_W - 1

    def finish(n_rows):
        y = _dwconv_rows(buf_scr, w_ref, n_rows, n_rows) + b_ref[...]
        return _ln_silu(y, g_ref[...], beta_ref[...])

    buf_scr[0:HALO, :] = jnp.zeros((HALO, W_MIX), F32)
    buf_scr[HALO:HALO + N_META, :] = u[0:N_META]
    o_ref[0:N_META, :] = finish(N_META)
    real = lax.broadcasted_iota(jnp.int32, (SAMPLE_ROWS, W_MIX), 0) < ds
    for b in range(db):
        r0 = N_META + SAMPLE_ROWS * b
        buf_scr[0:HALO, :] = st_ref[b]
        buf_scr[HALO:HALO + SAMPLE_ROWS, :] = u[r0:r0 + SAMPLE_ROWS]
        o_ref[r0:r0 + SAMPLE_ROWS, :] = jnp.where(real, finish(SAMPLE_ROWS), 0.0)
        cs_ref[b] = buf_scr[HALO + ds - keep:HALO + ds, :]


def _conv_tail(mix, proj, state, w, bias, g, beta, b, s, ds):
    r = proj.shape[0]
    tb = (b * s) // TAIL_ROWS
    db = state.shape[0]
    keep = CONV_W - 1
    vcol, gcol = 3, 4
    state_pad = jnp.pad(state, ((0, 0), (HALO - keep, 0), (0, 0)))
    row = lambda x: x.reshape(1, W_MIX)
    full2 = lambda i: (0, 0)
    full3 = lambda i: (0, 0, 0)
    return pl.pallas_call(
        functools.partial(_conv_tail_kernel, db=db, ds=ds),
        out_shape=[jax.ShapeDtypeStruct((r, 2 * W_MIX), F32),
                   jax.ShapeDtypeStruct((db, keep, W_MIX), F32)],
        grid=(1,),
        in_specs=[
            pl.BlockSpec((TAIL_ROWS, W_MIX), lambda i: (tb, vcol)),
            pl.BlockSpec((TAIL_ROWS, W_MIX), lambda i: (tb, gcol)),
            pl.BlockSpec((db, HALO, W_MIX), full3),
            pl.BlockSpec((CONV_W, W_MIX), full2),
            pl.BlockSpec((1, W_MIX), full2),
            pl.BlockSpec((1, W_MIX), full2),
            pl.BlockSpec((1, W_MIX), full2),
            pl.BlockSpec(memory_space=pl.ANY),
        ],
        out_specs=[pl.BlockSpec((TAIL_ROWS, W_MIX), lambda i: (tb, 1)),
                   pl.BlockSpec((db, keep, W_MIX), full3)],
        scratch_shapes=[pltpu.VMEM((HALO + N_META, W_MIX), F32)],
        input_output_aliases={7: 0},
        compiler_params=_cparams(("arbitrary",)),
        name="conv_tail",
    )(proj, proj, state_pad, w, row(bias), row(g), row(beta), mix)


def _group_ref(cb, c, lvl):
    grp, half = 2 << lvl, 1 << lvl
    if grp >= 2 * SUBLANES:
        x3 = cb.reshape(c // grp, grp, LANES)
        ref = x3[:, half - 1:half, :]
        return jnp.broadcast_to(ref, (c // grp, grp, LANES)).reshape(c, LANES)
    x3 = cb.reshape(c // SUBLANES, SUBLANES, LANES)
    sub = lax.broadcasted_iota(jnp.int32, x3.shape, 1)
    out = None
    for g0 in range(0, SUBLANES, grp):
        ref = jnp.broadcast_to(x3[:, g0 + half - 1:g0 + half, :], x3.shape)
        out = ref if out is None else jnp.where(sub >= g0, ref, out)
    return out.reshape(c, LANES)


def _pad_rows(a):
    c = a.shape[0]
    if c % LANES == 0:
        return a
    return jnp.concatenate([a, jnp.zeros((LANES - c % LANES, a.shape[1]), a.dtype)], axis=0)


def _hgrn_chunk(hq, hf, hi, lb, st, c, valid, precise=False):
    if precise:
        mm_nt = lambda x, y: _dot3_nt(_split2(x), _split2(y))
        mm = lambda x, y: _dot3(_split2(x), _split2(y))
    else:
        mm_nt = lambda x, y: _dot_nt(x.astype(BF16), y.astype(BF16))
        mm = lambda x, y: _dot(x.astype(BF16), y.astype(BF16))
    f = lb + (1.0 - lb) * _sigmoid(hf)
    logf = jnp.log(f)
    k = 1.0 - f
    if valid is not None:
        logf = jnp.where(valid, logf, 0.0)
        k = jnp.where(valid, k, 0.0)
    row = lax.broadcasted_iota(jnp.int32, (c, LANES), 0)
    cb = logf
    sh = 1
    while sh < c:
        cb = cb + jnp.where(row >= sh, pltpu.roll(cb, sh, axis=0), 0.0)
        sh *= 2
    rows = lax.broadcasted_iota(jnp.int32, (c, c), 0)
    cols = lax.broadcasted_iota(jnp.int32, (c, c), 1)
    a = jnp.where(rows == cols, mm_nt(hq, k), 0.0)
    lvl = 0
    while (1 << lvl) < c:
        ref = _group_ref(cb, c, lvl)
        right = ((row >> lvl) & 1) == 1
        qe = jnp.where(right, hq * jnp.exp(jnp.minimum(cb - ref, 0.0)), 0.0)
        ke = jnp.where(right, 0.0, k * jnp.exp(jnp.minimum(ref - cb, 0.0)))
        a = a + jnp.where((rows >> (lvl + 1)) == (cols >> (lvl + 1)), mm_nt(qe, ke), 0.0)
        lvl += 1
    o = mm(a, hi) + mm_nt(hq * jnp.exp(cb), st)
    last = cb[c - 1:c, :]
    st_new = st * jnp.exp(last) + mm(jnp.transpose(_pad_rows(hi)), _pad_rows(k * jnp.exp(last - cb)))
    return o, st_new


def _hgrn_lb(lbl_ref):
    lg = lbl_ref[...]
    m = jnp.max(lg, axis=0, keepdims=True)
    e = jnp.exp(lg - m)
    sm = e / jnp.sum(e, axis=0, keepdims=True)
    return (sm[0:1] + sm[1:2]) - sm[0:1]


def _hgrn_out(o, g, hg):
    return _rms(o, g) * (hg * _sigmoid(hg))


def _hgrn_prompt_kernel(hq_ref, hf_ref, hi_ref, hg_ref, lbl_ref, g_ref, st0_ref, _mix_in, o_ref, s_ref, st_scr, *, c):
    ci = pl.program_id(2)

    @pl.when(ci == 0)
    def _():
        st_scr[...] = st0_ref[0]

    o, st = _hgrn_chunk(hq_ref[...], hf_ref[...], hi_ref[...], _hgrn_lb(lbl_ref), st_scr[...], c, None)
    st_scr[...] = st
    o_ref[...] = _hgrn_out(o, g_ref[...], hg_ref[...])

    @pl.when(ci == pl.num_programs(2) - 1)
    def _():
        s_ref[0, 0] = jnp.transpose(st)


def _hgrn_cols():
    base = 3 * W_MIX // HEAD_DIM
    return base, base + N_HEADS, base + 2 * N_HEADS, base + 3 * N_HEADS


def _hgrn_prompt(mix, proj, lb_logits, g, st_meta, b, s, c):
    r = proj.shape[0]
    nc = s // c
    qc, fc, ic, gc = _hgrn_cols()
    blk = lambda col: pl.BlockSpec((c, HEAD_DIM), lambda bb, h, ci: (bb * nc + ci, col + h))
    return pl.pallas_call(
        functools.partial(_hgrn_prompt_kernel, c=c),
        out_shape=[jax.ShapeDtypeStruct((r, 2 * W_MIX), F32),
                   jax.ShapeDtypeStruct((b, N_HEADS, HEAD_DIM, HEAD_DIM), F32)],
        grid=(b, N_HEADS, nc),
        in_specs=[
            blk(qc), blk(fc), blk(ic), blk(gc),
            pl.BlockSpec((2, HEAD_DIM), lambda bb, h, ci: (0, h)),
            pl.BlockSpec((1, HEAD_DIM), lambda bb, h, ci: (0, 0)),
            pl.BlockSpec((1, HEAD_DIM, HEAD_DIM), lambda bb, h, ci: (h, 0, 0)),
            pl.BlockSpec(memory_space=pl.ANY),
        ],
        out_specs=[pl.BlockSpec((c, HEAD_DIM), lambda bb, h, ci: (bb * nc + ci, N_HEADS + h)),
                   pl.BlockSpec((1, 1, HEAD_DIM, HEAD_DIM), lambda bb, h, ci: (bb, h, 0, 0))],
        scratch_shapes=[pltpu.VMEM((HEAD_DIM, HEAD_DIM), F32)],
        input_output_aliases={7: 0},
        compiler_params=_cparams(("parallel", "parallel", "arbitrary")),
        name="hgrn_prompt",
    )(proj, proj, proj, proj, lb_logits, g.reshape(1, HEAD_DIM), st_meta, mix)


def _hgrn_tail_kernel(hq_ref, hf_ref, hi_ref, hg_ref, lbl_ref, g_ref, s0_ref, _mix_in,
                      o_ref, stm_ref, sn_ref, st_scr, *, db, ds):
    j = pl.program_id(1)
    c = 2 * SAMPLE_ROWS
    row = lax.broadcasted_iota(jnp.int32, (c, HEAD_DIM), 0)
    lb = _hgrn_lb(lbl_ref)

    @pl.when(j == 0)
    def _():
        o, st = _hgrn_chunk(hq_ref[...], hf_ref[...], hi_ref[...], lb, jnp.zeros((HEAD_DIM, HEAD_DIM), F32), c, None)
        stm_ref[0] = st
        o_ref[...] = _hgrn_out(o, g_ref[...], hg_ref[...])

    @pl.when((j > 0) & (j <= db // 2))
    def _():
        outs = []
        for half in range(2):
            valid = (row >= half * SAMPLE_ROWS) & (row < half * SAMPLE_ROWS + ds)
            st0 = jnp.transpose(s0_ref[half, 0])
            o, st = _hgrn_chunk(hq_ref[...], hf_ref[...], hi_ref[...], lb, st0, c, valid, precise=True)
            sn_ref[half, 0] = jnp.transpose(st)
            outs.append(jnp.where(valid, o, 0.0))
        o_ref[...] = _hgrn_out(outs[0] + outs[1], g_ref[...], hg_ref[...])

    @pl.when(j > db // 2)
    def _():
        o_ref[...] = jnp.zeros_like(o_ref)


def _hgrn_tail(mix, proj, lb_logits, g, state, b, s, ds):
    r = proj.shape[0]
    db = state.shape[0]
    assert db % 2 == 0
    c = 2 * SAMPLE_ROWS
    row_blk0 = (b * s) // c
    n_chunks = TAIL_ROWS // c
    qc, fc, ic, gc = _hgrn_cols()
    blk = lambda col: pl.BlockSpec((c, HEAD_DIM), lambda h, j: (row_blk0 + j, col + h))
    pair = lambda h, j: (jnp.clip(j - 1, 0, db // 2 - 1), h, 0, 0)
    return pl.pallas_call(
        functools.partial(_hgrn_tail_kernel, db=db, ds=ds),
        out_shape=[jax.ShapeDtypeStruct((r, 2 * W_MIX), F32),
                   jax.ShapeDtypeStruct((N_HEADS, HEAD_DIM, HEAD_DIM), F32),
                   jax.ShapeDtypeStruct((db, N_HEADS, HEAD_DIM, HEAD_DIM), F32)],
        grid=(N_HEADS, n_chunks),
        in_specs=[
            blk(qc), blk(fc), blk(ic), blk(gc),
            pl.BlockSpec((2, HEAD_DIM), lambda h, j: (0, h)),
            pl.BlockSpec((1, HEAD_DIM), lambda h, j: (0, 0)),
            pl.BlockSpec((2, 1, HEAD_DIM, HEAD_DIM), pair),
            pl.BlockSpec(memory_space=pl.ANY),
        ],
        out_specs=[pl.BlockSpec((c, HEAD_DIM), lambda h, j: (row_blk0 + j, N_HEADS + h)),
                   pl.BlockSpec((1, HEAD_DIM, HEAD_DIM), lambda h, j: (h, 0, 0)),
                   pl.BlockSpec((2, 1, HEAD_DIM, HEAD_DIM), pair)],
        scratch_shapes=[pltpu.VMEM((HEAD_DIM, HEAD_DIM), F32)],
        input_output_aliases={7: 0},
        compiler_params=_cparams(("arbitrary", "arbitrary")),
        name="hgrn_tail",
    )(proj, proj, proj, proj, lb_logits, g.reshape(1, HEAD_DIM), state, mix)


def _router_kernel(h_ref, g_ref, w_ref, eid_ref, gate_ref, a_ref):
    a = _rms(h_ref[...], g_ref[...])
    a_ref[...] = a.astype(BF16)
    logits = jnp.dot(a, w_ref[...], preferred_element_type=F32, precision=lax.Precision.HIGHEST)
    lane = lax.broadcasted_iota(jnp.int32, logits.shape, 1)
    logits = jnp.where(lane < N_EXPERTS, logits, -jnp.inf)
    m1 = jnp.max(logits, axis=-1, keepdims=True)
    i1 = jnp.min(jnp.where(logits == m1, lane, LANES), axis=-1, keepdims=True)
    rest = jnp.where(lane == i1, -jnp.inf, logits)
    m2 = jnp.max(rest, axis=-1, keepdims=True)
    i2 = jnp.min(jnp.where(rest == m2, lane, LANES), axis=-1, keepdims=True)
    e2 = jnp.exp(m2 - m1)
    g1 = 1.0 / (1.0 + e2)
    g2 = e2 / (1.0 + e2)
    eid_ref[...] = jnp.where(lane == 0, i1, jnp.where(lane == 1, i2, 0))
    gate_ref[...] = jnp.where(lane == 0, g1, jnp.where(lane == 1, g2, 0.0))


def _router(h, g, w_router, tm):
    r, d = h.shape
    w_pad = jnp.pad(w_router, ((0, 0), (0, LANES - N_EXPERTS)))
    eid, gate, a = pl.pallas_call(
        _router_kernel,
        out_shape=[jax.ShapeDtypeStruct((r, LANES), jnp.int32), jax.ShapeDtypeStruct((r, LANES), F32),
                   jax.ShapeDtypeStruct((r, d), BF16)],
        grid=(r // tm,),
        in_specs=[
            pl.BlockSpec((tm, d), lambda i: (i, 0)),
            pl.BlockSpec((1, d), lambda i: (0, 0)),
            pl.BlockSpec((d, LANES), lambda i: (0, 0)),
        ],
        out_specs=[pl.BlockSpec((tm, LANES), lambda i: (i, 0)), pl.BlockSpec((tm, LANES), lambda i: (i, 0)),
                   pl.BlockSpec((tm, d), lambda i: (i, 0))],
        compiler_params=_cparams(("parallel",)),
        name="moe_router",
    )(h, g.reshape(1, d), w_pad)
    return eid, gate, a


def _moe_permute_kernel(tile_ref, blk_ref, flag_ref, tok_ref, a_ref, o_ref, *, sblk):
    s = pl.program_id(0)
    del tile_ref
    flag = flag_ref[s]

    @pl.when((flag & 1) == 1)
    def _():
        o_ref[...] = jnp.zeros_like(o_ref)

    @pl.when((flag & 2) == 2)
    def _():
        col = lax.broadcasted_iota(jnp.int32, (o_ref.shape[0], sblk), 1)
        onehot = jnp.where(tok_ref[...] - blk_ref[s] * sblk == col, 1.0, 0.0).astype(BF16)
        o_ref[...] += _dot(onehot, a_ref[...]).astype(BF16)


def _moe_permute(a, slot_tok, tm, sblk):
    r, d = a.shape
    cap = slot_tok.shape[0]
    n_tiles, n_blk = cap // tm, r // sblk
    tok_tiles = slot_tok.reshape(n_tiles, tm)
    tok_max = jnp.max(tok_tiles, axis=1)
    tok_min = jnp.min(jnp.where(tok_tiles >= 0, tok_tiles, jnp.int32(2 ** 30)), axis=1)
    used = tok_max >= 0
    first = jnp.where(used, tok_min // sblk, 0)
    n_steps_tile = jnp.where(used, tok_max // sblk - first + 1, 1)
    ends = jnp.cumsum(n_steps_tile)
    max_steps = n_tiles + N_EXPERTS * (n_blk - 1)
    s = jnp.arange(max_steps, dtype=jnp.int32)
    s_used = jnp.minimum(s, ends[-1] - 1)
    tile = jnp.sum((ends[None, :] <= s_used[:, None]).astype(jnp.int32), axis=1)
    k = s_used - (ends - n_steps_tile)[tile]
    blk = (first[tile] + k).astype(jnp.int32)
    live = s < ends[-1]
    flag = (jnp.where(live & (k == 0), 1, 0) + jnp.where(live & used[tile], 2, 0)).astype(jnp.int32)
    grid_spec = pltpu.PrefetchScalarGridSpec(
        num_scalar_prefetch=3,
        grid=(max_steps,),
        in_specs=[
            pl.BlockSpec((tm, 1), lambda s, tl, bk, fl: (tl[s], 0)),
            pl.BlockSpec((sblk, d), lambda s, tl, bk, fl: (bk[s], 0)),
        ],
        out_specs=pl.BlockSpec((tm, d), lambda s, tl, bk, fl: (tl[s], 0)),
    )
    return pl.pallas_call(
        functools.partial(_moe_permute_kernel, sblk=sblk),
        out_shape=jax.ShapeDtypeStruct((cap, d), BF16),
        grid_spec=grid_spec,
        compiler_params=_cparams(("arbitrary",)),
        name="moe_permute",
    )(tile.astype(jnp.int32), blk, flag, slot_tok.reshape(cap, 1), a)


MOE_GROUP = 2


def _moe_ffn_kernel(ge_ref, g0_ref, gr_ref, x_ref, sg_ref, wg_ref, wu_ref, wd_ref, y_ref, acc_scr, sem, *, tm, ch):
    i = pl.program_id(0)
    f = pl.program_id(1)
    del ge_ref
    rows_valid = gr_ref[i]
    last = f == pl.num_programs(1) - 1
    wg = wg_ref[0].astype(BF16)
    wu = wu_ref[0].astype(BF16)
    wd = wd_ref[0].astype(BF16)
    gm, d = acc_scr.shape

    def out_copy(g):
        return pltpu.make_async_copy(acc_scr.at[pl.ds(g * tm, tm)], y_ref.at[pl.ds((g0_ref[i] + g) * tm, tm)],
                                     sem.at[g])

    def work(rows, n):
        a = x_ref[rows, :]
        gate = _dot(a, wg)
        up = _dot(a, wu)
        t = (gate * _sigmoid(gate)) * up

        @pl.when(f == 0)
        def _():
            acc_scr[rows, :] = jnp.zeros((n, d), F32)

        acc_scr[rows, :] += _dot(t.astype(BF16), wd)

        @pl.when(last)
        def _():
            acc_scr[rows, :] = acc_scr[rows, :] * sg_ref[rows, :]

    for c in range(gm // ch):
        rows = slice(c * ch, (c + 1) * ch)

        @pl.when(c * ch < rows_valid)
        def _():
            work(rows, ch)

        @pl.when((c * ch >= rows_valid) & last)
        def _():
            acc_scr[rows, :] = jnp.zeros((ch, d), F32)

    @pl.when(last)
    def _():
        for g in range(MOE_GROUP):
            @pl.when(g * tm < rows_valid)
            def _():
                out_copy(g).start()
        for g in range(MOE_GROUP):
            @pl.when(g * tm < rows_valid)
            def _():
                out_copy(g).wait()


def _moe_ffn(x_sorted, slot_gate, wg, wu, wd, group_expert, group_first, group_rows, n_groups, tm, tf):
    rows, d = x_sorted.shape
    cap = rows - (MOE_GROUP - 1) * tm
    ff = wg.shape[2]
    nf = ff // tf
    gm = MOE_GROUP * tm
    ch = tm
    fidx = lambda i, f, gn: jnp.where(gn[i] > 0, f, nf - 1)
    grid_spec = pltpu.PrefetchScalarGridSpec(
        num_scalar_prefetch=3,
        grid=(n_groups, nf),
        in_specs=[
            pl.BlockSpec((pl.Element(gm), pl.Element(d)), lambda i, f, ge, g0, gn: (g0[i] * tm, 0),
                         pipeline_mode=pl.Buffered(1)),
            pl.BlockSpec((pl.Element(gm), pl.Element(1)), lambda i, f, ge, g0, gn: (g0[i] * tm, 0),
                         pipeline_mode=pl.Buffered(1)),
            pl.BlockSpec((1, d, tf), lambda i, f, ge, g0, gn: (ge[i], 0, fidx(i, f, gn))),
            pl.BlockSpec((1, d, tf), lambda i, f, ge, g0, gn: (ge[i], 0, fidx(i, f, gn))),
            pl.BlockSpec((1, tf, d), lambda i, f, ge, g0, gn: (ge[i], fidx(i, f, gn), 0)),
        ],
        out_specs=pl.BlockSpec(memory_space=pl.ANY),
        scratch_shapes=[pltpu.VMEM((gm, d), F32), pltpu.SemaphoreType.DMA((MOE_GROUP,))],
    )
    return pl.pallas_call(
        functools.partial(_moe_ffn_kernel, tm=tm, ch=ch),
        out_shape=jax.ShapeDtypeStruct((cap, d), F32),
        grid_spec=grid_spec,
        compiler_params=_cparams(("arbitrary", "arbitrary")),
        name="moe_ffn",
    )(group_expert, group_first, group_rows, x_sorted, slot_gate.reshape(rows, 1), wg, wu, wd)


def _moe_combine_kernel(ws_ref, h_ref, eid_ref, dest_ref, g_ref, *refs, n_main, win):
    y_refs = refs[:N_EXPERTS]
    yp_ref, yt_ref = refs[N_EXPERTS:]
    j = pl.program_id(0)
    eid = eid_ref[...]
    dest = dest_ref[...]
    col = lax.broadcasted_iota(jnp.int32, (TAIL_ROWS, win), 1)
    acc = h_ref[...]
    for e in range(N_EXPERTS):
        start = ws_ref[j * N_EXPERTS + e] * SUBLANES
        onehot = jnp.zeros((TAIL_ROWS, win), BF16)
        for k in range(TOP_K):
            rel = jnp.where(eid[:, k:k + 1] == e, dest[:, k:k + 1] - start, -1)
            onehot = onehot + jnp.where(rel == col, 1.0, 0.0).astype(BF16)
        hi, lo = _split2(y_refs[e][...])
        acc = acc + _dot(onehot, hi) + _dot(onehot, lo)
    y = _rms(acc, g_ref[...])

    @pl.when(j < n_main)
    def _():
        yp_ref[...] = y

    @pl.when(j == n_main)
    def _():
        yt_ref[...] = y


def _moe_combine(h, y_sorted, eid, dest, win_start, g, n_main_rows, win):
    r, d = h.shape
    t = TAIL_ROWS
    n_main = n_main_rows // t
    window = lambda e: pl.BlockSpec((pl.Element(win), pl.Element(d)),
                                    lambda j, ws: (ws[j * N_EXPERTS + e] * SUBLANES, 0))
    grid_spec = pltpu.PrefetchScalarGridSpec(
        num_scalar_prefetch=1,
        grid=(r // t,),
        in_specs=[
            pl.BlockSpec((t, d), lambda j, ws: (j, 0)),
            pl.BlockSpec((t, LANES), lambda j, ws: (j, 0)),
            pl.BlockSpec((t, LANES), lambda j, ws: (j, 0)),
            pl.BlockSpec((1, d), lambda j, ws: (0, 0)),
        ] + [window(e) for e in range(N_EXPERTS)],
        out_specs=[pl.BlockSpec((t, d), lambda j, ws: (jnp.minimum(j, n_main - 1), 0)),
                   pl.BlockSpec((t, d), lambda j, ws: (0, 0))],
    )
    return pl.pallas_call(
        functools.partial(_moe_combine_kernel, n_main=n_main, win=win),
        out_shape=[jax.ShapeDtypeStruct((n_main_rows, d), F32), jax.ShapeDtypeStruct((t, d), F32)],
        grid_spec=grid_spec,
        compiler_params=_cparams(("arbitrary",)),
        name="moe_combine_norm",
    )(win_start, h, eid, dest, g.reshape(1, d), *([y_sorted] * N_EXPERTS))


def _moe(h, g_ffn, w_router, wg, wu, wd, g_final, n_main_rows, tm):
    r, d = h.shape
    eid_pad, gate_pad, a = _router(h, g_ffn, w_router, _pick_tile(r, 1024, LANES))
    eid, gate = eid_pad[:, :TOP_K], gate_pad[:, :TOP_K]
    n_slots = r * TOP_K
    big = jnp.int32(2 ** 30)
    e_flat = eid.reshape(n_slots)
    onehot = (e_flat[:, None] == jnp.arange(N_EXPERTS, dtype=jnp.int32)[None, :]).astype(jnp.int32)
    rank = jnp.take_along_axis(jnp.cumsum(onehot, axis=0), e_flat[:, None], axis=1)[:, 0] - 1
    counts = jnp.sum(onehot, axis=0)
    padded = (counts + tm - 1) // tm * tm
    pends = jnp.cumsum(padded)
    pstarts = pends - padded
    dest = (pstarts[e_flat] + rank).astype(jnp.int32)
    n_tiles = (n_slots + N_EXPERTS * (tm - 1)) // tm
    cap = n_tiles * tm
    extra = (MOE_GROUP - 1) * tm
    tok_f = (jnp.arange(n_slots, dtype=jnp.int32) // TOP_K).astype(F32)
    empty = jnp.broadcast_to(jnp.array([-1.0, 0.0], F32), (cap + extra, 2))
    slot_tab = empty.at[dest].set(jnp.stack([tok_f, gate.reshape(n_slots)], axis=1))
    slot_tok = slot_tab[:, 0].astype(jnp.int32)
    slot_gate = slot_tab[:, 1]
    tiles_e = padded // tm
    groups_e = (tiles_e + MOE_GROUP - 1) // MOE_GROUP
    gends = jnp.cumsum(groups_e)
    n_groups = n_tiles // MOE_GROUP + N_EXPERTS
    gi = jnp.arange(n_groups, dtype=jnp.int32)
    gi_used = jnp.minimum(gi, gends[-1] - 1)
    group_expert = jnp.minimum(
        jnp.sum((gends[None, :] <= gi_used[:, None]).astype(jnp.int32), axis=1), N_EXPERTS - 1)
    within = gi_used - (gends - groups_e)[group_expert]
    group_first = ((pstarts // tm)[group_expert] + within * MOE_GROUP).astype(jnp.int32)
    group_rows = jnp.where(gi < gends[-1],
                           jnp.clip(counts[group_expert] - within * MOE_GROUP * tm, 0, MOE_GROUP * tm),
                           0).astype(jnp.int32)
    win = TAIL_ROWS + SUBLANES
    n_tok_tiles = r // TAIL_ROWS
    dest_e = jnp.where(onehot == 1, dest[:, None], big).reshape(n_tok_tiles, TAIL_ROWS * TOP_K, N_EXPERTS)
    first_row = jnp.min(dest_e, axis=1)
    win_start = jnp.clip(first_row // SUBLANES, 0, (pends[-1] - win) // SUBLANES).reshape(-1).astype(jnp.int32)

    x_sorted = _moe_permute(a, slot_tok, tm, tm)
    y_sorted = _moe_ffn(x_sorted, slot_gate, wg, wu, wd, group_expert.astype(jnp.int32), group_first, group_rows,
                        n_groups, tm, _pick_tile(wg.shape[2], 512, LANES))
    dest_pad = jnp.pad(dest.reshape(r, TOP_K), ((0, 0), (0, LANES - TOP_K)))
    return _moe_combine(h, y_sorted, eid_pad, dest_pad, win_start, g_final, n_main_rows, win)


def kernel(x_prompt, x_sample, cache_k0, cache_v0, state_conv0, cache_k1, cache_v1, state_hgrn1, page_table, meta_tokens, norm_mix, norm_ffn, norm_final, w_in0, lambda_q1, lambda_k1, lambda_q2, lambda_k2, subln_g0, conv_w0, conv_b0, conv_ln_g0, conv_ln_b0, w_out0, ffn_gate0, ffn_up0, ffn_down0, w_in1, hgrn_lb_logits, hgrn_norm_g1, w_out1, router1, moe_gate1, moe_up1, moe_down1):
    b, s, d = x_prompt.shape
    db, ds, _ = x_sample.shape
    n_pages = page_table.shape[1]
    page = cache_k0.shape[1]
    assert d == 2 * W_MIX and ds <= SAMPLE_ROWS and N_META + SAMPLE_ROWS * db <= TAIL_ROWS
    assert s % 256 == 0
    bs = b * s
    r = bs + TAIL_ROWS
    tm_proj = _pick_tile(r, 1664, 16)
    tm_out = _pick_tile(r, 1040, 16)
    tm_ffn = _pick_tile(r, 832, 16)
    tm_moe = _pick_tile(r, 768, LANES)
    tq = 512 if s % 512 == 0 else 256

    xs = jnp.pad(x_sample, ((0, 0), (0, SAMPLE_ROWS - ds), (0, 0))).reshape(db * SAMPLE_ROWS, d)
    tail = jnp.concatenate(
        [meta_tokens.astype(F32), xs, jnp.zeros((TAIL_ROWS - N_META - db * SAMPLE_ROWS, d), F32)], axis=0)
    x_flat = jnp.concatenate([x_prompt.reshape(bs, d), tail], axis=0)

    pos_tail = jnp.concatenate([
        jnp.arange(N_META, dtype=jnp.int32),
        jnp.tile(n_pages * page + jnp.arange(SAMPLE_ROWS, dtype=jnp.int32), db),
        jnp.zeros((TAIL_ROWS - N_META - db * SAMPLE_ROWS,), jnp.int32)])
    tables = _rope_tables(jnp.concatenate([N_META + jnp.arange(s, dtype=jnp.int32), pos_tail]))
    lamp = jnp.stack([lambda_q1, lambda_k1, lambda_q2, lambda_k2]).astype(F32)
    g_sub = subln_g0.reshape(1, HEAD_DIM)

    def sample_rows(x):
        return x[N_META:N_META + db * SAMPLE_ROWS].reshape(db, SAMPLE_ROWS, N_HEADS, HEAD_DIM)[:, :ds]

    proj0 = _norm_matmul(x_flat, norm_mix[0], w_in0, tm_proj, 512)
    k0p, v0p, k0_tail = _kv_layout(proj0, b, s, tables)
    extra0 = (tables[0], tables[1], lamp, g_sub)
    mix0 = _attn_prompt(proj0, k0p, v0p, "diff", 0, b, s, tq, extra0)
    mix0 = _attn_tail(mix0, proj0, k0_tail, (0, 0), cache_k0, cache_v0, page_table, "diff", 0, b, s, ds, extra0)
    mix0, conv0_p = _conv_prompt(mix0, proj0, conv_w0, conv_b0, conv_ln_g0, conv_ln_b0, b, s, 128)
    mix0, conv0_s = _conv_tail(mix0, proj0, state_conv0, conv_w0, conv_b0, conv_ln_g0, conv_ln_b0, b, s, ds)
    h = _matmul_residual(mix0, w_out0, x_flat, tm_out, 512)
    h = _ffn(h, norm_ffn[0], ffn_gate0, ffn_up0, ffn_down0, tm_ffn, _pick_tile(ffn_gate0.shape[1], 512, LANES))

    proj1 = _norm_matmul(h, norm_mix[1], w_in1, tm_proj, 512)
    k1p, v1p = _kv_layout(proj1, b, s, None)
    tb = bs // TAIL_ROWS
    mix1 = _attn_prompt(proj1, k1p, v1p, "sb", 1, b, s, tq, None)
    mix1 = _attn_tail(mix1, proj1, proj1, (tb, 1), cache_k1, cache_v1, page_table, "sb", 1, b, s, ds, None)
    mix1, st_meta, hgrn_s = _hgrn_tail(mix1, proj1, hgrn_lb_logits, hgrn_norm_g1, state_hgrn1, b, s, ds)
    mix1, hgrn_p = _hgrn_prompt(mix1, proj1, hgrn_lb_logits, hgrn_norm_g1, st_meta, b, s, 256)
    h = _matmul_residual(mix1, w_out1, h, tm_out, 512)
    y_main, y_tail = _moe(h, norm_ffn[1], router1, moe_gate1, moe_up1, moe_down1, norm_final, bs, tm_moe)

    y_prompt = y_main.reshape(b, s, d)
    y_sample = y_tail[N_META:N_META + db * SAMPLE_ROWS].reshape(db, SAMPLE_ROWS, d)[:, :ds]
    heads = lambda x: x.reshape(b, N_META + s, N_HEADS, HEAD_DIM)
    tail0 = proj0[bs:]
    tail1 = proj1[bs:]
    return (y_prompt, y_sample,
            heads(k0p), heads(v0p), conv0_p, heads(k1p), heads(v1p), hgrn_p,
            sample_rows(k0_tail), sample_rows(tail0[:, 2 * W_MIX:3 * W_MIX]), conv0_s,
            sample_rows(tail1[:, W_MIX:2 * W_MIX]), sample_rows(tail1[:, 2 * W_MIX:3 * W_MIX]), hgrn_s)
```

```python
import functools
import math

import jax
import jax.numpy as jnp
from jax import lax
from jax.experimental import pallas as pl
from jax.experimental.pallas import tpu as pltpu

F32 = jnp.float32
BF16 = jnp.bfloat16

N_META = 16
EPS = 1e-6
ROPE_THETA = 10000.0
N_HEADS = 8
HEAD_DIM = 128
W_MIX = N_HEADS * HEAD_DIM
DK_A = 64
CONV_W = 31
N_EXPERTS = 8
TOP_K = 2
HALO = 32

LANES = 128
SUBLANES = 8
TAIL_ROWS = 128
SAMPLE_ROWS = 8
V7X_VMEM_BYTES = 64 * 1024 * 1024
VMEM_LIMIT_BYTES = 56 * 1024 * 1024
PAGES_PER_STEP = 8


def _lambda_init(layer):
    return 0.8 - 0.6 * math.exp(-0.3 * layer)


def _cparams(semantics, vmem=VMEM_LIMIT_BYTES):
    return pltpu.CompilerParams(dimension_semantics=semantics, vmem_limit_bytes=vmem)


def _pick_tile(n, cap, mult):
    best = None
    for t in range(mult, min(n, cap) + 1, mult):
        if n % t == 0:
            best = t
    assert best is not None, (n, cap, mult)
    return best


def _dot(a, b):
    return jnp.dot(a, b, preferred_element_type=F32)


def _dot_nt(a, b):
    return lax.dot_general(a, b, (((1,), (1,)), ((), ())), preferred_element_type=F32)


def _sigmoid(x):
    return 1.0 / (1.0 + jnp.exp(-x))


def _log_sigmoid(x):
    return jnp.minimum(x, 0.0) - jnp.log(1.0 + jnp.exp(-jnp.abs(x)))


def _rms(x, g):
    return x * lax.rsqrt(jnp.mean(x * x, axis=-1, keepdims=True) + EPS) * g


def _split2(x):
    hi = x.astype(BF16)
    lo = (x - hi.astype(F32)).astype(BF16)
    return hi, lo


def _dot3(a, b):
    return _dot(a[0], b[0]) + _dot(a[1], b[0]) + _dot(a[0], b[1])


def _dot3_nt(a, b):
    return _dot_nt(a[0], b[0]) + _dot_nt(a[1], b[0]) + _dot_nt(a[0], b[1])


def _mm3(a, w):
    return _dot3(a, _split2(w))


def _is_last_row_tile():
    return pl.program_id(0) == pl.num_programs(0) - 1


def _lhs_store(a_scr, t_scr, a):
    a_scr[...] = a.astype(BF16)

    @pl.when(_is_last_row_tile())
    def _():
        hi, lo = _split2(a[a.shape[0] - TAIL_ROWS:])
        t_scr[0] = hi
        t_scr[1] = lo


def _proj_kernel(x_ref, g_ref, w_ref, o_ref, a_scr, t_scr):
    @pl.when(pl.program_id(1) == 0)
    def _():
        _lhs_store(a_scr, t_scr, _rms(x_ref[...], g_ref[...]))

    o_ref[...] = _dot(a_scr[...], w_ref[...].astype(BF16))

    @pl.when(_is_last_row_tile())
    def _():
        tm = o_ref.shape[0]
        o_ref[tm - TAIL_ROWS:tm, :] = _mm3((t_scr[0], t_scr[1]), w_ref[...])


def _outproj_kernel(x_ref, w_ref, r_ref, o_ref, a_scr, t_scr):
    @pl.when(pl.program_id(1) == 0)
    def _():
        _lhs_store(a_scr, t_scr, x_ref[...])

    o_ref[...] = r_ref[...] + _dot(a_scr[...], w_ref[...].astype(BF16))

    @pl.when(_is_last_row_tile())
    def _():
        tm = o_ref.shape[0]
        o_ref[tm - TAIL_ROWS:tm, :] = r_ref[tm - TAIL_ROWS:tm, :] + _mm3((t_scr[0], t_scr[1]), w_ref[...])


def _ffn_kernel(h_ref, g_ref, wg_ref, wu_ref, wd_ref, o_ref, a_scr, t_scr, tacc_scr):
    f = pl.program_id(1)
    tm = o_ref.shape[0]

    @pl.when(f == 0)
    def _():
        h = h_ref[...]
        _lhs_store(a_scr, t_scr, _rms(h, g_ref[...]))
        o_ref[...] = h

    a = a_scr[...]
    gate = _dot(a, wg_ref[...].astype(BF16))
    up = _dot(a, wu_ref[...].astype(BF16))
    t = (gate * _sigmoid(gate)) * up
    o_ref[...] += _dot(t.astype(BF16), wd_ref[...].astype(BF16))

    @pl.when(_is_last_row_tile())
    def _():
        @pl.when(f == 0)
        def _():
            tacc_scr[...] = h_ref[tm - TAIL_ROWS:tm, :]

        ta = (t_scr[0], t_scr[1])
        tgate = _mm3(ta, wg_ref[...])
        tup = _mm3(ta, wu_ref[...])
        tacc_scr[...] += _mm3(_split2((tgate * _sigmoid(tgate)) * tup), wd_ref[...])

        @pl.when(f == pl.num_programs(1) - 1)
        def _():
            o_ref[tm - TAIL_ROWS:tm, :] = tacc_scr[...]


def _dense_call(kernel_fn, name, x, consts, weights, res, out_width, out_tile, tm, n_steps, accumulates):
    single_rows = accumulates
    r, d = x.shape
    assert r % tm == 0 and tm >= TAIL_ROWS
    if out_tile == out_width:
        out_map = lambda i, j: (i, 0)
    else:
        out_map = lambda i, j: (i, j)
    row_mode = dict(pipeline_mode=pl.Buffered(1)) if single_rows else {}
    in_specs = [pl.BlockSpec((tm, d), lambda i, j: (i, 0), **row_mode)]
    args = [x]
    for arr in consts:
        in_specs.append(pl.BlockSpec(arr.shape, lambda i, j: (0, 0)))
        args.append(arr)
    for arr, shape, imap in weights:
        in_specs.append(pl.BlockSpec(shape, lambda i, j, imap=imap: imap(j)))
        args.append(arr)
    if res is not None:
        in_specs.append(pl.BlockSpec((tm, out_tile), out_map))
        args.append(res)
    return pl.pallas_call(
        kernel_fn,
        out_shape=jax.ShapeDtypeStruct((r, out_width), F32),
        grid=(r // tm, n_steps),
        in_specs=in_specs,
        out_specs=pl.BlockSpec((tm, out_tile), out_map, **(row_mode if out_tile == out_width else {})),
        scratch_shapes=[pltpu.VMEM((tm, d), BF16), pltpu.VMEM((2, TAIL_ROWS, d), BF16)] + (
            [pltpu.VMEM((TAIL_ROWS, d), F32)] if accumulates else []),
        compiler_params=_cparams(("arbitrary", "arbitrary")),
        name=name,
    )(*args)


def _norm_matmul(x, g, w, tm, tn):
    d, n = w.shape
    return _dense_call(_proj_kernel, "norm_matmul", x, [g.reshape(1, d)], [(w, (d, tn), lambda j: (0, j))],
                       None, n, tn, tm, n // tn, False)


def _matmul_residual(x, w, res, tm, tn):
    d, n = w.shape
    return _dense_call(_outproj_kernel, "matmul_residual", x, [], [(w, (d, tn), lambda j: (0, j))],
                       res, n, tn, tm, n // tn, False)


def _ffn(h, g, wg, wu, wd, tm, tf):
    d, ff = wg.shape
    weights = [(wg, (d, tf), lambda j: (0, j)), (wu, (d, tf), lambda j: (0, j)), (wd, (tf, d), lambda j: (j, 0))]
    return _dense_call(_ffn_kernel, "ffn_swiglu", h, [g.reshape(1, d)], weights, None, d, d, tm, ff // tf, True)


def _rope_tables(pos):
    half = DK_A // 2
    inv = ROPE_THETA ** (-jnp.arange(half, dtype=F32) * 2.0 / DK_A)
    ang = pos.astype(F32)[:, None] * inv[None, :]
    cos = jnp.cos(ang)
    sin = jnp.sin(ang)
    cos_t = jnp.concatenate([cos, cos, cos, cos], axis=1)
    sin_t = jnp.concatenate([-sin, sin, -sin, sin], axis=1)
    return cos_t, sin_t


def _rope(x, cos_t, sin_t):
    half = DK_A // 2
    lane = lax.broadcasted_iota(jnp.int32, x.shape, 1)
    first = (lane % DK_A) < half
    partner = jnp.where(first, pltpu.roll(x, LANES - half, axis=1), pltpu.roll(x, half, axis=1))
    return x * cos_t + partner * sin_t


def _kv_layout_kernel(*refs, rope, s):
    if rope:
        (k_ref, kt_ref, v_ref, vt_ref, cos_ref, sin_ref, cost_ref, sint_ref, kp_ref, vp_ref, ktr_ref) = refs
        k = _rope(k_ref[...], cos_ref[...], sin_ref[...])
        kt = _rope(kt_ref[...], cost_ref[...], sint_ref[...])
        ktr_ref[...] = kt
    else:
        (k_ref, kt_ref, v_ref, vt_ref, kp_ref, vp_ref) = refs
        k = k_ref[...]
        kt = kt_ref[...]
    kp_ref[0, 0:N_META, :] = kt[0:N_META]
    kp_ref[0, N_META:N_META + s, :] = k
    vp_ref[0, 0:N_META, :] = vt_ref[0:N_META, :]
    vp_ref[0, N_META:N_META + s, :] = v_ref[...]


def _kv_layout(proj, b, s, tables):
    tb = (b * s) // TAIL_ROWS
    kcol, vcol = W_MIX // HEAD_DIM, 2 * W_MIX // HEAD_DIM
    rope = tables is not None
    in_specs = [
        pl.BlockSpec((s, HEAD_DIM), lambda h, bb: (bb, kcol + h)),
        pl.BlockSpec((TAIL_ROWS, HEAD_DIM), lambda h, bb: (tb, kcol + h)),
        pl.BlockSpec((s, HEAD_DIM), lambda h, bb: (bb, vcol + h)),
        pl.BlockSpec((TAIL_ROWS, HEAD_DIM), lambda h, bb: (tb, vcol + h)),
    ]
    args = [proj, proj, proj, proj]
    out_shape = [jax.ShapeDtypeStruct((b, N_META + s, W_MIX), F32)] * 2
    out_specs = [pl.BlockSpec((1, N_META + s, HEAD_DIM), lambda h, bb: (bb, 0, h))] * 2
    if rope:
        cos_t, sin_t = tables
        in_specs += [
            pl.BlockSpec((s, LANES), lambda h, bb: (0, 0)),
            pl.BlockSpec((s, LANES), lambda h, bb: (0, 0)),
            pl.BlockSpec((TAIL_ROWS, LANES), lambda h, bb: (s // TAIL_ROWS, 0)),
            pl.BlockSpec((TAIL_ROWS, LANES), lambda h, bb: (s // TAIL_ROWS, 0)),
        ]
        args += [cos_t, sin_t, cos_t, sin_t]
        out_shape = out_shape + [jax.ShapeDtypeStruct((TAIL_ROWS, W_MIX), F32)]
        out_specs = out_specs + [pl.BlockSpec((TAIL_ROWS, HEAD_DIM), lambda h, bb: (0, h))]
    return pl.pallas_call(
        functools.partial(_kv_layout_kernel, rope=rope, s=s),
        out_shape=out_shape,
        grid=(N_HEADS, b),
        in_specs=in_specs,
        out_specs=out_specs,
        compiler_params=_cparams(("arbitrary", "arbitrary")),
        name="kv_layout_rope" if rope else "kv_layout",
    )(*args)


def _diff_lambda(lamp_ref, layer):
    lp = lamp_ref[...]
    d1 = jnp.sum(lp[0:1] * lp[1:2], axis=-1, keepdims=True)
    d2 = jnp.sum(lp[2:3] * lp[3:4], axis=-1, keepdims=True)
    return jnp.exp(d1) - jnp.exp(d2) + _lambda_init(layer)


def _diff_queries_f32(q):
    lane = lax.broadcasted_iota(jnp.int32, q.shape, 1)
    qs = q * (DK_A ** -0.5)
    q1 = jnp.where(lane < DK_A, qs, 0.0)
    q2 = jnp.where(lane >= DK_A, qs, 0.0)
    return jnp.concatenate([q1, q2], axis=0)


def _diff_queries(q):
    return _diff_queries_f32(q).astype(BF16)


def _diff_update(carry, qs, kc, vc, mask2):
    m, l, acc = carry
    sc = _dot_nt(qs, kc.astype(BF16))
    if mask2 is not None:
        sc = jnp.where(mask2, sc, -jnp.inf)
    m_new = jnp.maximum(m, jnp.max(sc, axis=-1, keepdims=True))
    alpha = jnp.exp(m - m_new)
    p = jnp.exp(sc - m_new)
    l = alpha * l + jnp.sum(p, axis=-1, keepdims=True)
    acc = alpha * acc + _dot(p.astype(BF16), vc.astype(BF16))
    return m_new, l, acc


def _diff_finish(carry, n, lam, g, layer):
    _, l, acc = carry
    o = acc[:n] / l[:n] - lam * (acc[n:] / l[n:])
    return _rms(o, g) * (1.0 - _lambda_init(layer))


def _upper_ones(n):
    j = lax.broadcasted_iota(jnp.int32, (n, n), 0)
    s = lax.broadcasted_iota(jnp.int32, (n, n), 1)
    return jnp.where(j > s, 1.0, 0.0).astype(BF16)


def _later_ones(n):
    s = lax.broadcasted_iota(jnp.int32, (n, n), 0)
    j = lax.broadcasted_iota(jnp.int32, (n, n), 1)
    return jnp.where(j > s, 1.0, 0.0).astype(BF16)


SB_SCAN = 256


def _sb_update(carry, qb, kc, vc, mask, u):
    c, acc = carry
    n = kc.shape[0]
    z = _dot_nt(qb, kc.astype(BF16)) * (HEAD_DIM ** -0.5)
    ls = _log_sigmoid(z)
    l1m = ls - z
    if mask is not None:
        l1m = jnp.where(mask, l1m, 0.0)
    blk = u.shape[0]
    parts = [None] * (n // blk)
    for i in reversed(range(n // blk)):
        sl = slice(blk * i, blk * (i + 1))
        hi, lo = _split2(l1m[:, sl])
        parts[i] = jnp.exp(ls[:, sl] + (_dot(hi, u) + _dot(lo, u)) + c)
        c = c + jnp.sum(l1m[:, sl], axis=-1, keepdims=True)
    a = parts[0] if len(parts) == 1 else jnp.concatenate(parts, axis=1)
    if mask is not None:
        a = jnp.where(mask, a, 0.0)
    acc = acc + _dot(a.astype(BF16), vc.astype(BF16))
    return c, acc


def _attn_prompt_kernel(*refs, mode, tq, layer):
    if mode == "diff":
        q_ref, k_ref, v_ref, cos_ref, sin_ref, lamp_ref, g_ref, o_ref = refs
    else:
        q_ref, k_ref, v_ref, o_ref = refs
    i = pl.program_id(2)
    rows = lax.broadcasted_iota(jnp.int32, (tq, tq), 0)
    cols = lax.broadcasted_iota(jnp.int32, (tq, tq), 1)

    def chunk(j):
        start = pl.multiple_of(N_META + j * tq, SUBLANES)
        return k_ref[0, pl.ds(start, tq), :], v_ref[0, pl.ds(start, tq), :]

    k_meta = k_ref[0, 0:N_META, :]
    v_meta = v_ref[0, 0:N_META, :]

    if mode == "diff":
        qs = _diff_queries(_rope(q_ref[...], cos_ref[...], sin_ref[...]))
        carry = (jnp.full((2 * tq, 1), -jnp.inf, F32), jnp.zeros((2 * tq, 1), F32),
                 jnp.zeros((2 * tq, HEAD_DIM), F32))
        carry = _diff_update(carry, qs, k_meta, v_meta, None)

        def body(j, c):
            kc, vc = chunk(j)
            return _diff_update(c, qs, kc, vc, None)

        carry = lax.fori_loop(0, i, body, carry)
        kc, vc = chunk(i)
        causal = cols <= rows
        carry = _diff_update(carry, qs, kc, vc, jnp.concatenate([causal, causal], axis=0))
        o_ref[...] = _diff_finish(carry, tq, _diff_lambda(lamp_ref, layer), g_ref[...], layer)
    else:
        qb = q_ref[...].astype(BF16)
        u = _upper_ones(min(tq, SB_SCAN))
        carry = (jnp.zeros((tq, 1), F32), jnp.zeros((tq, HEAD_DIM), F32))
        kc, vc = chunk(i)
        carry = _sb_update(carry, qb, kc, vc, cols < rows, u)

        def body(it, c):
            kc, vc = chunk(i - 1 - it)
            return _sb_update(c, qb, kc, vc, None, u)

        carry = lax.fori_loop(0, i, body, carry)
        carry = _sb_update(carry, qb, k_meta, v_meta, None, _upper_ones(N_META))
        o_ref[...] = carry[1]


def _attn_prompt(proj, kp, vp, mode, layer, b, s, tq, extra):
    r = proj.shape[0]
    nq = s // tq
    in_specs = [
        pl.BlockSpec((tq, HEAD_DIM), lambda bb, h, i: (bb * nq + i, h)),
        pl.BlockSpec((1, N_META + s, HEAD_DIM), lambda bb, h, i: (bb, 0, h)),
        pl.BlockSpec((1, N_META + s, HEAD_DIM), lambda bb, h, i: (bb, 0, h)),
    ]
    args = [proj, kp, vp]
    if mode == "diff":
        cos_t, sin_t, lamp, g = extra
        in_specs += [
            pl.BlockSpec((tq, LANES), lambda bb, h, i: (i, 0)),
            pl.BlockSpec((tq, LANES), lambda bb, h, i: (i, 0)),
            pl.BlockSpec(lamp.shape, lambda bb, h, i: (0, 0)),
            pl.BlockSpec((1, HEAD_DIM), lambda bb, h, i: (0, 0)),
        ]
        args += [cos_t, sin_t, lamp, g]
    return pl.pallas_call(
        functools.partial(_attn_prompt_kernel, mode=mode, tq=tq, layer=layer),
        out_shape=jax.ShapeDtypeStruct((r, 2 * W_MIX), F32),
        grid=(b, N_HEADS, nq),
        in_specs=in_specs,
        out_specs=pl.BlockSpec((tq, HEAD_DIM), lambda bb, h, i: (bb * nq + i, h)),
        compiler_params=_cparams(("parallel", "parallel", "arbitrary")),
        name="attn_prompt_" + mode,
    )(*args)


def _tail_columns(mode, ds):
    if mode == "diff":
        rq = 4 if ds <= 4 else SAMPLE_ROWS
        return rq, 2 * rq
    return SAMPLE_ROWS, SAMPLE_ROWS


def _attn_tail_kernel(pt_ref, *refs, mode, layer, ds, n_steps):
    del pt_ref
    pps = PAGES_PER_STEP
    if mode == "diff":
        (q_ref, k_ref, v_ref, cos_ref, sin_ref, lamp_ref, g_ref) = refs[:7]
        refs = refs[7:]
    else:
        (q_ref, k_ref, v_ref) = refs[:3]
        refs = refs[3:]
    kc_refs = refs[:pps]
    vc_refs = refs[pps:2 * pps]
    _mix_in, o_ref, q_scr, m_scr, l_scr, acc_scr = refs[2 * pps:]
    b = pl.program_id(0)
    st = pl.program_id(1)
    row0 = pl.multiple_of(N_META + SAMPLE_ROWS * b, SUBLANES)

    def head(x, h):
        return x[:, HEAD_DIM * h:HEAD_DIM * (h + 1)]

    @pl.when((b == 0) & (st == 0))
    def _():
        o_ref[...] = jnp.zeros_like(o_ref)
        rows = lax.broadcasted_iota(jnp.int32, (N_META, N_META), 0)
        cols = lax.broadcasted_iota(jnp.int32, (N_META, N_META), 1)
        qm = q_ref[0:N_META, :]
        km = k_ref[0:N_META, :]
        vm = v_ref[0:N_META, :]
        for h in range(N_HEADS):
            if mode == "diff":
                qs = _diff_queries(_rope(head(qm, h), cos_ref[0:N_META, :], sin_ref[0:N_META, :]))
                causal = cols <= rows
                carry = (jnp.full((2 * N_META, 1), -jnp.inf, F32), jnp.zeros((2 * N_META, 1), F32),
                         jnp.zeros((2 * N_META, HEAD_DIM), F32))
                carry = _diff_update(carry, qs, head(km, h), head(vm, h),
                                     jnp.concatenate([causal, causal], axis=0))
                o = _diff_finish(carry, N_META, _diff_lambda(lamp_ref, layer), g_ref[...], layer)
            else:
                carry = (jnp.zeros((N_META, 1), F32), jnp.zeros((N_META, HEAD_DIM), F32))
                carry = _sb_update(carry, head(qm, h).astype(BF16), head(km, h), head(vm, h),
                                   cols < rows, _upper_ones(N_META))
                o = carry[1]
            o_ref[0:N_META, HEAD_DIM * h:HEAD_DIM * (h + 1)] = o

    rq, cph = _tail_columns(mode, ds)
    nc = N_HEADS * cph

    def scores(kflat):
        k_hi, k_lo = _split2(kflat)
        r = _dot_nt(q_scr[...], k_hi)
        return r[:nc] + r[nc:] + _dot_nt(q_scr[0:nc, :], k_lo)

    def weighted_values(p, vflat):
        p_hi, p_lo = _split2(p)
        v_hi, v_lo = _split2(vflat)
        r = _dot(jnp.concatenate([p_hi, p_lo], axis=0), jnp.concatenate([v_hi, v_lo], axis=1))
        return r[:nc, :HEAD_DIM] + r[:nc, HEAD_DIM:] + r[nc:, :HEAD_DIM]

    def cumsum_and_total(l1m):
        hi, lo = _split2(l1m)
        ones = jnp.ones((LANES, LANES), BF16)
        r = _dot(jnp.concatenate([hi, lo], axis=0), jnp.concatenate([_upper_ones(LANES), ones], axis=1))
        return r[:nc, :LANES] + r[nc:, :LANES], r[:nc, LANES:] + r[nc:, LANES:]

    def flat(ref):
        return ref[0].reshape(LANES * N_HEADS, HEAD_DIM)

    col = lax.broadcasted_iota(jnp.int32, (nc, LANES), 0)
    lane = lax.broadcasted_iota(jnp.int32, (nc, LANES), 1)
    page_lane = lax.broadcasted_iota(jnp.int32, (nc, LANES * N_HEADS), 1)
    page_col = lax.broadcasted_iota(jnp.int32, (nc, LANES * N_HEADS), 0)
    own_head = (page_lane % N_HEADS) == (page_col // cph)

    @pl.when(st == 0)
    def _():
        q8 = q_ref[pl.ds(row0, SAMPLE_ROWS), :]
        blocks = []
        for h in range(N_HEADS):
            qh = head(q8, h)
            if mode == "diff":
                qh = _diff_queries_f32(_rope(qh, cos_ref[pl.ds(row0, SAMPLE_ROWS), :],
                                             sin_ref[pl.ds(row0, SAMPLE_ROWS), :]))
                qh = jnp.concatenate([qh[0:rq], qh[SAMPLE_ROWS:SAMPLE_ROWS + rq]], axis=0)
            blocks.append(qh)
        q_hi, q_lo = _split2(jnp.concatenate(blocks, axis=0))
        q_scr[0:nc, :] = q_hi
        q_scr[nc:2 * nc, :] = q_lo

        k8 = k_ref[pl.ds(row0, SAMPLE_ROWS), :]
        v8 = v_ref[pl.ds(row0, SAMPLE_ROWS), :]
        pad = jnp.zeros((LANES - SAMPLE_ROWS * N_HEADS, HEAD_DIM), F32)
        k_new = jnp.concatenate([head(k8, h) for h in range(N_HEADS)] + [pad], axis=0)
        v_new = jnp.concatenate([head(v8, h) for h in range(N_HEADS)] + [pad], axis=0)
        tok = lane % SAMPLE_ROWS
        qrow = col % rq
        ok = (lane // SAMPLE_ROWS == col // cph) & (tok < ds)
        sc = scores(k_new)
        if mode == "diff":
            sc = jnp.where(ok & (tok <= qrow), sc, -jnp.inf)
            m = jnp.max(sc, axis=1, keepdims=True)
            p = jnp.exp(sc - m)
            m_scr[...] = jnp.broadcast_to(m, (nc, LANES))
            l_scr[...] = jnp.broadcast_to(jnp.sum(p, axis=1, keepdims=True), (nc, LANES))
        else:
            ok = ok & (tok < qrow)
            z = sc * (HEAD_DIM ** -0.5)
            ls = _log_sigmoid(z)
            between, total = cumsum_and_total(jnp.where(ok, ls - z, 0.0))
            p = jnp.where(ok, jnp.exp(ls + between), 0.0)
            m_scr[...] = total
            l_scr[...] = jnp.zeros_like(l_scr)
        acc_scr[...] = weighted_values(p, v_new)

    if mode == "diff":
        m = m_scr[:, 0:1]
        l = l_scr[:, 0:1]
        acc = acc_scr[...]
        for j in range(pps):
            sc = jnp.where(own_head, scores(flat(kc_refs[j])), -jnp.inf)
            m_new = jnp.maximum(m, jnp.max(sc, axis=1, keepdims=True))
            alpha = jnp.exp(m - m_new)
            p = jnp.exp(sc - m_new)
            l = alpha * l + jnp.sum(p, axis=1, keepdims=True)
            acc = alpha * acc + weighted_values(p, flat(vc_refs[j]))
            m = m_new
        m_scr[...] = jnp.broadcast_to(m, (nc, LANES))
        l_scr[...] = jnp.broadcast_to(l, (nc, LANES))
        acc_scr[...] = acc
    else:
        running = m_scr[...]
        acc = acc_scr[...]
        n_blk = N_HEADS
        for j in reversed(range(pps)):
            z = scores(flat(kc_refs[j])) * (HEAD_DIM ** -0.5)
            ls = _log_sigmoid(z)
            l1m = jnp.where(own_head, ls - z, 0.0)
            ps = [None] * n_blk
            for i in reversed(range(n_blk)):
                blk = slice(LANES * i, LANES * (i + 1))
                between, total = cumsum_and_total(l1m[:, blk])
                ps[i] = jnp.where(own_head[:, blk], jnp.exp(ls[:, blk] + between + running), 0.0)
                running = running + total
            acc = acc + weighted_values(jnp.concatenate(ps, axis=1), flat(vc_refs[j]))
        m_scr[...] = running
        acc_scr[...] = acc

    @pl.when(st == n_steps - 1)
    def _():
        acc = acc_scr[...]
        real = lax.broadcasted_iota(jnp.int32, (SAMPLE_ROWS, HEAD_DIM), 0) < ds
        outs = []
        if mode == "diff":
            acc = acc / l_scr[:, 0:1]
            lam = _diff_lambda(lamp_ref, layer)
            for h in range(N_HEADS):
                o = acc[cph * h:cph * h + rq] - lam * acc[cph * h + rq:cph * h + 2 * rq]
                o = _rms(o, g_ref[...]) * (1.0 - _lambda_init(layer))
                if rq < SAMPLE_ROWS:
                    o = jnp.concatenate([o, jnp.zeros((SAMPLE_ROWS - rq, HEAD_DIM), F32)], axis=0)
                outs.append(jnp.where(real, o, 0.0))
        else:
            for h in range(N_HEADS):
                outs.append(jnp.where(real, acc[cph * h:cph * (h + 1)], 0.0))
        o_ref[pl.ds(row0, SAMPLE_ROWS), :] = jnp.concatenate(outs, axis=1)


def _attn_tail(mix, proj, k_tail, k_tail_col, cache_k, cache_v, page_table, mode, layer, b, s, ds, extra):
    r = proj.shape[0]
    tb = (b * s) // TAIL_ROWS
    db, n_pages = page_table.shape
    pps = PAGES_PER_STEP
    assert n_pages % pps == 0
    n_steps = n_pages // pps
    page = cache_k.shape[1]
    assert page == LANES and cache_k.shape[2:] == (N_HEADS, HEAD_DIM)
    page_block = (1, page, N_HEADS, HEAD_DIM)
    vcol = 2
    nc = N_HEADS * _tail_columns(mode, ds)[1]

    in_specs = [
        pl.BlockSpec((TAIL_ROWS, W_MIX), lambda bb, st, pt: (tb, 0)),
        pl.BlockSpec((TAIL_ROWS, W_MIX), lambda bb, st, pt: k_tail_col),
        pl.BlockSpec((TAIL_ROWS, W_MIX), lambda bb, st, pt: (tb, vcol)),
    ]
    args = [proj, k_tail, proj]
    if mode == "diff":
        cos_t, sin_t, lamp, g = extra
        in_specs += [
            pl.BlockSpec((TAIL_ROWS, LANES), lambda bb, st, pt: (s // TAIL_ROWS, 0)),
            pl.BlockSpec((TAIL_ROWS, LANES), lambda bb, st, pt: (s // TAIL_ROWS, 0)),
            pl.BlockSpec(lamp.shape, lambda bb, st, pt: (0, 0)),
            pl.BlockSpec((1, HEAD_DIM), lambda bb, st, pt: (0, 0)),
        ]
        args += [cos_t, sin_t, lamp, g]

    def page_spec(j):
        if mode == "diff":
            return pl.BlockSpec(page_block, lambda bb, st, pt: (pt[bb, st * pps + j], 0, 0, 0))
        return pl.BlockSpec(page_block, lambda bb, st, pt: (pt[bb, n_pages - (st + 1) * pps + j], 0, 0, 0))

    in_specs += [page_spec(j) for j in range(pps)] + [page_spec(j) for j in range(pps)]
    args += [cache_k] * pps + [cache_v] * pps
    in_specs.append(pl.BlockSpec(memory_space=pl.ANY))
    args.append(mix)
    n_in = len(args)

    grid_spec = pltpu.PrefetchScalarGridSpec(
        num_scalar_prefetch=1,
        grid=(db, n_steps),
        in_specs=in_specs,
        out_specs=pl.BlockSpec((TAIL_ROWS, W_MIX), lambda bb, st, pt: (tb, 0)),
        scratch_shapes=[
            pltpu.VMEM((2 * nc, HEAD_DIM), BF16),
            pltpu.VMEM((nc, LANES), F32),
            pltpu.VMEM((nc, LANES), F32),
            pltpu.VMEM((nc, HEAD_DIM), F32),
        ],
    )
    return pl.pallas_call(
        functools.partial(_attn_tail_kernel, mode=mode, layer=layer, ds=ds, n_steps=n_steps),
        out_shape=jax.ShapeDtypeStruct((r, 2 * W_MIX), F32),
        grid_spec=grid_spec,
        input_output_aliases={n_in: 0},
        compiler_params=_cparams(("arbitrary", "arbitrary")),
        name="attn_tail_" + mode,
    )(page_table, *args)


def _glu(val, gate):
    return val * _sigmoid(gate)


def _dwconv_rows(buf_ref, w_ref, n_rows, row_strip):
    off = HALO - (CONV_W - 1)
    strips = []
    for r0 in range(0, n_rows, row_strip):
        cols = []
        for c0 in range(0, W_MIX, LANES):
            acc = jnp.zeros((row_strip, LANES), F32)
            for k in range(CONV_W):
                acc = acc + w_ref[k:k + 1, c0:c0 + LANES] * buf_ref[r0 + off + k:r0 + off + k + row_strip, c0:c0 + LANES]
            cols.append(acc)
        strips.append(jnp.concatenate(cols, axis=1))
    return strips[0] if len(strips) == 1 else jnp.concatenate(strips, axis=0)


def _ln_silu(y, g, b):
    yc = y - jnp.mean(y, axis=-1, keepdims=True)
    yn = yc * lax.rsqrt(jnp.mean(yc * yc, axis=-1, keepdims=True) + EPS) * g + b
    return yn * _sigmoid(yn)


def _conv_prompt_kernel(val_ref, gate_ref, tval_ref, tgate_ref, w_ref, b_ref, g_ref, beta_ref, _mix_in,
                        o_ref, cp_ref, buf_scr, *, t):
    i = pl.program_id(1)

    @pl.when(i == 0)
    def _():
        buf_scr[0:HALO - N_META, :] = jnp.zeros((HALO - N_META, W_MIX), F32)
        buf_scr[HALO - N_META:HALO, :] = _glu(tval_ref[0:N_META, :], tgate_ref[0:N_META, :])

    @pl.when(i > 0)
    def _():
        buf_scr[0:HALO, :] = buf_scr[t:t + HALO, :]

    buf_scr[HALO:HALO + t, :] = _glu(val_ref[...], gate_ref[...])
    y = _dwconv_rows(buf_scr, w_ref, t, 64) + b_ref[...]
    o_ref[...] = _ln_silu(y, g_ref[...], beta_ref[...])

    @pl.when(i == pl.num_programs(1) - 1)
    def _():
        cp_ref[0] = buf_scr[HALO + t - (CONV_W - 1):HALO + t, :]


def _conv_prompt(mix, proj, w, bias, g, beta, b, s, t):
    r = proj.shape[0]
    tb = (b * s) // TAIL_ROWS
    nb = s // t
    vcol, gcol = 3, 4
    row = lambda x: x.reshape(1, W_MIX)
    full = lambda bb, i: (0, 0)
    return pl.pallas_call(
        functools.partial(_conv_prompt_kernel, t=t),
        out_shape=[jax.ShapeDtypeStruct((r, 2 * W_MIX), F32),
                   jax.ShapeDtypeStruct((b, CONV_W - 1, W_MIX), F32)],
        grid=(b, nb),
        in_specs=[
            pl.BlockSpec((t, W_MIX), lambda bb, i: (bb * nb + i, vcol)),
            pl.BlockSpec((t, W_MIX), lambda bb, i: (bb * nb + i, gcol)),
            pl.BlockSpec((TAIL_ROWS, W_MIX), lambda bb, i: (tb, vcol)),
            pl.BlockSpec((TAIL_ROWS, W_MIX), lambda bb, i: (tb, gcol)),
            pl.BlockSpec((CONV_W, W_MIX), full),
            pl.BlockSpec((1, W_MIX), full),
            pl.BlockSpec((1, W_MIX), full),
            pl.BlockSpec((1, W_MIX), full),
            pl.BlockSpec(memory_space=pl.ANY),
        ],
        out_specs=[pl.BlockSpec((t, W_MIX), lambda bb, i: (bb * nb + i, 1)),
                   pl.BlockSpec((1, CONV_W - 1, W_MIX), lambda bb, i: (bb, 0, 0))],
        scratch_shapes=[pltpu.VMEM((HALO + t, W_MIX), F32)],
        input_output_aliases={8: 0},
        compiler_params=_cparams(("parallel", "arbitrary")),
        name="conv_prompt",
    )(proj, proj, proj, proj, w, row(bias), row(g), row(beta), mix)


def _conv_tail_kernel(tval_ref, tgate_ref, st_ref, w_ref, b_ref, g_ref, beta_ref, _mix_in,
                      o_ref, cs_ref, buf_scr, *, db, ds):
    u = _glu(tval_ref[...], tgate_ref[...])
    o_ref[...] = jnp.zeros_like(o_ref)
    keep = CONV_W - 1

    def finish(n_rows):
        y = _dwconv_rows(buf_scr, w_ref, n_rows, n_rows) + b_ref[...]
        return _ln_silu(y, g_ref[...], beta_ref[...])

    buf_scr[0:HALO, :] = jnp.zeros((HALO, W_MIX), F32)
    buf_scr[HALO:HALO + N_META, :] = u[0:N_META]
    o_ref[0:N_META, :] = finish(N_META)
    real = lax.broadcasted_iota(jnp.int32, (SAMPLE_ROWS, W_MIX), 0) < ds
    for b in range(db):
        r0 = N_META + SAMPLE_ROWS * b
        buf_scr[0:HALO, :] = st_ref[b]
        buf_scr[HALO:HALO + SAMPLE_ROWS, :] = u[r0:r0 + SAMPLE_ROWS]
        o_ref[r0:r0 + SAMPLE_ROWS, :] = jnp.where(real, finish(SAMPLE_ROWS), 0.0)
        cs_ref[b] = buf_scr[HALO + ds - keep:HALO + ds, :]


def _conv_tail(mix, proj, state, w, bias, g, beta, b, s, ds):
    r = proj.shape[0]
    tb = (b * s) // TAIL_ROWS
    db = state.shape[0]
    keep = CONV_W - 1
    vcol, gcol = 3, 4
    state_pad = jnp.pad(state, ((0, 0), (HALO - keep, 0), (0, 0)))
    row = lambda x: x.reshape(1, W_MIX)
    full2 = lambda i: (0, 0)
    full3 = lambda i: (0, 0, 0)
    return pl.pallas_call(
        functools.partial(_conv_tail_kernel, db=db, ds=ds),
        out_shape=[jax.ShapeDtypeStruct((r, 2 * W_MIX), F32),
                   jax.ShapeDtypeStruct((db, keep, W_MIX), F32)],
        grid=(1,),
        in_specs=[
            pl.BlockSpec((TAIL_ROWS, W_MIX), lambda i: (tb, vcol)),
            pl.BlockSpec((TAIL_ROWS, W_MIX), lambda i: (tb, gcol)),
            pl.BlockSpec((db, HALO, W_MIX), full3),
            pl.BlockSpec((CONV_W, W_MIX), full2),
            pl.BlockSpec((1, W_MIX), full2),
            pl.BlockSpec((1, W_MIX), full2),
            pl.BlockSpec((1, W_MIX), full2),
            pl.BlockSpec(memory_space=pl.ANY),
        ],
        out_specs=[pl.BlockSpec((TAIL_ROWS, W_MIX), lambda i: (tb, 1)),
                   pl.BlockSpec((db, keep, W_MIX), full3)],
        scratch_shapes=[pltpu.VMEM((HALO + N_META, W_MIX), F32)],
        input_output_aliases={7: 0},
        compiler_params=_cparams(("arbitrary",)),
        name="conv_tail",
    )(proj, proj, state_pad, w, row(bias), row(g), row(beta), mix)


def _group_ref(cb, c, lvl):
    grp, half = 2 << lvl, 1 << lvl
    if grp >= 2 * SUBLANES:
        x3 = cb.reshape(c // grp, grp, LANES)
        ref = x3[:, half - 1:half, :]
        return jnp.broadcast_to(ref, (c // grp, grp, LANES)).reshape(c, LANES)
    x3 = cb.reshape(c // SUBLANES, SUBLANES, LANES)
    sub = lax.broadcasted_iota(jnp.int32, x3.shape, 1)
    out = None
    for g0 in range(0, SUBLANES, grp):
        ref = jnp.broadcast_to(x3[:, g0 + half - 1:g0 + half, :], x3.shape)
        out = ref if out is None else jnp.where(sub >= g0, ref, out)
    return out.reshape(c, LANES)


def _pad_rows(a):
    c = a.shape[0]
    if c % LANES == 0:
        return a
    return jnp.concatenate([a, jnp.zeros((LANES - c % LANES, a.shape[1]), a.dtype)], axis=0)


def _hgrn_chunk(hq, hf, hi, lb, st, c, valid, precise=False):
    if precise:
        mm_nt = lambda x, y: _dot3_nt(_split2(x), _split2(y))
        mm = lambda x, y: _dot3(_split2(x), _split2(y))
    else:
        mm_nt = lambda x, y: _dot_nt(x.astype(BF16), y.astype(BF16))
        mm = lambda x, y: _dot(x.astype(BF16), y.astype(BF16))
    f = lb + (1.0 - lb) * _sigmoid(hf)
    logf = jnp.log(f)
    k = 1.0 - f
    if valid is not None:
        logf = jnp.where(valid, logf, 0.0)
        k = jnp.where(valid, k, 0.0)
    row = lax.broadcasted_iota(jnp.int32, (c, LANES), 0)
    cb = logf
    sh = 1
    while sh < c:
        cb = cb + jnp.where(row >= sh, pltpu.roll(cb, sh, axis=0), 0.0)
        sh *= 2
    rows = lax.broadcasted_iota(jnp.int32, (c, c), 0)
    cols = lax.broadcasted_iota(jnp.int32, (c, c), 1)
    a = jnp.where(rows == cols, mm_nt(hq, k), 0.0)
    lvl = 0
    while (1 << lvl) < c:
        ref = _group_ref(cb, c, lvl)
        right = ((row >> lvl) & 1) == 1
        qe = jnp.where(right, hq * jnp.exp(jnp.minimum(cb - ref, 0.0)), 0.0)
        ke = jnp.where(right, 0.0, k * jnp.exp(jnp.minimum(ref - cb, 0.0)))
        a = a + jnp.where((rows >> (lvl + 1)) == (cols >> (lvl + 1)), mm_nt(qe, ke), 0.0)
        lvl += 1
    o = mm(a, hi) + mm_nt(hq * jnp.exp(cb), st)
    last = cb[c - 1:c, :]
    st_new = st * jnp.exp(last) + mm(jnp.transpose(_pad_rows(hi)), _pad_rows(k * jnp.exp(last - cb)))
    return o, st_new


def _hgrn_lb(lbl_ref):
    lg = lbl_ref[...]
    m = jnp.max(lg, axis=0, keepdims=True)
    e = jnp.exp(lg - m)
    sm = e / jnp.sum(e, axis=0, keepdims=True)
    return (sm[0:1] + sm[1:2]) - sm[0:1]


def _hgrn_out(o, g, hg):
    return _rms(o, g) * (hg * _sigmoid(hg))


def _hgrn_prompt_kernel(hq_ref, hf_ref, hi_ref, hg_ref, lbl_ref, g_ref, st0_ref, _mix_in, o_ref, s_ref, st_scr, *, c):
    ci = pl.program_id(2)

    @pl.when(ci == 0)
    def _():
        st_scr[...] = st0_ref[0]

    o, st = _hgrn_chunk(hq_ref[...], hf_ref[...], hi_ref[...], _hgrn_lb(lbl_ref), st_scr[...], c, None)
    st_scr[...] = st
    o_ref[...] = _hgrn_out(o, g_ref[...], hg_ref[...])

    @pl.when(ci == pl.num_programs(2) - 1)
    def _():
        s_ref[0, 0] = jnp.transpose(st)


def _hgrn_cols():
    base = 3 * W_MIX // HEAD_DIM
    return base, base + N_HEADS, base + 2 * N_HEADS, base + 3 * N_HEADS


def _hgrn_prompt(mix, proj, lb_logits, g, st_meta, b, s, c):
    r = proj.shape[0]
    nc = s // c
    qc, fc, ic, gc = _hgrn_cols()
    blk = lambda col: pl.BlockSpec((c, HEAD_DIM), lambda bb, h, ci: (bb * nc + ci, col + h))
    return pl.pallas_call(
        functools.partial(_hgrn_prompt_kernel, c=c),
        out_shape=[jax.ShapeDtypeStruct((r, 2 * W_MIX), F32),
                   jax.ShapeDtypeStruct((b, N_HEADS, HEAD_DIM, HEAD_DIM), F32)],
        grid=(b, N_HEADS, nc),
        in_specs=[
            blk(qc), blk(fc), blk(ic), blk(gc),
            pl.BlockSpec((2, HEAD_DIM), lambda bb, h, ci: (0, h)),
            pl.BlockSpec((1, HEAD_DIM), lambda bb, h, ci: (0, 0)),
            pl.BlockSpec((1, HEAD_DIM, HEAD_DIM), lambda bb, h, ci: (h, 0, 0)),
            pl.BlockSpec(memory_space=pl.ANY),
        ],
        out_specs=[pl.BlockSpec((c, HEAD_DIM), lambda bb, h, ci: (bb * nc + ci, N_HEADS + h)),
                   pl.BlockSpec((1, 1, HEAD_DIM, HEAD_DIM), lambda bb, h, ci: (bb, h, 0, 0))],
        scratch_shapes=[pltpu.VMEM((HEAD_DIM, HEAD_DIM), F32)],
        input_output_aliases={7: 0},
        compiler_params=_cparams(("parallel", "parallel", "arbitrary")),
        name="hgrn_prompt",
    )(proj, proj, proj, proj, lb_logits, g.reshape(1, HEAD_DIM), st_meta, mix)


def _hgrn_tail_kernel(hq_ref, hf_ref, hi_ref, hg_ref, lbl_ref, g_ref, s0_ref, _mix_in,
                      o_ref, stm_ref, sn_ref, st_scr, *, db, ds):
    j = pl.program_id(1)
    c = 2 * SAMPLE_ROWS
    row = lax.broadcasted_iota(jnp.int32, (c, HEAD_DIM), 0)
    lb = _hgrn_lb(lbl_ref)

    @pl.when(j == 0)
    def _():
        o, st = _hgrn_chunk(hq_ref[...], hf_ref[...], hi_ref[...], lb, jnp.zeros((HEAD_DIM, HEAD_DIM), F32), c, None)
        stm_ref[0] = st
        o_ref[...] = _hgrn_out(o, g_ref[...], hg_ref[...])

    @pl.when((j > 0) & (j <= db // 2))
    def _():
        outs = []
        for half in range(2):
            valid = (row >= half * SAMPLE_ROWS) & (row < half * SAMPLE_ROWS + ds)
            st0 = jnp.transpose(s0_ref[half, 0])
            o, st = _hgrn_chunk(hq_ref[...], hf_ref[...], hi_ref[...], lb, st0, c, valid, precise=True)
            sn_ref[half, 0] = jnp.transpose(st)
            outs.append(jnp.where(valid, o, 0.0))
        o_ref[...] = _hgrn_out(outs[0] + outs[1], g_ref[...], hg_ref[...])

    @pl.when(j > db // 2)
    def _():
        o_ref[...] = jnp.zeros_like(o_ref)


def _hgrn_tail(mix, proj, lb_logits, g, state, b, s, ds):
    r = proj.shape[0]
    db = state.shape[0]
    assert db % 2 == 0
    c = 2 * SAMPLE_ROWS
    row_blk0 = (b * s) // c
    n_chunks = TAIL_ROWS // c
    qc, fc, ic, gc = _hgrn_cols()
    blk = lambda col: pl.BlockSpec((c, HEAD_DIM), lambda h, j: (row_blk0 + j, col + h))
    pair = lambda h, j: (jnp.clip(j - 1, 0, db // 2 - 1), h, 0, 0)
    return pl.pallas_call(
        functools.partial(_hgrn_tail_kernel, db=db, ds=ds),
        out_shape=[jax.ShapeDtypeStruct((r, 2 * W_MIX), F32),
                   jax.ShapeDtypeStruct((N_HEADS, HEAD_DIM, HEAD_DIM), F32),
                   jax.ShapeDtypeStruct((db, N_HEADS, HEAD_DIM, HEAD_DIM), F32)],
        grid=(N_HEADS, n_chunks),
        in_specs=[
            blk(qc), blk(fc), blk(ic), blk(gc),
            pl.BlockSpec((2, HEAD_DIM), lambda h, j: (0, h)),
            pl.BlockSpec((1, HEAD_DIM), lambda h, j: (0, 0)),
            pl.BlockSpec((2, 1, HEAD_DIM, HEAD_DIM), pair),
            pl.BlockSpec(memory_space=pl.ANY),
        ],
        out_specs=[pl.BlockSpec((c, HEAD_DIM), lambda h, j: (row_blk0 + j, N_HEADS + h)),
                   pl.BlockSpec((1, HEAD_DIM, HEAD_DIM), lambda h, j: (h, 0, 0)),
                   pl.BlockSpec((2, 1, HEAD_DIM, HEAD_DIM), pair)],
        scratch_shapes=[pltpu.VMEM((HEAD_DIM, HEAD_DIM), F32)],
        input_output_aliases={7: 0},
        compiler_params=_cparams(("arbitrary", "arbitrary")),
        name="hgrn_tail",
    )(proj, proj, proj, proj, lb_logits, g.reshape(1, HEAD_DIM), state, mix)


def _router_kernel(h_ref, g_ref, w_ref, eid_ref, gate_ref, a_ref):
    a = _rms(h_ref[...], g_ref[...])
    a_ref[...] = a.astype(BF16)
    logits = jnp.dot(a, w_ref[...], preferred_element_type=F32, precision=lax.Precision.HIGHEST)
    lane = lax.broadcasted_iota(jnp.int32, logits.shape, 1)
    logits = jnp.where(lane < N_EXPERTS, logits, -jnp.inf)
    m1 = jnp.max(logits, axis=-1, keepdims=True)
    i1 = jnp.min(jnp.where(logits == m1, lane, LANES), axis=-1, keepdims=True)
    rest = jnp.where(lane == i1, -jnp.inf, logits)
    m2 = jnp.max(rest, axis=-1, keepdims=True)
    i2 = jnp.min(jnp.where(rest == m2, lane, LANES), axis=-1, keepdims=True)
    e2 = jnp.exp(m2 - m1)
    g1 = 1.0 / (1.0 + e2)
    g2 = e2 / (1.0 + e2)
    eid_ref[...] = jnp.where(lane == 0, i1, jnp.where(lane == 1, i2, 0))
    gate_ref[...] = jnp.where(lane == 0, g1, jnp.where(lane == 1, g2, 0.0))


def _router(h, g, w_router, tm):
    r, d = h.shape
    w_pad = jnp.pad(w_router, ((0, 0), (0, LANES - N_EXPERTS)))
    eid, gate, a = pl.pallas_call(
        _router_kernel,
        out_shape=[jax.ShapeDtypeStruct((r, LANES), jnp.int32), jax.ShapeDtypeStruct((r, LANES), F32),
                   jax.ShapeDtypeStruct((r, d), BF16)],
        grid=(r // tm,),
        in_specs=[
            pl.BlockSpec((tm, d), lambda i: (i, 0)),
            pl.BlockSpec((1, d), lambda i: (0, 0)),
            pl.BlockSpec((d, LANES), lambda i: (0, 0)),
        ],
        out_specs=[pl.BlockSpec((tm, LANES), lambda i: (i, 0)), pl.BlockSpec((tm, LANES), lambda i: (i, 0)),
                   pl.BlockSpec((tm, d), lambda i: (i, 0))],
        compiler_params=_cparams(("parallel",)),
        name="moe_router",
    )(h, g.reshape(1, d), w_pad)
    return eid, gate, a


def _moe_permute_kernel(tile_ref, blk_ref, flag_ref, tok_ref, a_ref, o_ref, *, sblk):
    s = pl.program_id(0)
    del tile_ref
    flag = flag_ref[s]

    @pl.when((flag & 1) == 1)
    def _():
        o_ref[...] = jnp.zeros_like(o_ref)

    @pl.when((flag & 2) == 2)
    def _():
        col = lax.broadcasted_iota(jnp.int32, (o_ref.shape[0], sblk), 1)
        onehot = jnp.where(tok_ref[...] - blk_ref[s] * sblk == col, 1.0, 0.0).astype(BF16)
        o_ref[...] += _dot(onehot, a_ref[...]).astype(BF16)


def _moe_permute(a, slot_tok, tm, sblk):
    r, d = a.shape
    cap = slot_tok.shape[0]
    n_tiles, n_blk = cap // tm, r // sblk
    tok_tiles = slot_tok.reshape(n_tiles, tm)
    tok_max = jnp.max(tok_tiles, axis=1)
    tok_min = jnp.min(jnp.where(tok_tiles >= 0, tok_tiles, jnp.int32(2 ** 30)), axis=1)
    used = tok_max >= 0
    first = jnp.where(used, tok_min // sblk, 0)
    n_steps_tile = jnp.where(used, tok_max // sblk - first + 1, 1)
    ends = jnp.cumsum(n_steps_tile)
    max_steps = n_tiles + N_EXPERTS * (n_blk - 1)
    s = jnp.arange(max_steps, dtype=jnp.int32)
    s_used = jnp.minimum(s, ends[-1] - 1)
    tile = jnp.sum((ends[None, :] <= s_used[:, None]).astype(jnp.int32), axis=1)
    k = s_used - (ends - n_steps_tile)[tile]
    blk = (first[tile] + k).astype(jnp.int32)
    live = s < ends[-1]
    flag = (jnp.where(live & (k == 0), 1, 0) + jnp.where(live & used[tile], 2, 0)).astype(jnp.int32)
    grid_spec = pltpu.PrefetchScalarGridSpec(
        num_scalar_prefetch=3,
        grid=(max_steps,),
        in_specs=[
            pl.BlockSpec((tm, 1), lambda s, tl, bk, fl: (tl[s], 0)),
            pl.BlockSpec((sblk, d), lambda s, tl, bk, fl: (bk[s], 0)),
        ],
        out_specs=pl.BlockSpec((tm, d), lambda s, tl, bk, fl: (tl[s], 0)),
    )
    return pl.pallas_call(
        functools.partial(_moe_permute_kernel, sblk=sblk),
        out_shape=jax.ShapeDtypeStruct((cap, d), BF16),
        grid_spec=grid_spec,
        compiler_params=_cparams(("arbitrary",)),
        name="moe_permute",
    )(tile.astype(jnp.int32), blk, flag, slot_tok.reshape(cap, 1), a)


MOE_GROUP = 2


def _moe_ffn_kernel(ge_ref, g0_ref, gr_ref, x_ref, sg_ref, wg_ref, wu_ref, wd_ref, y_ref, acc_scr, sem, *, tm, ch):
    i = pl.program_id(0)
    f = pl.program_id(1)
    del ge_ref
    rows_valid = gr_ref[i]
    last = f == pl.num_programs(1) - 1
    wg = wg_ref[0].astype(BF16)
    wu = wu_ref[0].astype(BF16)
    wd = wd_ref[0].astype(BF16)
    gm, d = acc_scr.shape

    def out_copy(g):
        return pltpu.make_async_copy(acc_scr.at[pl.ds(g * tm, tm)], y_ref.at[pl.ds((g0_ref[i] + g) * tm, tm)],
                                     sem.at[g])

    def work(rows, n):
        a = x_ref[rows, :]
        gate = _dot(a, wg)
        up = _dot(a, wu)
        t = (gate * _sigmoid(gate)) * up

        @pl.when(f == 0)
        def _():
            acc_scr[rows, :] = jnp.zeros((n, d), F32)

        acc_scr[rows, :] += _dot(t.astype(BF16), wd)

        @pl.when(last)
        def _():
            acc_scr[rows, :] = acc_scr[rows, :] * sg_ref[rows, :]

    for c in range(gm // ch):
        rows = slice(c * ch, (c + 1) * ch)

        @pl.when(c * ch < rows_valid)
        def _():
            work(rows, ch)

        @pl.when((c * ch >= rows_valid) & last)
        def _():
            acc_scr[rows, :] = jnp.zeros((ch, d), F32)

    @pl.when(last)
    def _():
        for g in range(MOE_GROUP):
            @pl.when(g * tm < rows_valid)
            def _():
                out_copy(g).start()
        for g in range(MOE_GROUP):
            @pl.when(g * tm < rows_valid)
            def _():
                out_copy(g).wait()


def _moe_ffn(x_sorted, slot_gate, wg, wu, wd, group_expert, group_first, group_rows, n_groups, tm, tf):
    rows, d = x_sorted.shape
    cap = rows - (MOE_GROUP - 1) * tm
    ff = wg.shape[2]
    nf = ff // tf
    gm = MOE_GROUP * tm
    ch = tm
    fidx = lambda i, f, gn: jnp.where(gn[i] > 0, f, nf - 1)
    grid_spec = pltpu.PrefetchScalarGridSpec(
        num_scalar_prefetch=3,
        grid=(n_groups, nf),
        in_specs=[
            pl.BlockSpec((pl.Element(gm), pl.Element(d)), lambda i, f, ge, g0, gn: (g0[i] * tm, 0),
                         pipeline_mode=pl.Buffered(1)),
            pl.BlockSpec((pl.Element(gm), pl.Element(1)), lambda i, f, ge, g0, gn: (g0[i] * tm, 0),
                         pipeline_mode=pl.Buffered(1)),
            pl.BlockSpec((1, d, tf), lambda i, f, ge, g0, gn: (ge[i], 0, fidx(i, f, gn))),
            pl.BlockSpec((1, d, tf), lambda i, f, ge, g0, gn: (ge[i], 0, fidx(i, f, gn))),
            pl.BlockSpec((1, tf, d), lambda i, f, ge, g0, gn: (ge[i], fidx(i, f, gn), 0)),
        ],
        out_specs=pl.BlockSpec(memory_space=pl.ANY),
        scratch_shapes=[pltpu.VMEM((gm, d), F32), pltpu.SemaphoreType.DMA((MOE_GROUP,))],
    )
    return pl.pallas_call(
        functools.partial(_moe_ffn_kernel, tm=tm, ch=ch),
        out_shape=jax.ShapeDtypeStruct((cap, d), F32),
        grid_spec=grid_spec,
        compiler_params=_cparams(("arbitrary", "arbitrary")),
        name="moe_ffn",
    )(group_expert, group_first, group_rows, x_sorted, slot_gate.reshape(rows, 1), wg, wu, wd)


def _moe_combine_kernel(ws_ref, h_ref, eid_ref, dest_ref, g_ref, *refs, n_main, win):
    y_refs = refs[:N_EXPERTS]
    yp_ref, yt_ref = refs[N_EXPERTS:]
    j = pl.program_id(0)
    eid = eid_ref[...]
    dest = dest_ref[...]
    col = lax.broadcasted_iota(jnp.int32, (TAIL_ROWS, win), 1)
    acc = h_ref[...]
    for e in range(N_EXPERTS):
        start = ws_ref[j * N_EXPERTS + e] * SUBLANES
        onehot = jnp.zeros((TAIL_ROWS, win), BF16)
        for k in range(TOP_K):
            rel = jnp.where(eid[:, k:k + 1] == e, dest[:, k:k + 1] - start, -1)
            onehot = onehot + jnp.where(rel == col, 1.0, 0.0).astype(BF16)
        hi, lo = _split2(y_refs[e][...])
        acc = acc + _dot(onehot, hi) + _dot(onehot, lo)
    y = _rms(acc, g_ref[...])

    @pl.when(j < n_main)
    def _():
        yp_ref[...] = y

    @pl.when(j == n_main)
    def _():
        yt_ref[...] = y


def _moe_combine(h, y_sorted, eid, dest, win_start, g, n_main_rows, win):
    r, d = h.shape
    t = TAIL_ROWS
    n_main = n_main_rows // t
    window = lambda e: pl.BlockSpec((pl.Element(win), pl.Element(d)),
                                    lambda j, ws: (ws[j * N_EXPERTS + e] * SUBLANES, 0))
    grid_spec = pltpu.PrefetchScalarGridSpec(
        num_scalar_prefetch=1,
        grid=(r // t,),
        in_specs=[
            pl.BlockSpec((t, d), lambda j, ws: (j, 0)),
            pl.BlockSpec((t, LANES), lambda j, ws: (j, 0)),
            pl.BlockSpec((t, LANES), lambda j, ws: (j, 0)),
            pl.BlockSpec((1, d), lambda j, ws: (0, 0)),
        ] + [window(e) for e in range(N_EXPERTS)],
        out_specs=[pl.BlockSpec((t, d), lambda j, ws: (jnp.minimum(j, n_main - 1), 0)),
                   pl.BlockSpec((t, d), lambda j, ws: (0, 0))],
    )
    return pl.pallas_call(
        functools.partial(_moe_combine_kernel, n_main=n_main, win=win),
        out_shape=[jax.ShapeDtypeStruct((n_main_rows, d), F32), jax.ShapeDtypeStruct((t, d), F32)],
        grid_spec=grid_spec,
        compiler_params=_cparams(("arbitrary",)),
        name="moe_combine_norm",
    )(win_start, h, eid, dest, g.reshape(1, d), *([y_sorted] * N_EXPERTS))


def _moe(h, g_ffn, w_router, wg, wu, wd, g_final, n_main_rows, tm):
    r, d = h.shape
    eid_pad, gate_pad, a = _router(h, g_ffn, w_router, _pick_tile(r, 1024, LANES))
    eid, gate = eid_pad[:, :TOP_K], gate_pad[:, :TOP_K]
    n_slots = r * TOP_K
    big = jnp.int32(2 ** 30)
    e_flat = eid.reshape(n_slots)
    onehot = (e_flat[:, None] == jnp.arange(N_EXPERTS, dtype=jnp.int32)[None, :]).astype(jnp.int32)
    rank = jnp.take_along_axis(jnp.cumsum(onehot, axis=0), e_flat[:, None], axis=1)[:, 0] - 1
    counts = jnp.sum(onehot, axis=0)
    padded = (counts + tm - 1) // tm * tm
    pends = jnp.cumsum(padded)
    pstarts = pends - padded
    dest = (pstarts[e_flat] + rank).astype(jnp.int32)
    n_tiles = (n_slots + N_EXPERTS * (tm - 1)) // tm
    cap = n_tiles * tm
    extra = (MOE_GROUP - 1) * tm
    tok_f = (jnp.arange(n_slots, dtype=jnp.int32) // TOP_K).astype(F32)
    empty = jnp.broadcast_to(jnp.array([-1.0, 0.0], F32), (cap + extra, 2))
    slot_tab = empty.at[dest].set(jnp.stack([tok_f, gate.reshape(n_slots)], axis=1))
    slot_tok = slot_tab[:, 0].astype(jnp.int32)
    slot_gate = slot_tab[:, 1]
    tiles_e = padded // tm
    groups_e = (tiles_e + MOE_GROUP - 1) // MOE_GROUP
    gends = jnp.cumsum(groups_e)
    n_groups = n_tiles // MOE_GROUP + N_EXPERTS
    gi = jnp.arange(n_groups, dtype=jnp.int32)
    gi_used = jnp.minimum(gi, gends[-1] - 1)
    group_expert = jnp.minimum(
        jnp.sum((gends[None, :] <= gi_used[:, None]).astype(jnp.int32), axis=1), N_EXPERTS - 1)
    within = gi_used - (gends - groups_e)[group_expert]
    group_first = ((pstarts // tm)[group_expert] + within * MOE_GROUP).astype(jnp.int32)
    group_rows = jnp.where(gi < gends[-1],
                           jnp.clip(counts[group_expert] - within * MOE_GROUP * tm, 0, MOE_GROUP * tm),
                           0).astype(jnp.int32)
    win = TAIL_ROWS + SUBLANES
    n_tok_tiles = r // TAIL_ROWS
    dest_e = jnp.where(onehot == 1, dest[:, None], big).reshape(n_tok_tiles, TAIL_ROWS * TOP_K, N_EXPERTS)
    first_row = jnp.min(dest_e, axis=1)
    win_start = jnp.clip(first_row // SUBLANES, 0, (pends[-1] - win) // SUBLANES).reshape(-1).astype(jnp.int32)

    x_sorted = _moe_permute(a, slot_tok, tm, tm)
    y_sorted = _moe_ffn(x_sorted, slot_gate, wg, wu, wd, group_expert.astype(jnp.int32), group_first, group_rows,
                        n_groups, tm, _pick_tile(wg.shape[2], 512, LANES))
    dest_pad = jnp.pad(dest.reshape(r, TOP_K), ((0, 0), (0, LANES - TOP_K)))
    return _moe_combine(h, y_sorted, eid_pad, dest_pad, win_start, g_final, n_main_rows, win)


def kernel(x_prompt, x_sample, cache_k0, cache_v0, state_conv0, cache_k1, cache_v1, state_hgrn1, page_table, meta_tokens, norm_mix, norm_ffn, norm_final, w_in0, lambda_q1, lambda_k1, lambda_q2, lambda_k2, subln_g0, conv_w0, conv_b0, conv_ln_g0, conv_ln_b0, w_out0, ffn_gate0, ffn_up0, ffn_down0, w_in1, hgrn_lb_logits, hgrn_norm_g1, w_out1, router1, moe_gate1, moe_up1, moe_down1):
    b, s, d = x_prompt.shape
    db, ds, _ = x_sample.shape
    n_pages = page_table.shape[1]
    page = cache_k0.shape[1]
    assert d == 2 * W_MIX and ds <= SAMPLE_ROWS and N_META + SAMPLE_ROWS * db <= TAIL_ROWS
    assert s % 256 == 0
    bs = b * s
    r = bs + TAIL_ROWS
    tm_proj = _pick_tile(r, 1664, 16)
    tm_out = _pick_tile(r, 1040, 16)
    tm_ffn = _pick_tile(r, 832, 16)
    tm_moe = _pick_tile(r, 768, LANES)
    tq = 1024 if s % 1024 == 0 else (512 if s % 512 == 0 else 256)

    xs = jnp.pad(x_sample, ((0, 0), (0, SAMPLE_ROWS - ds), (0, 0))).reshape(db * SAMPLE_ROWS, d)
    tail = jnp.concatenate(
        [meta_tokens.astype(F32), xs, jnp.zeros((TAIL_ROWS - N_META - db * SAMPLE_ROWS, d), F32)], axis=0)
    x_flat = jnp.concatenate([x_prompt.reshape(bs, d), tail], axis=0)

    pos_tail = jnp.concatenate([
        jnp.arange(N_META, dtype=jnp.int32),
        jnp.tile(n_pages * page + jnp.arange(SAMPLE_ROWS, dtype=jnp.int32), db),
        jnp.zeros((TAIL_ROWS - N_META - db * SAMPLE_ROWS,), jnp.int32)])
    tables = _rope_tables(jnp.concatenate([N_META + jnp.arange(s, dtype=jnp.int32), pos_tail]))
    lamp = jnp.stack([lambda_q1, lambda_k1, lambda_q2, lambda_k2]).astype(F32)
    g_sub = subln_g0.reshape(1, HEAD_DIM)

    def sample_rows(x):
        return x[N_META:N_META + db * SAMPLE_ROWS].reshape(db, SAMPLE_ROWS, N_HEADS, HEAD_DIM)[:, :ds]

    proj0 = _norm_matmul(x_flat, norm_mix[0], w_in0, tm_proj, 512)
    k0p, v0p, k0_tail = _kv_layout(proj0, b, s, tables)
    extra0 = (tables[0], tables[1], lamp, g_sub)
    mix0 = _attn_prompt(proj0, k0p, v0p, "diff", 0, b, s, tq, extra0)
    mix0 = _attn_tail(mix0, proj0, k0_tail, (0, 0), cache_k0, cache_v0, page_table, "diff", 0, b, s, ds, extra0)
    mix0, conv0_p = _conv_prompt(mix0, proj0, conv_w0, conv_b0, conv_ln_g0, conv_ln_b0, b, s, 128)
    mix0, conv0_s = _conv_tail(mix0, proj0, state_conv0, conv_w0, conv_b0, conv_ln_g0, conv_ln_b0, b, s, ds)
    h = _matmul_residual(mix0, w_out0, x_flat, tm_out, 512)
    h = _ffn(h, norm_ffn[0], ffn_gate0, ffn_up0, ffn_down0, tm_ffn, _pick_tile(ffn_gate0.shape[1], 512, LANES))

    proj1 = _norm_matmul(h, norm_mix[1], w_in1, tm_proj, 512)
    k1p, v1p = _kv_layout(proj1, b, s, None)
    tb = bs // TAIL_ROWS
    mix1 = _attn_prompt(proj1, k1p, v1p, "sb", 1, b, s, tq, None)
    mix1 = _attn_tail(mix1, proj1, proj1, (tb, 1), cache_k1, cache_v1, page_table, "sb", 1, b, s, ds, None)
    mix1, st_meta, hgrn_s = _hgrn_tail(mix1, proj1, hgrn_lb_logits, hgrn_norm_g1, state_hgrn1, b, s, ds)
    mix1, hgrn_p = _hgrn_prompt(mix1, proj1, hgrn_lb_logits, hgrn_norm_g1, st_meta, b, s, 256)
    h = _matmul_residual(mix1, w_out1, h, tm_out, 512)
    y_main, y_tail = _moe(h, norm_ffn[1], router1, moe_gate1, moe_up1, moe_down1, norm_final, bs, tm_moe)

    y_prompt = y_main.reshape(b, s, d)
    y_sample = y_tail[N_META:N_META + db * SAMPLE_ROWS].reshape(db, SAMPLE_ROWS, d)[:, :ds]
    heads = lambda x: x.reshape(b, N_META + s, N_HEADS, HEAD_DIM)
    tail0 = proj0[bs:]
    tail1 = proj1[bs:]
    return (y_prompt, y_sample,
            heads(k0p), heads(v0p), conv0_p, heads(k1p), heads(v1p), hgrn_p,
            sample_rows(k0_tail), sample_rows(tail0[:, 2 * W_MIX:3 * W_MIX]), conv0_s,
            sample_rows(tail1[:, W_MIX:2 * W_MIX]), sample_rows(tail1[:, 2 * W_MIX:3 * W_MIX]), hgrn_s)
```
